```python
import math, functools
import jax, jax.numpy as jnp
from jax import lax
import numpy as np

D_MODEL = 1024
BATCH = 4
SEQ = 4096
DEPTH = 1
DEC_BATCH = 128
DEC_SEQ = 8
PAST_LEN = 2048
PAGE_SIZE = 128

H_A = 8
HD_A = 64
W_A = H_A * HD_A
DECAY_LORA = 64
AAA_LORA = 64
GATE_LORA = 128
RWKV_SPLITS = [W_A, 2 * W_A, 3 * W_A, 3 * W_A + DECAY_LORA, 3 * W_A + DECAY_LORA + AAA_LORA]
RWKV_COLS = 3 * W_A + DECAY_LORA + AAA_LORA + GATE_LORA
H_B = 4
HD_B = 64
W_B = H_B * 2 * HD_B
DIFF_COLS = 3 * W_B
GATE_COLS = 2 * D_MODEL
N_COLS = RWKV_COLS + DIFF_COLS + GATE_COLS
D_FF = 4 * D_MODEL
Q_BLOCK = 128
NORM_EPS = 1e-6
GN_EPS = 64e-5
SUBLN_EPS = 1e-5
NEG = -1e30

kernel_name = "rwkv7_diffattn_gated_hybrid_step"


def rmsnorm(x, g, eps):
    xf = x.astype(jnp.float32)
    y = xf * lax.rsqrt(jnp.mean(xf * xf, axis=-1, keepdims=True) + eps)
    return (y * g.astype(jnp.float32)).astype(x.dtype)


def alibi_slopes():
    return 2.0 ** (-8.0 * jnp.arange(1, H_B + 1, dtype=jnp.float32) / H_B)


def rwkv7_scan(S0, r, w, k, v, a, b):
    xs = tuple(jnp.moveaxis(t.astype(jnp.float32), 1, 0) for t in (r, w, k, v, a, b))

    def step(S, inp):
        r_t, w_t, k_t, v_t, a_t, b_t = inp
        sa = jnp.einsum('bhij,bhj->bhi', S, a_t)
        S = S * w_t[:, :, None, :] + sa[..., None] * b_t[:, :, None, :] + v_t[..., None] * k_t[:, :, None, :]
        return S, jnp.einsum('bhij,bhj->bhi', S, r_t)

    S, ys = lax.scan(step, S0.astype(jnp.float32), xs)
    return S, jnp.moveaxis(ys, 0, 1)


def rwkv7_branch(Pr, S0, lp):
    B, T = Pr.shape[:2]
    f32 = jnp.float32
    r, k, v, wd, ad, gd = jnp.split(Pr, RWKV_SPLITS, axis=-1)
    w = -jax.nn.softplus(-(lp['w0'] + jnp.tanh(wd) @ lp['w2']).astype(f32)) - 0.5
    decay = jnp.exp(-jnp.exp(w))
    a = jax.nn.sigmoid((lp['a0'] + ad @ lp['a2']).astype(f32))
    g = jax.nn.sigmoid(gd) @ lp['g2']
    hs = lambda t: t.reshape(B, T, H_A, HD_A)
    kk = hs((k * lp['k_k']).astype(f32))
    kk = kk / jnp.maximum(jnp.linalg.norm(kk, axis=-1, keepdims=True), 1e-12)
    k = k.astype(f32) * (1.0 + (a - 1.0) * lp['k_a'].astype(f32))
    S, y = rwkv7_scan(S0, hs(r), hs(decay), hs(k), hs(v), -kk, kk * hs(a))
    mean = jnp.mean(y, axis=-1, keepdims=True)
    var = jnp.mean(jnp.square(y - mean), axis=-1, keepdims=True)
    y = (y - mean) * lax.rsqrt(var + GN_EPS) * lp['ln_x_w'].astype(f32).reshape(H_A, HD_A) \
        + lp['ln_x_b'].astype(f32).reshape(H_A, HD_A)
    bonus = jnp.sum(hs(r).astype(f32) * hs(k) * lp['r_k'].astype(f32), axis=-1, keepdims=True) * hs(v).astype(f32)
    o = (y + bonus).reshape(B, T, W_A).astype(Pr.dtype) * g
    return o, S


def diff_attend(q, q_pos, segs, lam, slopes):
    scale = HD_B ** -0.5
    scores = []
    for k, _, k_pos in segs:
        s = jnp.einsum('bqhcd,bkhcd->bhcqk', q, k).astype(jnp.float32) * scale
        dist = q_pos[:, None] - k_pos[None, :]
        bias = -slopes[:, None, None] * dist.astype(jnp.float32)
        scores.append(jnp.where(dist >= 0, s + bias[None, :, None], NEG))
    p = jax.nn.softmax(jnp.concatenate(scores, axis=-1), axis=-1)
    attn = p[:, :, 0] - lam * p[:, :, 1]
    out = None
    start = 0
    for _, v, _ in segs:
        n = v.shape[1]
        o = jnp.einsum('bhqk,bkhe->bqhe', attn[..., start:start + n].astype(v.dtype), v)
        out = o if out is None else out + o
        start += n
    return out


def prompt_attend(q, k, v, lam, slopes):
    B, S = q.shape[:2]
    nb = S // Q_BLOCK
    qb = jnp.moveaxis(q.reshape(B, nb, Q_BLOCK, H_B, 2, HD_B), 1, 0)
    k_pos = jnp.arange(S)

    def blk(args):
        qi, i = args
        q_pos = i * Q_BLOCK + jnp.arange(Q_BLOCK)
        return diff_attend(qi, q_pos, ((k, v, k_pos),), lam, slopes)

    out = lax.map(blk, (qb, jnp.arange(nb)))
    return jnp.moveaxis(out, 0, 1).reshape(B, S, H_B, 2 * HD_B)


def sample_attend(q, k, v, lam, k_cache, v_cache, page_table, slopes):
    Bd, T = q.shape[:2]
    past = page_table.shape[1] * PAGE_SIZE
    kp = k_cache[page_table].reshape(Bd, past, H_B, 2, HD_B)
    vp = v_cache[page_table].reshape(Bd, past, H_B, 2 * HD_B)
    q_pos = past + jnp.arange(T)
    segs = ((kp, vp, jnp.arange(past)), (k, v, q_pos))
    return diff_attend(q, q_pos, segs, lam, slopes)


def layer_forward(x, prev_row, S0, attend, lam_init, lp):
    B, T, _ = x.shape
    f32 = jnp.float32
    xn = rmsnorm(x, lp['g_mix'], NORM_EPS)
    P = xn @ lp['w_in']
    Pr = P[..., :RWKV_COLS]
    first_prev = (prev_row.astype(x.dtype) @ lp['w_in'][:, :RWKV_COLS])[:, None]
    Pprev = jnp.concatenate([first_prev, Pr[:, :-1]], axis=1)
    Pr = Pr + (Pprev - Pr) * lp['mu_shift']
    o_a, S = rwkv7_branch(Pr, S0, lp)
    off = RWKV_COLS
    q = P[..., off:off + W_B].reshape(B, T, H_B, 2, HD_B)
    k = P[..., off + W_B:off + 2 * W_B].reshape(B, T, H_B, 2, HD_B)
    v = P[..., off + 2 * W_B:off + 3 * W_B].reshape(B, T, H_B, 2 * HD_B)
    lam = (jnp.exp(jnp.sum(lp['lam_q1'].astype(f32) * lp['lam_k1'].astype(f32)))
           - jnp.exp(jnp.sum(lp['lam_q2'].astype(f32) * lp['lam_k2'].astype(f32))) + lam_init)
    att = attend(q, k, v, lam)
    o_b = (rmsnorm(att, lp['subln_g'], SUBLN_EPS) * (1.0 - lam_init)).reshape(B, T, W_B)
    off2 = RWKV_COLS + DIFF_COLS
    gate_a = jax.nn.sigmoid(P[..., off2:off2 + D_MODEL])
    gate_b = jax.nn.sigmoid(P[..., off2 + D_MODEL:off2 + 2 * D_MODEL])
    m = gate_a * (o_a @ lp['w_br_a']) + gate_b * (o_b @ lp['w_br_b'])
    x = x + m @ lp['w_out']
    hn = rmsnorm(x, lp['g_ffn'], NORM_EPS)
    x = x + jnp.square(jax.nn.relu(hn @ lp['w_up'])) @ lp['w_down']
    return x, S, k.reshape(B, T, H_B, 2 * HD_B), v, xn[:, -1]


def setup_inputs(seed: int = 0) -> dict:
    key = jax.random.key(seed)
    ks = iter(jax.random.split(key, 40))
    f32 = jnp.float32
    L = DEPTH
    n_pages = PAST_LEN // PAGE_SIZE
    n_used = DEC_BATCH * n_pages
    n_pool = n_used + n_used // 4

    def nrm(shape, scale):
        return scale * jax.random.normal(next(ks), shape, f32)

    x_prompt = nrm((BATCH, SEQ, D_MODEL), 1.0)
    x_sample = nrm((DEC_BATCH, DEC_SEQ, D_MODEL), 1.0)
    cache_k = nrm((L, n_pool, PAGE_SIZE, H_B, 2 * HD_B), 1.0)
    cache_v = nrm((L, n_pool, PAGE_SIZE, H_B, 2 * HD_B), 1.0)
    state_rwkv = nrm((L, DEC_BATCH, H_A, HD_A, HD_A), 0.5)
    state_shift = nrm((L, DEC_BATCH, D_MODEL), 1.0)
    page_table = jax.random.permutation(next(ks), n_pool)[:n_used].reshape(DEC_BATCH, n_pages).astype(jnp.int32)
    w_in = nrm((L, D_MODEL, N_COLS), D_MODEL ** -0.5)
    mu_shift = jax.random.uniform(next(ks), (L, RWKV_COLS), f32, 0.0, 1.0)
    w0 = jax.random.uniform(next(ks), (L, W_A), f32, -6.5, -1.5)
    w2 = nrm((L, DECAY_LORA, W_A), 0.1)
    a0 = nrm((L, W_A), 0.1)
    a2 = nrm((L, AAA_LORA, W_A), AAA_LORA ** -0.5)
    g2 = nrm((L, GATE_LORA, W_A), GATE_LORA ** -0.5)
    k_k = 0.85 + nrm((L, W_A), 0.02)
    k_a = 1.0 + nrm((L, W_A), 0.02)
    r_k = nrm((L, H_A, HD_A), 0.1)
    ln_x_w = 1.0 + nrm((L, W_A), 0.05)
    ln_x_b = nrm((L, W_A), 0.02)
    lam_q1 = nrm((L, HD_B), 0.1)
    lam_k1 = nrm((L, HD_B), 0.1)
    lam_q2 = nrm((L, HD_B), 0.1)
    lam_k2 = nrm((L, HD_B), 0.1)
    subln_g = 1.0 + nrm((L, 2 * HD_B), 0.05)
    w_br_a = nrm((L, W_A, D_MODEL), W_A ** -0.5)
    w_br_b = nrm((L, W_B, D_MODEL), W_B ** -0.5)
    w_out = nrm((L, D_MODEL, D_MODEL), D_MODEL ** -0.5)
    g_mix = 1.0 + nrm((L, D_MODEL), 0.05)
    g_ffn = 1.0 + nrm((L, D_MODEL), 0.05)
    w_up = nrm((L, D_MODEL, D_FF), D_MODEL ** -0.5)
    w_down = nrm((L, D_FF, D_MODEL), D_FF ** -0.5)
    g_final = 1.0 + nrm((D_MODEL,), 0.05)
    return {'x_prompt': x_prompt, 'x_sample': x_sample, 'cache_k': cache_k, 'cache_v': cache_v,
            'state_rwkv': state_rwkv, 'state_shift': state_shift, 'page_table': page_table,
            'w_in': w_in, 'mu_shift': mu_shift, 'w0': w0, 'w2': w2, 'a0': a0, 'a2': a2, 'g2': g2,
            'k_k': k_k, 'k_a': k_a, 'r_k': r_k, 'ln_x_w': ln_x_w, 'ln_x_b': ln_x_b,
            'lam_q1': lam_q1, 'lam_k1': lam_k1, 'lam_q2': lam_q2, 'lam_k2': lam_k2, 'subln_g': subln_g,
            'w_br_a': w_br_a, 'w_br_b': w_br_b, 'w_out': w_out, 'g_mix': g_mix, 'g_ffn': g_ffn,
            'w_up': w_up, 'w_down': w_down, 'g_final': g_final}


def reference(x_prompt, x_sample, cache_k, cache_v, state_rwkv, state_shift, page_table,
              w_in, mu_shift, w0, w2, a0, a2, g2, k_k, k_a, r_k, ln_x_w, ln_x_b,
              lam_q1, lam_k1, lam_q2, lam_k2, subln_g, w_br_a, w_br_b, w_out,
              g_mix, g_ffn, w_up, w_down, g_final):
    weights = dict(w_in=w_in, mu_shift=mu_shift, w0=w0, w2=w2, a0=a0, a2=a2, g2=g2,
                   k_k=k_k, k_a=k_a, r_k=r_k, ln_x_w=ln_x_w, ln_x_b=ln_x_b,
                   lam_q1=lam_q1, lam_k1=lam_k1, lam_q2=lam_q2, lam_k2=lam_k2, subln_g=subln_g,
                   w_br_a=w_br_a, w_br_b=w_br_b, w_out=w_out, g_mix=g_mix, g_ffn=g_ffn,
                   w_up=w_up, w_down=w_down)
    slopes = alibi_slopes()
    B = x_prompt.shape[0]
    hp, hs = x_prompt, x_sample
    kps, vps, kss, vss, sps, sss, rps, rss = [], [], [], [], [], [], [], []
    for l in range(DEPTH):
        lp = {name: arr[l] for name, arr in weights.items()}
        lam_init = 0.8 - 0.6 * math.exp(-0.3 * l)
        p_attn = functools.partial(prompt_attend, slopes=slopes)
        s_attn = functools.partial(sample_attend, k_cache=cache_k[l], v_cache=cache_v[l],
                                   page_table=page_table, slopes=slopes)
        hp, Sp, kp, vp, rp = layer_forward(hp, jnp.zeros((B, D_MODEL), x_prompt.dtype),
                                           jnp.zeros((B, H_A, HD_A, HD_A), jnp.float32),
                                           p_attn, lam_init, lp)
        hs, Ss, ksm, vsm, rsm = layer_forward(hs, state_shift[l], state_rwkv[l], s_attn, lam_init, lp)
        kps.append(kp); vps.append(vp); kss.append(ksm); vss.append(vsm)
        sps.append(Sp); sss.append(Ss); rps.append(rp); rss.append(rsm)
    y_prompt = rmsnorm(hp, g_final, NORM_EPS)
    y_sample = rmsnorm(hs, g_final, NORM_EPS)
    return (y_prompt, y_sample, jnp.stack(kps), jnp.stack(vps), jnp.stack(kss), jnp.stack(vss),
            jnp.stack(sps), jnp.stack(sss), jnp.stack(rps), jnp.stack(rss))
```

```python
import functools

import jax
import jax.numpy as jnp
from jax import lax
from jax.experimental import pallas as pl
from jax.experimental.pallas import tpu as pltpu

F32 = jnp.float32
BF16 = jnp.bfloat16

D_MODEL = 1024
H_A = 8
HD_A = 64
W_A = H_A * HD_A
DECAY_LORA = 64
AAA_LORA = 64
GATE_LORA = 128
RWKV_COLS = 3 * W_A + DECAY_LORA + AAA_LORA + GATE_LORA
H_B = 4
HD_B = 64
W_B = H_B * 2 * HD_B
GATE_COLS = 2 * D_MODEL
N_COLS = RWKV_COLS + 3 * W_B + GATE_COLS
D_FF = 4 * D_MODEL
PAGE_SIZE = 128
NORM_EPS = 1e-6
GN_EPS = 64e-5
SUBLN_EPS = 1e-5
NEG = -1e30
LAM_INIT = 0.8 - 0.6 * 1.0

LANES = 128
SUBLANES = 8
VMEM_LIMIT = 56 * 1024 * 1024

OFF_Q = RWKV_COLS
OFF_K = OFF_Q + W_B
OFF_V = OFF_K + W_B
OFF_GA = OFF_V + W_B
OFF_GB = OFF_GA + D_MODEL


def _params(sem):
    return pltpu.CompilerParams(dimension_semantics=sem, vmem_limit_bytes=VMEM_LIMIT)


def _const_spec(shape):
    nd = len(shape)
    return pl.BlockSpec(shape, lambda *_: (0,) * nd, pipeline_mode=pl.Buffered(1))


def _rms(x, g, eps):
    return x * lax.rsqrt(jnp.mean(x * x, axis=-1, keepdims=True) + eps) * g


def _mm(a, w):
    return jnp.dot(a.astype(BF16), w, preferred_element_type=F32)


def _head_ones(width, head):
    r = lax.broadcasted_iota(jnp.int32, (width, width), 0) // head
    c = lax.broadcasted_iota(jnp.int32, (width, width), 1) // head
    return (r == c).astype(BF16)


def _head_sum(x, ones):
    hi = x.astype(BF16)
    r1 = x - hi.astype(F32)
    mid = r1.astype(BF16)
    lo = (r1 - mid.astype(F32)).astype(BF16)
    d = lambda a: jnp.dot(a, ones, preferred_element_type=F32)
    return d(hi) + d(mid) + d(lo)


def _proj_tail(xn, w_ref, q_ref, k_ref, v_ref, gate_ref):
    xb = xn.astype(BF16)
    q_ref[...] = jnp.dot(xb, w_ref[:, OFF_Q:OFF_K], preferred_element_type=F32)
    k_ref[...] = jnp.dot(xb, w_ref[:, OFF_K:OFF_V], preferred_element_type=F32)
    v_ref[...] = jnp.dot(xb, w_ref[:, OFF_V:OFF_GA], preferred_element_type=F32)
    gate_ref[...] = jax.nn.sigmoid(
        jnp.dot(xb, w_ref[:, OFF_GA:N_COLS], preferred_element_type=F32))
    return xb


def _inproj_prompt_kernel(x_ref, g_ref, w_ref, mu_ref,
                          pr_ref, q_ref, k_ref, v_ref, gate_ref, xl_ref, carry_ref):
    tm = x_ref.shape[0]
    xn = _rms(x_ref[...], g_ref[...], NORM_EPS)
    xl_ref[0] = xn[tm - 1:tm, :]
    xb = _proj_tail(xn, w_ref, q_ref, k_ref, v_ref, gate_ref)
    pr = jnp.dot(xb, w_ref[:, 0:RWKV_COLS], preferred_element_type=F32)

    @pl.when(pl.program_id(1) == 0)
    def _():
        carry_ref[...] = jnp.zeros_like(carry_ref)

    prev = pltpu.roll(pr, 1, 0)
    row = lax.broadcasted_iota(jnp.int32, pr.shape, 0)
    prev = jnp.where(row == 0, carry_ref[SUBLANES - 1:SUBLANES, :], prev)
    carry_ref[...] = pr[tm - SUBLANES:tm, :]
    pr_ref[...] = pr + (prev - pr) * mu_ref[...]


def _inproj_prompt(x2d, seq, g_mix, w_in, mu):
    m = x2d.shape[0]
    nb = m // seq
    tm = min(256, seq)
    nt = seq // tm
    row_spec = lambda w: pl.BlockSpec((tm, w), lambda b, t: (b * nt + t, 0))
    return pl.pallas_call(
        _inproj_prompt_kernel,
        grid=(nb, nt),
        in_specs=[row_spec(D_MODEL), _const_spec((1, D_MODEL)),
                  _const_spec((D_MODEL, N_COLS)), _const_spec((1, RWKV_COLS))],
        out_specs=[row_spec(RWKV_COLS), row_spec(W_B), row_spec(W_B), row_spec(W_B),
                   row_spec(GATE_COLS),
                   pl.BlockSpec((1, 1, D_MODEL), lambda b, t: (b, 0, 0))],
        out_shape=[jax.ShapeDtypeStruct((m, RWKV_COLS), F32),
                   jax.ShapeDtypeStruct((m, W_B), F32),
                   jax.ShapeDtypeStruct((m, W_B), F32),
                   jax.ShapeDtypeStruct((m, W_B), F32),
                   jax.ShapeDtypeStruct((m, GATE_COLS), F32),
                   jax.ShapeDtypeStruct((nb, 1, D_MODEL), F32)],
        scratch_shapes=[pltpu.VMEM((SUBLANES, RWKV_COLS), F32)],
        compiler_params=_params(("parallel", "arbitrary")),
        name="inproj_prompt",
    )(x2d, g_mix, w_in, mu)


def _inproj_sample_kernel(x_ref, sp_ref, g_ref, w_ref, mu_ref,
                          pr_ref, q_ref, k_ref, v_ref, gate_ref, xn_ref, *, seq):
    xn = _rms(x_ref[...], g_ref[...], NORM_EPS)
    xn_ref[...] = xn
    xb = _proj_tail(xn, w_ref, q_ref, k_ref, v_ref, gate_ref)
    row = lax.broadcasted_iota(jnp.int32, xn.shape, 0)
    xprev = jnp.where(row % seq == 0, sp_ref[...], pltpu.roll(xn, 1, 0))
    w_r = w_ref[:, 0:RWKV_COLS]
    pr = jnp.dot(xb, w_r, preferred_element_type=F32)
    prev = jnp.dot(xprev.astype(BF16), w_r, preferred_element_type=F32)
    pr_ref[...] = pr + (prev - pr) * mu_ref[...]


def _inproj_sample(x2d, seq, shift_pad, g_mix, w_in, mu):
    m = x2d.shape[0]
    tm = min(256, m)
    assert tm % seq == 0 and m % tm == 0
    row_spec = lambda w: pl.BlockSpec((tm, w), lambda i: (i, 0))
    return pl.pallas_call(
        functools.partial(_inproj_sample_kernel, seq=seq),
        grid=(m // tm,),
        in_specs=[row_spec(D_MODEL), row_spec(D_MODEL), _const_spec((1, D_MODEL)),
                  _const_spec((D_MODEL, N_COLS)), _const_spec((1, RWKV_COLS))],
        out_specs=[row_spec(RWKV_COLS), row_spec(W_B), row_spec(W_B), row_spec(W_B),
                   row_spec(GATE_COLS), row_spec(D_MODEL)],
        out_shape=[jax.ShapeDtypeStruct((m, RWKV_COLS), F32),
                   jax.ShapeDtypeStruct((m, W_B), F32),
                   jax.ShapeDtypeStruct((m, W_B), F32),
                   jax.ShapeDtypeStruct((m, W_B), F32),
                   jax.ShapeDtypeStruct((m, GATE_COLS), F32),
                   jax.ShapeDtypeStruct((m, D_MODEL), F32)],
        compiler_params=_params(("parallel",)),
        name="inproj_sample",
    )(x2d, shift_pad, g_mix, w_in, mu)


def _dot_f32(a, b):
    return jnp.dot(a, b, precision=lax.Precision.HIGHEST, preferred_element_type=F32)


def _rwkv_pre_kernel(pr_ref, w0_ref, w2_ref, a0_ref, a2_ref, g2_ref, kk_ref, ka_ref,
                     dec_ref, k_ref, a_ref, b_ref, g_ref):
    o_wd = 3 * W_A
    o_ad = o_wd + DECAY_LORA
    o_gd = o_ad + AAA_LORA
    k = pr_ref[:, W_A:2 * W_A]
    wd = pr_ref[:, o_wd:o_ad]
    ad = pr_ref[:, o_ad:o_gd]
    gd = pr_ref[:, o_gd:RWKV_COLS]
    z = -(w0_ref[...] + _dot_f32(jnp.tanh(wd), w2_ref[...]))
    softplus = jnp.maximum(z, 0.0) + jnp.log1p(jnp.exp(-jnp.abs(z)))
    w = -softplus - 0.5
    dec_ref[...] = jnp.exp(-jnp.exp(w))
    a = jax.nn.sigmoid(a0_ref[...] + _dot_f32(ad, a2_ref[...]))
    g_ref[...] = _dot_f32(jax.nn.sigmoid(gd), g2_ref[...])
    kk = k * kk_ref[...]
    ones = _head_ones(W_A, HD_A)
    norm = jnp.sqrt(_head_sum(kk * kk, ones))
    kk = kk / jnp.maximum(norm, 1e-12)
    k_ref[...] = k * (1.0 + (a - 1.0) * ka_ref[...])
    a_ref[...] = -kk
    b_ref[...] = kk * a


def _rwkv_pre(pr, w0, w2, a0, a2, g2, k_k, k_a):
    m = pr.shape[0]
    tm = min(256, m)
    row_spec = lambda w: pl.BlockSpec((tm, w), lambda i: (i, 0))
    full = lambda a: _const_spec(a.shape)
    out = jax.ShapeDtypeStruct((m, W_A), F32)
    return pl.pallas_call(
        _rwkv_pre_kernel,
        grid=(m // tm,),
        in_specs=[row_spec(RWKV_COLS), full(w0), full(w2), full(a0), full(a2), full(g2),
                  full(k_k), full(k_a)],
        out_specs=[row_spec(W_A)] * 5,
        out_shape=[out] * 5,
        compiler_params=_params(("parallel",)),
        name="rwkv_pre",
    )(pr, w0, w2, a0, a2, g2, k_k, k_a)


def _sublane_allsum(x):
    n = x.shape[0] // SUBLANES
    acc = x[0:SUBLANES]
    for g in range(1, n):
        acc = acc + x[g * SUBLANES:(g + 1) * SUBLANES]
    shift = SUBLANES // 2
    while shift >= 1:
        acc = acc + pltpu.roll(acc, shift, 0)
        shift //= 2
    return acc


def _rwkv_scan_kernel(w_ref, a_ref, b_ref, k_ref, r_ref, v_ref, s0_ref, y_ref, s_ref):
    tc = w_ref.shape[0]
    nkey = w_ref.shape[1]
    ni = v_ref.shape[1]
    ng = nkey // SUBLANES

    @pl.when(pl.program_id(1) == 0)
    def _():
        s_ref[...] = s0_ref[...]

    sub = lax.broadcasted_iota(jnp.int32, (SUBLANES, LANES), 0)

    def tile8(x):
        return jnp.concatenate([x] * ng, axis=0)

    def step(t, carry):
        def group(ig, carry2):
            y_acc = jnp.zeros((SUBLANES, LANES), F32)
            for ii in range(SUBLANES):
                i = ig * SUBLANES + ii
                s_i = s_ref[i]
                sa = _sublane_allsum(s_i * a_ref[t])
                v_i = jnp.broadcast_to(v_ref[t, pl.ds(i, 1), :], (SUBLANES, LANES))
                s_new = s_i * w_ref[t] + tile8(sa) * b_ref[t] + tile8(v_i) * k_ref[t]
                s_ref[i] = s_new
                y_i = _sublane_allsum(s_new * r_ref[t])
                y_acc = jnp.where(sub == ii, y_i, y_acc)
            y_ref[t, pl.ds(pl.multiple_of(ig * SUBLANES, SUBLANES), SUBLANES), :] = y_acc
            return carry2

        lax.fori_loop(0, ni // SUBLANES, group, 0)
        return carry

    lax.fori_loop(0, tc, step, 0)


def _rwkv_scan(w, a, b, k, r, v, s0, tc):
    g, t, nkey, _ = w.shape
    ni = v.shape[2]
    x_spec = pl.BlockSpec((None, tc, nkey, LANES), lambda gi, ti: (gi, ti, 0, 0))
    v_spec = pl.BlockSpec((None, tc, ni, LANES), lambda gi, ti: (gi, ti, 0, 0))
    s_spec = pl.BlockSpec((None, ni, nkey, LANES), lambda gi, ti: (gi, 0, 0, 0))
    return pl.pallas_call(
        _rwkv_scan_kernel,
        grid=(g, t // tc),
        in_specs=[x_spec] * 5 + [v_spec, s_spec],
        out_specs=[v_spec, s_spec],
        out_shape=[jax.ShapeDtypeStruct((g, t, ni, LANES), F32),
                   jax.ShapeDtypeStruct((g, ni, nkey, LANES), F32)],
        compiler_params=_params(("parallel", "arbitrary")),
        name="rwkv_scan",
    )(w, a, b, k, r, v, s0)


def _rwkv_post_kernel(y_ref, pr_ref, k_ref, g_ref, lnw_ref, lnb_ref, rk_ref, o_ref):
    ones = _head_ones(W_A, HD_A)
    y = y_ref[...]
    r = pr_ref[:, 0:W_A]
    v = pr_ref[:, 2 * W_A:3 * W_A]
    mean = _head_sum(y, ones) * (1.0 / HD_A)
    d = y - mean
    var = _head_sum(d * d, ones) * (1.0 / HD_A)
    yn = d * lax.rsqrt(var + GN_EPS) * lnw_ref[...] + lnb_ref[...]
    bonus = _head_sum(r * k_ref[...] * rk_ref[...], ones) * v
    o_ref[...] = (yn + bonus) * g_ref[...]


def _rwkv_post(y, pr, k_mod, g, ln_w, ln_b, r_k):
    m = y.shape[0]
    tm = min(256, m)
    row_spec = lambda w: pl.BlockSpec((tm, w), lambda i: (i, 0))
    vec = _const_spec((1, W_A))
    return pl.pallas_call(
        _rwkv_post_kernel,
        grid=(m // tm,),
        in_specs=[row_spec(W_A), row_spec(RWKV_COLS), row_spec(W_A), row_spec(W_A),
                  vec, vec, vec],
        out_specs=row_spec(W_A),
        out_shape=jax.ShapeDtypeStruct((m, W_A), F32),
        compiler_params=_params(("parallel",)),
        name="rwkv_post",
    )(y, pr, k_mod, g, ln_w, ln_b, r_k)


def _lam_value(lam_ref):
    lv = lam_ref[...]
    s1 = jnp.sum(lv[0:1] * lv[1:2], axis=-1, keepdims=True)
    s2 = jnp.sum(lv[2:3] * lv[3:4], axis=-1, keepdims=True)
    return jnp.exp(s1) - jnp.exp(s2) + LAM_INIT


def _attn_prompt_kernel(slope_ref, q_ref, k_ref, v_ref, lam_ref, sg_ref, o_ref,
                        m_ref, l_ref, acc_ref):
    h = pl.program_id(1)
    qi = pl.program_id(2)
    ki = pl.program_id(3)
    tq = q_ref.shape[0]
    tk = k_ref.shape[0]

    @pl.when(ki == 0)
    def _():
        m_ref[...] = jnp.full_like(m_ref, NEG)
        l_ref[...] = jnp.zeros_like(l_ref)
        acc_ref[...] = jnp.zeros_like(acc_ref)

    @pl.when(ki <= qi)
    def _():
        slope = slope_ref[h]
        q = (q_ref[...] * (HD_B ** -0.5)).astype(BF16)
        k = k_ref[...].astype(BF16)
        v = v_ref[...].astype(BF16)
        dist = (qi * tq + lax.broadcasted_iota(jnp.int32, (tq, tk), 0)
                - ki * tk - lax.broadcasted_iota(jnp.int32, (tq, tk), 1))
        bias = -slope * dist.astype(F32)
        for c in range(2):
            s = lax.dot_general(q[:, c * HD_B:(c + 1) * HD_B], k[:, c * HD_B:(c + 1) * HD_B],
                                (((1,), (1,)), ((), ())), preferred_element_type=F32)
            s = jnp.where(dist >= 0, s + bias, NEG)
            m_old = m_ref[c]
            m_new = jnp.maximum(m_old, jnp.max(s, axis=-1, keepdims=True))
            alpha = jnp.exp(m_old - m_new)
            p = jnp.exp(s - m_new)
            l_ref[c] = alpha * l_ref[c] + jnp.sum(p, axis=-1, keepdims=True)
            acc_ref[c] = alpha * acc_ref[c] + jnp.dot(p.astype(BF16), v,
                                                      preferred_element_type=F32)
            m_ref[c] = m_new

    @pl.when(ki == qi)
    def _():
        lam = _lam_value(lam_ref)
        att = acc_ref[0] / l_ref[0] - lam * (acc_ref[1] / l_ref[1])
        o_ref[...] = _rms(att, sg_ref[...], SUBLN_EPS) * (1.0 - LAM_INIT)


def _attn_prompt(q, k, v, seq, slopes, lam_rows, subln_g):
    m = q.shape[0]
    nb = m // seq
    tq = min(512, seq)
    nq = seq // tq
    width = 2 * HD_B
    q_spec = pl.BlockSpec((tq, width), lambda b, h, qi, ki: (b * nq + qi, h))
    kv_spec = pl.BlockSpec((tq, width),
                           lambda b, h, qi, ki: (b * nq + jnp.minimum(ki, qi), h))
    return pl.pallas_call(
        _attn_prompt_kernel,
        grid=(nb, H_B, nq, nq),
        in_specs=[pl.BlockSpec(memory_space=pltpu.SMEM), q_spec, kv_spec, kv_spec,
                  pl.BlockSpec((4, HD_B), lambda *_: (0, 0)),
                  pl.BlockSpec((1, width), lambda *_: (0, 0))],
        out_specs=q_spec,
        out_shape=jax.ShapeDtypeStruct((m, W_B), F32),
        scratch_shapes=[pltpu.VMEM((2, tq, 1), F32), pltpu.VMEM((2, tq, 1), F32),
                        pltpu.VMEM((2, tq, width), F32)],
        compiler_params=_params(("parallel", "parallel", "parallel", "arbitrary")),
        name="attn_prompt",
    )(slopes, q, k, v, lam_rows, subln_g)


def _attn_sample_kernel(pt_ref, q_ref, kn_ref, vn_ref, lam_ref, sg_ref, *rest,
                        n_pages, past):
    del pt_ref
    k_pages = rest[:n_pages]
    v_pages = rest[n_pages:2 * n_pages]
    o_ref, s_ref = rest[2 * n_pages], rest[2 * n_pages + 1]
    tq = q_ref.shape[0]
    rows = 2 * H_B * tq
    nt = (((1,), (1,)), ((), ()))

    q = q_ref[...] * (HD_B ** -0.5)
    q_rep = jnp.concatenate([q] * (2 * H_B), axis=0)
    r_i = lax.broadcasted_iota(jnp.int32, (rows, W_B), 0)
    c_i = lax.broadcasted_iota(jnp.int32, (rows, W_B), 1)
    q_blk = jnp.where(r_i // tq == c_i // HD_B, q_rep, 0.0).astype(BF16)

    row = lax.broadcasted_iota(jnp.int32, (rows, PAGE_SIZE), 0)
    lane = lax.broadcasted_iota(jnp.int32, (rows, PAGE_SIZE), 1)
    head = row // (2 * tq)
    slope = jnp.exp2(-8.0 * (head + 1).astype(F32) / H_B)
    q_pos = past + row % tq

    for p in range(n_pages):
        s = lax.dot_general(q_blk, k_pages[p][...].astype(BF16), nt,
                            preferred_element_type=F32)
        dist = q_pos - (p * PAGE_SIZE + lane)
        s_ref[:, p * PAGE_SIZE:(p + 1) * PAGE_SIZE] = s - slope * dist.astype(F32)

    kn = jnp.concatenate(
        [kn_ref[...], jnp.zeros((PAGE_SIZE - tq, W_B), F32)], axis=0).astype(BF16)
    s = lax.dot_general(q_blk, kn, nt, preferred_element_type=F32)
    dist = q_pos - (past + lane)
    s = jnp.where((dist >= 0) & (lane < tq), s - slope * dist.astype(F32), NEG)
    s_ref[:, past:past + PAGE_SIZE] = s

    s = s_ref[...]
    mx = jnp.max(s, axis=-1, keepdims=True)
    p_un = jnp.exp(s - mx)
    pn = p_un / jnp.sum(p_un, axis=-1, keepdims=True)
    lam = _lam_value(lam_ref)
    attn = jnp.concatenate(
        [pn[(2 * h) * tq:(2 * h + 1) * tq] - lam * pn[(2 * h + 1) * tq:(2 * h + 2) * tq]
         for h in range(H_B)], axis=0).astype(BF16)

    acc = jnp.zeros((H_B * tq, W_B), F32)
    for p in range(n_pages):
        acc = acc + jnp.dot(attn[:, p * PAGE_SIZE:(p + 1) * PAGE_SIZE],
                            v_pages[p][...].astype(BF16), preferred_element_type=F32)
    vn = jnp.concatenate(
        [vn_ref[...], jnp.zeros((PAGE_SIZE - tq, W_B), F32)], axis=0).astype(BF16)
    acc = acc + jnp.dot(attn[:, past:past + PAGE_SIZE], vn, preferred_element_type=F32)

    width = 2 * HD_B
    outs = []
    for h in range(H_B):
        att = acc[h * tq:(h + 1) * tq, h * width:(h + 1) * width]
        outs.append(_rms(att, sg_ref[...], SUBLN_EPS) * (1.0 - LAM_INIT))
    o_ref[...] = jnp.concatenate(outs, axis=-1)


def _attn_sample(q, k_new, v_new, cache_k, cache_v, page_table, lam_rows, subln_g):
    bd, tq, _ = q.shape
    n_pages = page_table.shape[1]
    past = n_pages * PAGE_SIZE
    tok_spec = pl.BlockSpec((None, tq, W_B), lambda b, pt: (b, 0, 0))
    page_specs = [pl.BlockSpec((None, PAGE_SIZE, W_B),
                               lambda b, pt, p=p: (pt[b * n_pages + p], 0, 0))
                  for p in range(n_pages)]
    grid_spec = pltpu.PrefetchScalarGridSpec(
        num_scalar_prefetch=1,
        grid=(bd,),
        in_specs=[tok_spec, tok_spec, tok_spec,
                  pl.BlockSpec((4, HD_B), lambda b, pt: (0, 0)),
                  pl.BlockSpec((1, 2 * HD_B), lambda b, pt: (0, 0))]
                 + page_specs + page_specs,
        out_specs=tok_spec,
        scratch_shapes=[pltpu.VMEM((2 * H_B * tq, past + PAGE_SIZE), F32)],
    )
    return pl.pallas_call(
        functools.partial(_attn_sample_kernel, n_pages=n_pages, past=past),
        grid_spec=grid_spec,
        out_shape=jax.ShapeDtypeStruct((bd, tq, W_B), F32),
        compiler_params=_params(("parallel",)),
        name="attn_sample",
    )(page_table.reshape(-1), q, k_new, v_new, lam_rows, subln_g,
      *([cache_k] * n_pages), *([cache_v] * n_pages))


def _merge_ffn_kernel(x_ref, oa_ref, ob_ref, gate_ref, wa_ref, wb_ref, wo_ref,
                      gf_ref, wu_ref, wd_ref, gl_ref, y_ref):
    ga = gate_ref[:, 0:D_MODEL]
    gb = gate_ref[:, D_MODEL:GATE_COLS]
    m = ga * _mm(oa_ref[...], wa_ref[...]) + gb * _mm(ob_ref[...], wb_ref[...])
    x1 = x_ref[...] + _mm(m, wo_ref[...])
    hn = _rms(x1, gf_ref[...], NORM_EPS).astype(BF16)
    acc = x1
    chunk = D_MODEL
    for c in range(D_FF // chunk):
        up = jnp.dot(hn, wu_ref[:, c * chunk:(c + 1) * chunk], preferred_element_type=F32)
        act = jnp.square(jnp.maximum(up, 0.0))
        acc = acc + _mm(act, wd_ref[c * chunk:(c + 1) * chunk, :])
    y_ref[...] = _rms(acc, gl_ref[...], NORM_EPS)


def _merge_ffn(x2d, o_a, o_b, gates, w_br_a, w_br_b, w_out, g_ffn, w_up, w_down, g_final):
    m = x2d.shape[0]
    tm = min(256, m)
    row_spec = lambda w: pl.BlockSpec((tm, w), lambda i: (i, 0))
    full = lambda a: _const_spec(a.shape)
    return pl.pallas_call(
        _merge_ffn_kernel,
        grid=(m // tm,),
        in_specs=[row_spec(D_MODEL), row_spec(W_A), row_spec(W_B), row_spec(GATE_COLS),
                  full(w_br_a), full(w_br_b), full(w_out), full(g_ffn), full(w_up),
                  full(w_down), full(g_final)],
        out_specs=row_spec(D_MODEL),
        out_shape=jax.ShapeDtypeStruct((m, D_MODEL), F32),
        compiler_params=_params(("parallel",)),
        name="merge_ffn",
    )(x2d, o_a, o_b, gates, w_br_a, w_br_b, w_out, g_ffn, w_up, w_down, g_final)


def _scan_prompt_layout(x, nb, seq, rep):
    x = x.reshape(nb, seq, H_A, HD_A).transpose(1, 3, 0, 2)
    x = jnp.broadcast_to(x[..., None], (seq, HD_A, nb, H_A, rep))
    return x.reshape(1, seq, HD_A, nb * H_A * rep)


def _rwkv_branch_prompt(pr, dec, k_mod, a_s, b_s, nb, seq):
    rep = LANES // (nb * H_A)
    ni = HD_A // rep
    lay = lambda x: _scan_prompt_layout(x, nb, seq, rep)
    v = pr[:, 2 * W_A:3 * W_A].reshape(nb, seq, H_A, ni, rep).transpose(1, 3, 0, 2, 4)
    v = v.reshape(1, seq, ni, LANES)
    s0 = jnp.zeros((1, ni, HD_A, LANES), F32)
    y, s = _rwkv_scan(lay(dec), lay(a_s), lay(b_s), lay(k_mod), lay(pr[:, 0:W_A]), v, s0,
                      tc=min(32, seq))
    y = y.reshape(seq, ni, nb, H_A, rep).transpose(2, 0, 3, 1, 4).reshape(nb * seq, W_A)
    s = s.reshape(ni, HD_A, nb, H_A, rep).transpose(2, 3, 0, 4, 1)
    return y, s.reshape(nb, H_A, HD_A, HD_A)


def _rwkv_branch_sample(pr, dec, k_mod, a_s, b_s, state, nb, seq):
    chains = nb * H_A
    ng = chains // LANES

    def lay(x):
        x = x.reshape(nb, seq, H_A, HD_A).transpose(1, 3, 0, 2).reshape(seq, HD_A, ng, LANES)
        return x.transpose(2, 0, 1, 3)

    s0 = state.transpose(2, 3, 0, 1).reshape(HD_A, HD_A, ng, LANES).transpose(2, 0, 1, 3)
    y, s = _rwkv_scan(lay(dec), lay(a_s), lay(b_s), lay(k_mod), lay(pr[:, 0:W_A]),
                      lay(pr[:, 2 * W_A:3 * W_A]), s0, tc=seq)
    y = y.transpose(1, 2, 0, 3).reshape(seq, HD_A, nb, H_A).transpose(2, 0, 3, 1)
    s = s.transpose(1, 2, 0, 3).reshape(HD_A, HD_A, nb, H_A).transpose(2, 3, 0, 1)
    return y.reshape(nb * seq, W_A), s


def kernel(x_prompt, x_sample, cache_k, cache_v, state_rwkv, state_shift, page_table,
           w_in, mu_shift, w0, w2, a0, a2, g2, k_k, k_a, r_k, ln_x_w, ln_x_b,
           lam_q1, lam_k1, lam_q2, lam_k2, subln_g, w_br_a, w_br_b, w_out,
           g_mix, g_ffn, w_up, w_down, g_final):
    nb, seq, _ = x_prompt.shape
    bd, dseq, _ = x_sample.shape
    w_in_b = w_in[0].astype(BF16)
    wa_b, wb_b, wo_b = (w[0].astype(BF16) for w in (w_br_a, w_br_b, w_out))
    wu_b, wd_b = w_up[0].astype(BF16), w_down[0].astype(BF16)
    r_k2 = r_k.reshape(1, W_A)
    lam_rows = jnp.concatenate([lam_q1, lam_k1, lam_q2, lam_k2], axis=0)
    slopes = 2.0 ** (-8.0 * jnp.arange(1, H_B + 1, dtype=F32) / H_B)
    g_fin = g_final.reshape(1, D_MODEL)

    def rwkv_ops(pr):
        return _rwkv_pre(pr, w0, w2[0], a0, a2[0], g2[0], k_k, k_a)

    def finish(x2d, pr, y, k_mod, g, o_b, gates):
        o_a = _rwkv_post(y, pr, k_mod, g, ln_x_w, ln_x_b, r_k2)
        return _merge_ffn(x2d, o_a, o_b, gates, wa_b, wb_b, wo_b, g_ffn, wu_b, wd_b, g_fin)

    xp = x_prompt.reshape(nb * seq, D_MODEL)
    pr, q, k, v, gates, x_last = _inproj_prompt(xp, seq, g_mix, w_in_b, mu_shift)
    dec, k_mod, a_s, b_s, g = rwkv_ops(pr)
    y, s_prompt = _rwkv_branch_prompt(pr, dec, k_mod, a_s, b_s, nb, seq)
    o_b = _attn_prompt(q, k, v, seq, slopes, lam_rows, subln_g)
    y_prompt = finish(xp, pr, y, k_mod, g, o_b, gates).reshape(nb, seq, D_MODEL)
    k_prompt = k.reshape(1, nb, seq, H_B, 2 * HD_B)
    v_prompt = v.reshape(1, nb, seq, H_B, 2 * HD_B)
    shift_prompt = x_last.reshape(1, nb, D_MODEL)

    xs = x_sample.reshape(bd * dseq, D_MODEL)
    shift_pad = jnp.zeros((bd, dseq, D_MODEL), F32).at[:, 0].set(state_shift[0])
    pr, q, k, v, gates, xn = _inproj_sample(xs, dseq, shift_pad.reshape(bd * dseq, D_MODEL),
                                            g_mix, w_in_b, mu_shift)
    dec, k_mod, a_s, b_s, g = rwkv_ops(pr)
    y, s_sample = _rwkv_branch_sample(pr, dec, k_mod, a_s, b_s, state_rwkv[0], bd, dseq)
    tok = lambda t: t.reshape(bd, dseq, W_B)
    pool = cache_k.shape[1]
    o_b = _attn_sample(tok(q), tok(k), tok(v),
                       cache_k[0].reshape(pool, PAGE_SIZE, W_B),
                       cache_v[0].reshape(pool, PAGE_SIZE, W_B),
                       page_table, lam_rows, subln_g)
    y_sample = finish(xs, pr, y, k_mod, g, o_b.reshape(bd * dseq, W_B), gates)
    y_sample = y_sample.reshape(bd, dseq, D_MODEL)
    k_sample = k.reshape(1, bd, dseq, H_B, 2 * HD_B)
    v_sample = v.reshape(1, bd, dseq, H_B, 2 * HD_B)
    shift_sample = xn.reshape(bd, dseq, D_MODEL)[:, -1][None]

    return (y_prompt, y_sample, k_prompt, v_prompt, k_sample, v_sample,
            s_prompt[None], s_sample[None], shift_prompt, shift_sample)
```

```python
import functools

import jax
import jax.numpy as jnp
from jax import lax
from jax.experimental import pallas as pl
from jax.experimental.pallas import tpu as pltpu

F32 = jnp.float32
BF16 = jnp.bfloat16

D_MODEL = 1024
H_A = 8
HD_A = 64
W_A = H_A * HD_A
DECAY_LORA = 64
AAA_LORA = 64
GATE_LORA = 128
RWKV_COLS = 3 * W_A + DECAY_LORA + AAA_LORA + GATE_LORA
H_B = 4
HD_B = 64
W_B = H_B * 2 * HD_B
GATE_COLS = 2 * D_MODEL
N_COLS = RWKV_COLS + 3 * W_B + GATE_COLS
D_FF = 4 * D_MODEL
PAGE_SIZE = 128
NORM_EPS = 1e-6
GN_EPS = 64e-5
SUBLN_EPS = 1e-5
NEG = -1e30
LAM_INIT = 0.8 - 0.6 * 1.0

LANES = 128
SUBLANES = 8
VMEM_LIMIT = 56 * 1024 * 1024

OFF_Q = RWKV_COLS
OFF_K = OFF_Q + W_B
OFF_V = OFF_K + W_B
OFF_GA = OFF_V + W_B
OFF_GB = OFF_GA + D_MODEL


def _params(sem):
    return pltpu.CompilerParams(dimension_semantics=sem, vmem_limit_bytes=VMEM_LIMIT)


def _const_spec(shape):
    nd = len(shape)
    return pl.BlockSpec(shape, lambda *_: (0,) * nd, pipeline_mode=pl.Buffered(1))


def _rms(x, g, eps):
    return x * lax.rsqrt(jnp.mean(x * x, axis=-1, keepdims=True) + eps) * g


def _mm(a, w):
    return jnp.dot(a.astype(BF16), w, preferred_element_type=F32)


def _head_ones(width, head):
    r = lax.broadcasted_iota(jnp.int32, (width, width), 0) // head
    c = lax.broadcasted_iota(jnp.int32, (width, width), 1) // head
    return (r == c).astype(BF16)


def _head_sum(x, ones):
    hi = x.astype(BF16)
    r1 = x - hi.astype(F32)
    mid = r1.astype(BF16)
    lo = (r1 - mid.astype(F32)).astype(BF16)
    d = lambda a: jnp.dot(a, ones, preferred_element_type=F32)
    return d(hi) + d(mid) + d(lo)


def _proj_tail(xn, w_ref, q_ref, k_ref, v_ref, gate_ref):
    xb = xn.astype(BF16)
    q_ref[...] = jnp.dot(xb, w_ref[:, OFF_Q:OFF_K], preferred_element_type=F32)
    k_ref[...] = jnp.dot(xb, w_ref[:, OFF_K:OFF_V], preferred_element_type=F32)
    v_ref[...] = jnp.dot(xb, w_ref[:, OFF_V:OFF_GA], preferred_element_type=F32)
    gate_ref[...] = jax.nn.sigmoid(
        jnp.dot(xb, w_ref[:, OFF_GA:N_COLS], preferred_element_type=F32))
    return xb


def _inproj_prompt_kernel(x_ref, g_ref, w_ref, mu_ref,
                          pr_ref, q_ref, k_ref, v_ref, gate_ref, xl_ref, carry_ref):
    tm = x_ref.shape[0]
    xn = _rms(x_ref[...], g_ref[...], NORM_EPS)
    xl_ref[0] = xn[tm - 1:tm, :]
    xb = _proj_tail(xn, w_ref, q_ref, k_ref, v_ref, gate_ref)
    pr = jnp.dot(xb, w_ref[:, 0:RWKV_COLS], preferred_element_type=F32)

    @pl.when(pl.program_id(1) == 0)
    def _():
        carry_ref[...] = jnp.zeros_like(carry_ref)

    prev = pltpu.roll(pr, 1, 0)
    row = lax.broadcasted_iota(jnp.int32, pr.shape, 0)
    prev = jnp.where(row == 0, carry_ref[SUBLANES - 1:SUBLANES, :], prev)
    carry_ref[...] = pr[tm - SUBLANES:tm, :]
    pr_ref[...] = pr + (prev - pr) * mu_ref[...]


def _inproj_prompt(x2d, seq, g_mix, w_in, mu):
    m = x2d.shape[0]
    nb = m // seq
    tm = min(256, seq)
    nt = seq // tm
    row_spec = lambda w: pl.BlockSpec((tm, w), lambda b, t: (b * nt + t, 0))
    return pl.pallas_call(
        _inproj_prompt_kernel,
        grid=(nb, nt),
        in_specs=[row_spec(D_MODEL), _const_spec((1, D_MODEL)),
                  _const_spec((D_MODEL, N_COLS)), _const_spec((1, RWKV_COLS))],
        out_specs=[row_spec(RWKV_COLS), row_spec(W_B), row_spec(W_B), row_spec(W_B),
                   row_spec(GATE_COLS),
                   pl.BlockSpec((1, 1, D_MODEL), lambda b, t: (b, 0, 0))],
        out_shape=[jax.ShapeDtypeStruct((m, RWKV_COLS), F32),
                   jax.ShapeDtypeStruct((m, W_B), F32),
                   jax.ShapeDtypeStruct((m, W_B), F32),
                   jax.ShapeDtypeStruct((m, W_B), F32),
                   jax.ShapeDtypeStruct((m, GATE_COLS), F32),
                   jax.ShapeDtypeStruct((nb, 1, D_MODEL), F32)],
        scratch_shapes=[pltpu.VMEM((SUBLANES, RWKV_COLS), F32)],
        compiler_params=_params(("parallel", "arbitrary")),
        name="inproj_prompt",
    )(x2d, g_mix, w_in, mu)


def _inproj_sample_kernel(x_ref, sp_ref, g_ref, w_ref, mu_ref,
                          pr_ref, q_ref, k_ref, v_ref, gate_ref, xn_ref, *, seq):
    xn = _rms(x_ref[...], g_ref[...], NORM_EPS)
    xn_ref[...] = xn
    xb = _proj_tail(xn, w_ref, q_ref, k_ref, v_ref, gate_ref)
    row = lax.broadcasted_iota(jnp.int32, xn.shape, 0)
    xprev = jnp.where(row % seq == 0, sp_ref[...], pltpu.roll(xn, 1, 0))
    w_r = w_ref[:, 0:RWKV_COLS]
    pr = jnp.dot(xb, w_r, preferred_element_type=F32)
    prev = jnp.dot(xprev.astype(BF16), w_r, preferred_element_type=F32)
    pr_ref[...] = pr + (prev - pr) * mu_ref[...]


def _inproj_sample(x2d, seq, shift_pad, g_mix, w_in, mu):
    m = x2d.shape[0]
    tm = min(256, m)
    assert tm % seq == 0 and m % tm == 0
    row_spec = lambda w: pl.BlockSpec((tm, w), lambda i: (i, 0))
    return pl.pallas_call(
        functools.partial(_inproj_sample_kernel, seq=seq),
        grid=(m // tm,),
        in_specs=[row_spec(D_MODEL), row_spec(D_MODEL), _const_spec((1, D_MODEL)),
                  _const_spec((D_MODEL, N_COLS)), _const_spec((1, RWKV_COLS))],
        out_specs=[row_spec(RWKV_COLS), row_spec(W_B), row_spec(W_B), row_spec(W_B),
                   row_spec(GATE_COLS), row_spec(D_MODEL)],
        out_shape=[jax.ShapeDtypeStruct((m, RWKV_COLS), F32),
                   jax.ShapeDtypeStruct((m, W_B), F32),
                   jax.ShapeDtypeStruct((m, W_B), F32),
                   jax.ShapeDtypeStruct((m, W_B), F32),
                   jax.ShapeDtypeStruct((m, GATE_COLS), F32),
                   jax.ShapeDtypeStruct((m, D_MODEL), F32)],
        compiler_params=_params(("parallel",)),
        name="inproj_sample",
    )(x2d, shift_pad, g_mix, w_in, mu)


def _dot_f32(a, b):
    return jnp.dot(a, b, precision=lax.Precision.HIGHEST, preferred_element_type=F32)


def _rwkv_pre_kernel(pr_ref, w0_ref, w2_ref, a0_ref, a2_ref, g2_ref, kk_ref, ka_ref,
                     dec_ref, k_ref, a_ref, b_ref, g_ref):
    o_wd = 3 * W_A
    o_ad = o_wd + DECAY_LORA
    o_gd = o_ad + AAA_LORA
    k = pr_ref[:, W_A:2 * W_A]
    wd = pr_ref[:, o_wd:o_ad]
    ad = pr_ref[:, o_ad:o_gd]
    gd = pr_ref[:, o_gd:RWKV_COLS]
    z = -(w0_ref[...] + _dot_f32(jnp.tanh(wd), w2_ref[...]))
    softplus = jnp.maximum(z, 0.0) + jnp.log1p(jnp.exp(-jnp.abs(z)))
    w = -softplus - 0.5
    dec_ref[...] = jnp.exp(-jnp.exp(w))
    a = jax.nn.sigmoid(a0_ref[...] + _dot_f32(ad, a2_ref[...]))
    g_ref[...] = _dot_f32(jax.nn.sigmoid(gd), g2_ref[...])
    kk = k * kk_ref[...]
    ones = _head_ones(W_A, HD_A)
    norm = jnp.sqrt(_head_sum(kk * kk, ones))
    kk = kk / jnp.maximum(norm, 1e-12)
    k_ref[...] = k * (1.0 + (a - 1.0) * ka_ref[...])
    a_ref[...] = -kk
    b_ref[...] = kk * a


def _rwkv_pre(pr, w0, w2, a0, a2, g2, k_k, k_a):
    m = pr.shape[0]
    tm = min(256, m)
    row_spec = lambda w: pl.BlockSpec((tm, w), lambda i: (i, 0))
    full = lambda a: _const_spec(a.shape)
    out = jax.ShapeDtypeStruct((m, W_A), F32)
    return pl.pallas_call(
        _rwkv_pre_kernel,
        grid=(m // tm,),
        in_specs=[row_spec(RWKV_COLS), full(w0), full(w2), full(a0), full(a2), full(g2),
                  full(k_k), full(k_a)],
        out_specs=[row_spec(W_A)] * 5,
        out_shape=[out] * 5,
        compiler_params=_params(("parallel",)),
        name="rwkv_pre",
    )(pr, w0, w2, a0, a2, g2, k_k, k_a)


def _sublane_allsum(x):
    n = x.shape[0] // SUBLANES
    acc = x[0:SUBLANES]
    for g in range(1, n):
        acc = acc + x[g * SUBLANES:(g + 1) * SUBLANES]
    shift = SUBLANES // 2
    while shift >= 1:
        acc = acc + pltpu.roll(acc, shift, 0)
        shift //= 2
    return acc


def _rwkv_scan_kernel(w_ref, a_ref, b_ref, k_ref, r_ref, v_ref, s0_ref, y_ref, s_ref):
    tc = w_ref.shape[0]
    ni, nkey = s0_ref.shape[0], s0_ref.shape[1]
    ng = nkey // SUBLANES
    pad_rows = y_ref.shape[1] - ni

    @pl.when(pl.program_id(1) == 0)
    def _():
        s_ref[...] = s0_ref[...]

    sub = lax.broadcasted_iota(jnp.int32, (SUBLANES, LANES), 0)
    keys = pl.ds(0, nkey)

    def tile8(x):
        return jnp.concatenate([x] * ng, axis=0)

    def step(t, carry):
        def group(ig, carry2):
            y_acc = jnp.zeros((SUBLANES, LANES), F32)
            for ii in range(SUBLANES):
                i = ig * SUBLANES + ii
                s_i = s_ref[i]
                sa = _sublane_allsum(s_i * a_ref[t, keys])
                v_i = jnp.broadcast_to(v_ref[t, pl.ds(i, 1), :], (SUBLANES, LANES))
                s_new = (s_i * w_ref[t, keys] + tile8(sa) * b_ref[t, keys]
                         + tile8(v_i) * k_ref[t, keys])
                s_ref[i] = s_new
                y_i = _sublane_allsum(s_new * r_ref[t, keys])
                y_acc = jnp.where(sub == ii, y_i, y_acc)
            y_ref[t, pl.ds(pl.multiple_of(ig * SUBLANES, SUBLANES), SUBLANES), :] = y_acc
            return carry2

        lax.fori_loop(0, ni // SUBLANES, group, 0)
        if pad_rows:
            y_ref[t, pl.ds(ni, pad_rows), :] = jnp.zeros((pad_rows, LANES), F32)
        return carry

    lax.fori_loop(0, tc, step, 0)


def _rwkv_scan(w, a, b, k, r, v, s0, tc):
    g, t, pk, _ = w.shape
    pv = v.shape[2]
    _, ni, nkey, _ = s0.shape
    x_spec = pl.BlockSpec((None, tc, pk, LANES), lambda gi, ti: (gi, ti, 0, 0))
    v_spec = pl.BlockSpec((None, tc, pv, LANES), lambda gi, ti: (gi, ti, 0, 0))
    s_spec = pl.BlockSpec((None, ni, nkey, LANES), lambda gi, ti: (gi, 0, 0, 0))
    return pl.pallas_call(
        _rwkv_scan_kernel,
        grid=(g, t // tc),
        in_specs=[x_spec] * 5 + [v_spec, s_spec],
        out_specs=[v_spec, s_spec],
        out_shape=[jax.ShapeDtypeStruct((g, t, pv, LANES), F32),
                   jax.ShapeDtypeStruct((g, ni, nkey, LANES), F32)],
        compiler_params=_params(("parallel", "arbitrary")),
        name="rwkv_scan",
    )(w, a, b, k, r, v, s0)


KEY_PITCH = HD_A + SUBLANES
TIME_TILE = LANES


def _val_pitch(n_rows):
    p = n_rows + SUBLANES
    return p if (p // SUBLANES) % 2 else p + SUBLANES


def _to_scan_kernel(x_ref, g_ref, c_scr, *, value_indexed, pitch):
    nb, tt, _ = x_ref.shape
    chains = nb * H_A
    groups = LANES // chains
    heads_per_tile = LANES // HD_A
    for b in range(nb):
        for hp in range(W_A // LANES):
            tile_t = x_ref[b, :, hp * LANES:(hp + 1) * LANES].T
            for h2 in range(heads_per_tile):
                c = b * H_A + hp * heads_per_tile + h2
                c_scr[c * KEY_PITCH:c * KEY_PITCH + HD_A, :] = tile_t[h2 * HD_A:(h2 + 1) * HD_A]
    chain_rows = lambda j: c_scr[pl.ds(j, chains, stride=KEY_PITCH), :]
    n_out = HD_A // groups if value_indexed else HD_A
    for n in range(n_out):
        if value_indexed:
            rows = [chain_rows(n * groups + r) for r in range(groups)]
        else:
            rows = [chain_rows(n)] * groups
        g_ref[pl.ds(n, tt, stride=pitch), :] = jnp.concatenate(rows, axis=0).T
    for n in range(n_out, pitch):
        g_ref[pl.ds(n, tt, stride=pitch), :] = jnp.zeros((tt, LANES), F32)


def _to_scan(x3, col_block, value_indexed):
    nb, t, _ = x3.shape
    groups = LANES // (nb * H_A)
    pitch = _val_pitch(HD_A // groups) if value_indexed else KEY_PITCH
    out = pl.pallas_call(
        functools.partial(_to_scan_kernel, value_indexed=value_indexed, pitch=pitch),
        grid=(t // TIME_TILE,),
        in_specs=[pl.BlockSpec((nb, TIME_TILE, W_A), lambda i: (0, i, col_block))],
        out_specs=pl.BlockSpec((TIME_TILE * pitch, LANES), lambda i: (i, 0)),
        out_shape=jax.ShapeDtypeStruct((t * pitch, LANES), F32),
        scratch_shapes=[pltpu.VMEM((nb * H_A * KEY_PITCH, TIME_TILE), F32)],
        compiler_params=_params(("parallel",)),
        name="to_scan",
    )(x3)
    return out.reshape(t, pitch, LANES)


def _from_scan_kernel(y_ref, o_ref, c_scr, *, pitch):
    nb, tt, _ = o_ref.shape
    chains = nb * H_A
    groups = LANES // chains
    heads_per_tile = LANES // HD_A
    for n in range(HD_A // groups):
        lanes_t = y_ref[pl.ds(n, tt, stride=pitch), :].T
        for r in range(groups):
            c_scr[pl.ds(n * groups + r, chains, stride=KEY_PITCH), :] = (
                lanes_t[r * chains:(r + 1) * chains])
    for b in range(nb):
        for hp in range(W_A // LANES):
            c0 = b * H_A + hp * heads_per_tile
            tile_t = jnp.concatenate(
                [c_scr[(c0 + h2) * KEY_PITCH:(c0 + h2) * KEY_PITCH + HD_A, :]
                 for h2 in range(heads_per_tile)], axis=0)
            o_ref[b, :, hp * LANES:(hp + 1) * LANES] = tile_t.T


def _from_scan(y, nb):
    t, pitch, _ = y.shape
    return pl.pallas_call(
        functools.partial(_from_scan_kernel, pitch=pitch),
        grid=(t // TIME_TILE,),
        in_specs=[pl.BlockSpec((TIME_TILE * pitch, LANES), lambda i: (i, 0))],
        out_specs=pl.BlockSpec((nb, TIME_TILE, W_A), lambda i: (0, i, 0)),
        out_shape=jax.ShapeDtypeStruct((nb, t, W_A), F32),
        scratch_shapes=[pltpu.VMEM((nb * H_A * KEY_PITCH, TIME_TILE), F32)],
        compiler_params=_params(("parallel",)),
        name="from_scan",
    )(y.reshape(t * pitch, LANES))


def _rwkv_post_kernel(y_ref, pr_ref, k_ref, g_ref, lnw_ref, lnb_ref, rk_ref, o_ref):
    ones = _head_ones(W_A, HD_A)
    y = y_ref[...]
    r = pr_ref[:, 0:W_A]
    v = pr_ref[:, 2 * W_A:3 * W_A]
    mean = _head_sum(y, ones) * (1.0 / HD_A)
    d = y - mean
    var = _head_sum(d * d, ones) * (1.0 / HD_A)
    yn = d * lax.rsqrt(var + GN_EPS) * lnw_ref[...] + lnb_ref[...]
    bonus = _head_sum(r * k_ref[...] * rk_ref[...], ones) * v
    o_ref[...] = (yn + bonus) * g_ref[...]


def _rwkv_post(y, pr, k_mod, g, ln_w, ln_b, r_k):
    m = y.shape[0]
    tm = min(256, m)
    row_spec = lambda w: pl.BlockSpec((tm, w), lambda i: (i, 0))
    vec = _const_spec((1, W_A))
    return pl.pallas_call(
        _rwkv_post_kernel,
        grid=(m // tm,),
        in_specs=[row_spec(W_A), row_spec(RWKV_COLS), row_spec(W_A), row_spec(W_A),
                  vec, vec, vec],
        out_specs=row_spec(W_A),
        out_shape=jax.ShapeDtypeStruct((m, W_A), F32),
        compiler_params=_params(("parallel",)),
        name="rwkv_post",
    )(y, pr, k_mod, g, ln_w, ln_b, r_k)


def _lam_value(lam_ref):
    lv = lam_ref[...]
    s1 = jnp.sum(lv[0:1] * lv[1:2], axis=-1, keepdims=True)
    s2 = jnp.sum(lv[2:3] * lv[3:4], axis=-1, keepdims=True)
    return jnp.exp(s1) - jnp.exp(s2) + LAM_INIT


def _attn_prompt_kernel(slope_ref, q_ref, k_ref, v_ref, lam_ref, sg_ref, o_ref,
                        m_ref, l_ref, acc_ref):
    h = pl.program_id(1)
    qi = pl.program_id(2)
    ki = pl.program_id(3)
    tq = q_ref.shape[0]
    tk = k_ref.shape[0]

    @pl.when(ki == 0)
    def _():
        m_ref[...] = jnp.full_like(m_ref, NEG)
        l_ref[...] = jnp.zeros_like(l_ref)
        acc_ref[...] = jnp.zeros_like(acc_ref)

    @pl.when(ki <= qi)
    def _():
        slope = slope_ref[h]
        q = (q_ref[...] * (HD_B ** -0.5)).astype(BF16)
        k = k_ref[...].astype(BF16)
        v = v_ref[...].astype(BF16)
        dist = (qi * tq + lax.broadcasted_iota(jnp.int32, (tq, tk), 0)
                - ki * tk - lax.broadcasted_iota(jnp.int32, (tq, tk), 1))
        bias = -slope * dist.astype(F32)
        for c in range(2):
            s = lax.dot_general(q[:, c * HD_B:(c + 1) * HD_B], k[:, c * HD_B:(c + 1) * HD_B],
                                (((1,), (1,)), ((), ())), preferred_element_type=F32)
            s = jnp.where(dist >= 0, s + bias, NEG)
            m_old = m_ref[c]
            m_new = jnp.maximum(m_old, jnp.max(s, axis=-1, keepdims=True))
            alpha = jnp.exp(m_old - m_new)
            p = jnp.exp(s - m_new)
            l_ref[c] = alpha * l_ref[c] + jnp.sum(p, axis=-1, keepdims=True)
            acc_ref[c] = alpha * acc_ref[c] + jnp.dot(p.astype(BF16), v,
                                                      preferred_element_type=F32)
            m_ref[c] = m_new

    @pl.when(ki == qi)
    def _():
        lam = _lam_value(lam_ref)
        att = acc_ref[0] / l_ref[0] - lam * (acc_ref[1] / l_ref[1])
        o_ref[...] = _rms(att, sg_ref[...], SUBLN_EPS) * (1.0 - LAM_INIT)


def _attn_prompt(q, k, v, seq, slopes, lam_rows, subln_g):
    m = q.shape[0]
    nb = m // seq
    tq = min(512, seq)
    nq = seq // tq
    width = 2 * HD_B
    q_spec = pl.BlockSpec((tq, width), lambda b, h, qi, ki: (b * nq + qi, h))
    kv_spec = pl.BlockSpec((tq, width),
                           lambda b, h, qi, ki: (b * nq + jnp.minimum(ki, qi), h))
    return pl.pallas_call(
        _attn_prompt_kernel,
        grid=(nb, H_B, nq, nq),
        in_specs=[pl.BlockSpec(memory_space=pltpu.SMEM), q_spec, kv_spec, kv_spec,
                  pl.BlockSpec((4, HD_B), lambda *_: (0, 0)),
                  pl.BlockSpec((1, width), lambda *_: (0, 0))],
        out_specs=q_spec,
        out_shape=jax.ShapeDtypeStruct((m, W_B), F32),
        scratch_shapes=[pltpu.VMEM((2, tq, 1), F32), pltpu.VMEM((2, tq, 1), F32),
                        pltpu.VMEM((2, tq, width), F32)],
        compiler_params=_params(("parallel", "parallel", "parallel", "arbitrary")),
        name="attn_prompt",
    )(slopes, q, k, v, lam_rows, subln_g)


def _attn_sample_kernel(pt_ref, q_ref, kn_ref, vn_ref, lam_ref, sg_ref, *rest,
                        n_pages, past):
    del pt_ref
    k_pages = rest[:n_pages]
    v_pages = rest[n_pages:2 * n_pages]
    o_ref, s_ref = rest[2 * n_pages], rest[2 * n_pages + 1]
    tq = q_ref.shape[0]
    rows = 2 * H_B * tq
    nt = (((1,), (1,)), ((), ()))

    q = q_ref[...] * (HD_B ** -0.5)
    q_rep = jnp.concatenate([q] * (2 * H_B), axis=0)
    r_i = lax.broadcasted_iota(jnp.int32, (rows, W_B), 0)
    c_i = lax.broadcasted_iota(jnp.int32, (rows, W_B), 1)
    q_blk = jnp.where(r_i // tq == c_i // HD_B, q_rep, 0.0).astype(BF16)

    row = lax.broadcasted_iota(jnp.int32, (rows, PAGE_SIZE), 0)
    lane = lax.broadcasted_iota(jnp.int32, (rows, PAGE_SIZE), 1)
    head = row // (2 * tq)
    slope = jnp.exp2(-8.0 * (head + 1).astype(F32) / H_B)
    q_pos = past + row % tq

    def page(ref):
        return jnp.concatenate([ref[:, h, :] for h in range(H_B)], axis=-1).astype(BF16)

    for p in range(n_pages):
        s = lax.dot_general(q_blk, page(k_pages[p]), nt, preferred_element_type=F32)
        dist = q_pos - (p * PAGE_SIZE + lane)
        s_ref[:, p * PAGE_SIZE:(p + 1) * PAGE_SIZE] = s - slope * dist.astype(F32)

    kn = jnp.concatenate(
        [kn_ref[...], jnp.zeros((PAGE_SIZE - tq, W_B), F32)], axis=0).astype(BF16)
    s = lax.dot_general(q_blk, kn, nt, preferred_element_type=F32)
    dist = q_pos - (past + lane)
    s = jnp.where((dist >= 0) & (lane < tq), s - slope * dist.astype(F32), NEG)
    s_ref[:, past:past + PAGE_SIZE] = s

    s = s_ref[...]
    mx = jnp.max(s, axis=-1, keepdims=True)
    p_un = jnp.exp(s - mx)
    pn = p_un / jnp.sum(p_un, axis=-1, keepdims=True)
    lam = _lam_value(lam_ref)
    attn = jnp.concatenate(
        [pn[(2 * h) * tq:(2 * h + 1) * tq] - lam * pn[(2 * h + 1) * tq:(2 * h + 2) * tq]
         for h in range(H_B)], axis=0).astype(BF16)

    acc = jnp.zeros((H_B * tq, W_B), F32)
    for p in range(n_pages):
        acc = acc + jnp.dot(attn[:, p * PAGE_SIZE:(p + 1) * PAGE_SIZE],
                            page(v_pages[p]), preferred_element_type=F32)
    vn = jnp.concatenate(
        [vn_ref[...], jnp.zeros((PAGE_SIZE - tq, W_B), F32)], axis=0).astype(BF16)
    acc = acc + jnp.dot(attn[:, past:past + PAGE_SIZE], vn, preferred_element_type=F32)

    width = 2 * HD_B
    outs = []
    for h in range(H_B):
        att = acc[h * tq:(h + 1) * tq, h * width:(h + 1) * width]
        outs.append(_rms(att, sg_ref[...], SUBLN_EPS) * (1.0 - LAM_INIT))
    o_ref[...] = jnp.concatenate(outs, axis=-1)


def _attn_sample(q, k_new, v_new, cache_k, cache_v, page_table, lam_rows, subln_g):
    bd, tq, _ = q.shape
    n_pages = page_table.shape[1]
    past = n_pages * PAGE_SIZE
    tok_spec = pl.BlockSpec((None, tq, W_B), lambda b, pt: (b, 0, 0))
    page_specs = [pl.BlockSpec((None, PAGE_SIZE, H_B, 2 * HD_B),
                               lambda b, pt, p=p: (pt[b * n_pages + p], 0, 0, 0))
                  for p in range(n_pages)]
    grid_spec = pltpu.PrefetchScalarGridSpec(
        num_scalar_prefetch=1,
        grid=(bd,),
        in_specs=[tok_spec, tok_spec, tok_spec,
                  pl.BlockSpec((4, HD_B), lambda b, pt: (0, 0)),
                  pl.BlockSpec((1, 2 * HD_B), lambda b, pt: (0, 0))]
                 + page_specs + page_specs,
        out_specs=tok_spec,
        scratch_shapes=[pltpu.VMEM((2 * H_B * tq, past + PAGE_SIZE), F32)],
    )
    return pl.pallas_call(
        functools.partial(_attn_sample_kernel, n_pages=n_pages, past=past),
        grid_spec=grid_spec,
        out_shape=jax.ShapeDtypeStruct((bd, tq, W_B), F32),
        compiler_params=_params(("parallel",)),
        name="attn_sample",
    )(page_table.reshape(-1), q, k_new, v_new, lam_rows, subln_g,
      *([cache_k] * n_pages), *([cache_v] * n_pages))


def _merge_ffn_kernel(x_ref, oa_ref, ob_ref, gate_ref, wa_ref, wb_ref, wo_ref,
                      gf_ref, wu_ref, wd_ref, gl_ref, y_ref):
    ga = gate_ref[:, 0:D_MODEL]
    gb = gate_ref[:, D_MODEL:GATE_COLS]
    m = ga * _mm(oa_ref[...], wa_ref[...]) + gb * _mm(ob_ref[...], wb_ref[...])
    x1 = x_ref[...] + _mm(m, wo_ref[...])
    hn = _rms(x1, gf_ref[...], NORM_EPS).astype(BF16)
    acc = x1
    chunk = D_MODEL
    for c in range(D_FF // chunk):
        up = jnp.dot(hn, wu_ref[:, c * chunk:(c + 1) * chunk], preferred_element_type=F32)
        act = jnp.square(jnp.maximum(up, 0.0))
        acc = acc + _mm(act, wd_ref[c * chunk:(c + 1) * chunk, :])
    y_ref[...] = _rms(acc, gl_ref[...], NORM_EPS)


def _merge_ffn(x2d, o_a, o_b, gates, w_br_a, w_br_b, w_out, g_ffn, w_up, w_down, g_final):
    m = x2d.shape[0]
    tm = min(256, m)
    row_spec = lambda w: pl.BlockSpec((tm, w), lambda i: (i, 0))
    full = lambda a: _const_spec(a.shape)
    return pl.pallas_call(
        _merge_ffn_kernel,
        grid=(m // tm,),
        in_specs=[row_spec(D_MODEL), row_spec(W_A), row_spec(W_B), row_spec(GATE_COLS),
                  full(w_br_a), full(w_br_b), full(w_out), full(g_ffn), full(w_up),
                  full(w_down), full(g_final)],
        out_specs=row_spec(D_MODEL),
        out_shape=jax.ShapeDtypeStruct((m, D_MODEL), F32),
        compiler_params=_params(("parallel",)),
        name="merge_ffn",
    )(x2d, o_a, o_b, gates, w_br_a, w_br_b, w_out, g_ffn, w_up, w_down, g_final)


def _rwkv_branch_prompt(pr, dec, k_mod, a_s, b_s, nb, seq):
    groups = LANES // (nb * H_A)
    ni = HD_A // groups
    nat = lambda x: x.reshape(nb, seq, x.shape[-1])
    key = lambda x, col=0: _to_scan(nat(x), col, value_indexed=False)[None]
    v = _to_scan(nat(pr), 2, value_indexed=True)[None]
    s0 = jnp.zeros((1, ni, HD_A, LANES), F32)
    y, s = _rwkv_scan(key(dec), key(a_s), key(b_s), key(k_mod), key(pr, 0), v, s0,
                      tc=min(32, seq))
    y = _from_scan(y[0], nb).reshape(nb * seq, W_A)
    s = s.reshape(ni, HD_A, groups, nb, H_A).transpose(3, 4, 0, 2, 1)
    return y, s.reshape(nb, H_A, HD_A, HD_A)


def _rwkv_branch_sample(pr, dec, k_mod, a_s, b_s, state, nb, seq):
    chains = nb * H_A
    ng = chains // LANES

    def lay(x):
        x = x.reshape(nb, seq, H_A, HD_A).transpose(1, 3, 0, 2).reshape(seq, HD_A, ng, LANES)
        return x.transpose(2, 0, 1, 3)

    s0 = state.transpose(2, 3, 0, 1).reshape(HD_A, HD_A, ng, LANES).transpose(2, 0, 1, 3)
    y, s = _rwkv_scan(lay(dec), lay(a_s), lay(b_s), lay(k_mod), lay(pr[:, 0:W_A]),
                      lay(pr[:, 2 * W_A:3 * W_A]), s0, tc=seq)
    y = y.transpose(1, 2, 0, 3).reshape(seq, HD_A, nb, H_A).transpose(2, 0, 3, 1)
    s = s.transpose(1, 2, 0, 3).reshape(HD_A, HD_A, nb, H_A).transpose(2, 3, 0, 1)
    return y.reshape(nb * seq, W_A), s


def kernel(x_prompt, x_sample, cache_k, cache_v, state_rwkv, state_shift, page_table,
           w_in, mu_shift, w0, w2, a0, a2, g2, k_k, k_a, r_k, ln_x_w, ln_x_b,
           lam_q1, lam_k1, lam_q2, lam_k2, subln_g, w_br_a, w_br_b, w_out,
           g_mix, g_ffn, w_up, w_down, g_final):
    nb, seq, _ = x_prompt.shape
    bd, dseq, _ = x_sample.shape
    w_in_b = w_in[0].astype(BF16)
    wa_b, wb_b, wo_b = (w[0].astype(BF16) for w in (w_br_a, w_br_b, w_out))
    wu_b, wd_b = w_up[0].astype(BF16), w_down[0].astype(BF16)
    r_k2 = r_k.reshape(1, W_A)
    lam_rows = jnp.concatenate([lam_q1, lam_k1, lam_q2, lam_k2], axis=0)
    slopes = 2.0 ** (-8.0 * jnp.arange(1, H_B + 1, dtype=F32) / H_B)
    g_fin = g_final.reshape(1, D_MODEL)

    def rwkv_ops(pr):
        return _rwkv_pre(pr, w0, w2[0], a0, a2[0], g2[0], k_k, k_a)

    def finish(x2d, pr, y, k_mod, g, o_b, gates):
        o_a = _rwkv_post(y, pr, k_mod, g, ln_x_w, ln_x_b, r_k2)
        return _merge_ffn(x2d, o_a, o_b, gates, wa_b, wb_b, wo_b, g_ffn, wu_b, wd_b, g_fin)

    xp = x_prompt.reshape(nb * seq, D_MODEL)
    pr, q, k, v, gates, x_last = _inproj_prompt(xp, seq, g_mix, w_in_b, mu_shift)
    dec, k_mod, a_s, b_s, g = rwkv_ops(pr)
    y, s_prompt = _rwkv_branch_prompt(pr, dec, k_mod, a_s, b_s, nb, seq)
    o_b = _attn_prompt(q, k, v, seq, slopes, lam_rows, subln_g)
    y_prompt = finish(xp, pr, y, k_mod, g, o_b, gates).reshape(nb, seq, D_MODEL)
    k_prompt = k.reshape(1, nb, seq, H_B, 2 * HD_B)
    v_prompt = v.reshape(1, nb, seq, H_B, 2 * HD_B)
    shift_prompt = x_last.reshape(1, nb, D_MODEL)

    xs = x_sample.reshape(bd * dseq, D_MODEL)
    shift_pad = jnp.zeros((bd, dseq, D_MODEL), F32).at[:, 0].set(state_shift[0])
    pr, q, k, v, gates, xn = _inproj_sample(xs, dseq, shift_pad.reshape(bd * dseq, D_MODEL),
                                            g_mix, w_in_b, mu_shift)
    dec, k_mod, a_s, b_s, g = rwkv_ops(pr)
    y, s_sample = _rwkv_branch_sample(pr, dec, k_mod, a_s, b_s, state_rwkv[0], bd, dseq)
    tok = lambda t: t.reshape(bd, dseq, W_B)
    o_b = _attn_sample(tok(q), tok(k), tok(v), cache_k[0], cache_v[0],
                       page_table, lam_rows, subln_g)
    y_sample = finish(xs, pr, y, k_mod, g, o_b.reshape(bd * dseq, W_B), gates)
    y_sample = y_sample.reshape(bd, dseq, D_MODEL)
    k_sample = k.reshape(1, bd, dseq, H_B, 2 * HD_B)
    v_sample = v.reshape(1, bd, dseq, H_B, 2 * HD_B)
    shift_sample = xn.reshape(bd, dseq, D_MODEL)[:, -1][None]

    return (y_prompt, y_sample, k_prompt, v_prompt, k_sample, v_sample,
            s_prompt[None], s_sample[None], shift_prompt, shift_sample)
```

```python
import functools

import jax
import jax.numpy as jnp
from jax import lax
from jax.experimental import pallas as pl
from jax.experimental.pallas import tpu as pltpu

F32 = jnp.float32
BF16 = jnp.bfloat16

D_MODEL = 1024
H_A = 8
HD_A = 64
W_A = H_A * HD_A
DECAY_LORA = 64
AAA_LORA = 64
GATE_LORA = 128
RWKV_COLS = 3 * W_A + DECAY_LORA + AAA_LORA + GATE_LORA
H_B = 4
HD_B = 64
W_B = H_B * 2 * HD_B
GATE_COLS = 2 * D_MODEL
N_COLS = RWKV_COLS + 3 * W_B + GATE_COLS
D_FF = 4 * D_MODEL
PAGE_SIZE = 128
NORM_EPS = 1e-6
GN_EPS = 64e-5
SUBLN_EPS = 1e-5
NEG = -1e30
LAM_INIT = 0.8 - 0.6 * 1.0

LANES = 128
SUBLANES = 8
VMEM_LIMIT = 56 * 1024 * 1024

OFF_Q = RWKV_COLS
OFF_K = OFF_Q + W_B
OFF_V = OFF_K + W_B
OFF_GA = OFF_V + W_B
OFF_GB = OFF_GA + D_MODEL


def _params(sem, flags=None):
    return pltpu.CompilerParams(dimension_semantics=sem, vmem_limit_bytes=VMEM_LIMIT,
                                flags=flags)


def _const_spec(shape):
    nd = len(shape)
    return pl.BlockSpec(shape, lambda *_: (0,) * nd, pipeline_mode=pl.Buffered(1))


def _rms(x, g, eps):
    return x * lax.rsqrt(jnp.mean(x * x, axis=-1, keepdims=True) + eps) * g


def _mm(a, w):
    return jnp.dot(a.astype(BF16), w, preferred_element_type=F32)


def _head_ones(width, head):
    r = lax.broadcasted_iota(jnp.int32, (width, width), 0) // head
    c = lax.broadcasted_iota(jnp.int32, (width, width), 1) // head
    return (r == c).astype(BF16)


def _head_sum(x, ones):
    hi = x.astype(BF16)
    r1 = x - hi.astype(F32)
    mid = r1.astype(BF16)
    lo = (r1 - mid.astype(F32)).astype(BF16)
    d = lambda a: jnp.dot(a, ones, preferred_element_type=F32)
    return d(hi) + d(mid) + d(lo)


def _proj_tail(xn, w_ref, q_ref, k_ref, v_ref, gate_ref):
    xb = xn.astype(BF16)
    q_ref[...] = jnp.dot(xb, w_ref[:, OFF_Q:OFF_K], preferred_element_type=F32)
    k_ref[...] = jnp.dot(xb, w_ref[:, OFF_K:OFF_V], preferred_element_type=F32)
    v_ref[...] = jnp.dot(xb, w_ref[:, OFF_V:OFF_GA], preferred_element_type=F32)
    gate_ref[...] = jax.nn.sigmoid(
        jnp.dot(xb, w_ref[:, OFF_GA:N_COLS], preferred_element_type=F32))
    return xb


def _inproj_prompt_kernel(x_ref, g_ref, w_ref, mu_ref,
                          pr_ref, q_ref, k_ref, v_ref, gate_ref, xl_ref, carry_ref):
    tm = x_ref.shape[0]
    xn = _rms(x_ref[...], g_ref[...], NORM_EPS)
    xl_ref[0] = xn[tm - 1:tm, :]
    xb = _proj_tail(xn, w_ref, q_ref, k_ref, v_ref, gate_ref)
    pr = jnp.dot(xb, w_ref[:, 0:RWKV_COLS], preferred_element_type=F32)

    @pl.when(pl.program_id(1) == 0)
    def _():
        carry_ref[...] = jnp.zeros_like(carry_ref)

    prev = pltpu.roll(pr, 1, 0)
    row = lax.broadcasted_iota(jnp.int32, pr.shape, 0)
    prev = jnp.where(row == 0, carry_ref[SUBLANES - 1:SUBLANES, :], prev)
    carry_ref[...] = pr[tm - SUBLANES:tm, :]
    pr_ref[...] = pr + (prev - pr) * mu_ref[...]


def _inproj_prompt(x2d, seq, g_mix, w_in, mu):
    m = x2d.shape[0]
    nb = m // seq
    tm = min(256, seq)
    nt = seq // tm
    row_spec = lambda w: pl.BlockSpec((tm, w), lambda b, t: (b * nt + t, 0))
    return pl.pallas_call(
        _inproj_prompt_kernel,
        grid=(nb, nt),
        in_specs=[row_spec(D_MODEL), _const_spec((1, D_MODEL)),
                  _const_spec((D_MODEL, N_COLS)), _const_spec((1, RWKV_COLS))],
        out_specs=[row_spec(RWKV_COLS), row_spec(W_B), row_spec(W_B), row_spec(W_B),
                   row_spec(GATE_COLS),
                   pl.BlockSpec((1, 1, D_MODEL), lambda b, t: (b, 0, 0))],
        out_shape=[jax.ShapeDtypeStruct((m, RWKV_COLS), F32),
                   jax.ShapeDtypeStruct((m, W_B), F32),
                   jax.ShapeDtypeStruct((m, W_B), F32),
                   jax.ShapeDtypeStruct((m, W_B), F32),
                   jax.ShapeDtypeStruct((m, GATE_COLS), F32),
                   jax.ShapeDtypeStruct((nb, 1, D_MODEL), F32)],
        scratch_shapes=[pltpu.VMEM((SUBLANES, RWKV_COLS), F32)],
        compiler_params=_params(("parallel", "arbitrary")),
        name="inproj_prompt",
    )(x2d, g_mix, w_in, mu)


def _inproj_sample_kernel(x_ref, sp_ref, g_ref, w_ref, mu_ref,
                          pr_ref, q_ref, k_ref, v_ref, gate_ref, xn_ref, *, seq):
    xn = _rms(x_ref[...], g_ref[...], NORM_EPS)
    xn_ref[...] = xn
    xb = _proj_tail(xn, w_ref, q_ref, k_ref, v_ref, gate_ref)
    row = lax.broadcasted_iota(jnp.int32, xn.shape, 0)
    xprev = jnp.where(row % seq == 0, sp_ref[...], pltpu.roll(xn, 1, 0))
    w_r = w_ref[:, 0:RWKV_COLS]
    pr = jnp.dot(xb, w_r, preferred_element_type=F32)
    prev = jnp.dot(xprev.astype(BF16), w_r, preferred_element_type=F32)
    pr_ref[...] = pr + (prev - pr) * mu_ref[...]


def _inproj_sample(x2d, seq, shift_pad, g_mix, w_in, mu):
    m = x2d.shape[0]
    tm = min(256, m)
    assert tm % seq == 0 and m % tm == 0
    row_spec = lambda w: pl.BlockSpec((tm, w), lambda i: (i, 0))
    return pl.pallas_call(
        functools.partial(_inproj_sample_kernel, seq=seq),
        grid=(m // tm,),
        in_specs=[row_spec(D_MODEL), row_spec(D_MODEL), _const_spec((1, D_MODEL)),
                  _const_spec((D_MODEL, N_COLS)), _const_spec((1, RWKV_COLS))],
        out_specs=[row_spec(RWKV_COLS), row_spec(W_B), row_spec(W_B), row_spec(W_B),
                   row_spec(GATE_COLS), row_spec(D_MODEL)],
        out_shape=[jax.ShapeDtypeStruct((m, RWKV_COLS), F32),
                   jax.ShapeDtypeStruct((m, W_B), F32),
                   jax.ShapeDtypeStruct((m, W_B), F32),
                   jax.ShapeDtypeStruct((m, W_B), F32),
                   jax.ShapeDtypeStruct((m, GATE_COLS), F32),
                   jax.ShapeDtypeStruct((m, D_MODEL), F32)],
        compiler_params=_params(("parallel",)),
        name="inproj_sample",
    )(x2d, shift_pad, g_mix, w_in, mu)


def _dot_f32(a, b):
    return jnp.dot(a, b, precision=lax.Precision.HIGHEST, preferred_element_type=F32)


def _rwkv_pre_kernel(pr_ref, w0_ref, w2_ref, a0_ref, a2_ref, g2_ref, kk_ref, ka_ref,
                     dec_ref, k_ref, a_ref, b_ref, g_ref):
    o_wd = 3 * W_A
    o_ad = o_wd + DECAY_LORA
    o_gd = o_ad + AAA_LORA
    k = pr_ref[:, W_A:2 * W_A]
    wd = pr_ref[:, o_wd:o_ad]
    ad = pr_ref[:, o_ad:o_gd]
    gd = pr_ref[:, o_gd:RWKV_COLS]
    z = -(w0_ref[...] + _dot_f32(jnp.tanh(wd), w2_ref[...]))
    softplus = jnp.maximum(z, 0.0) + jnp.log1p(jnp.exp(-jnp.abs(z)))
    w = -softplus - 0.5
    dec_ref[...] = jnp.exp(-jnp.exp(w))
    a = jax.nn.sigmoid(a0_ref[...] + _dot_f32(ad, a2_ref[...]))
    g_ref[...] = _dot_f32(jax.nn.sigmoid(gd), g2_ref[...])
    kk = k * kk_ref[...]
    ones = _head_ones(W_A, HD_A)
    norm = jnp.sqrt(_head_sum(kk * kk, ones))
    kk = kk / jnp.maximum(norm, 1e-12)
    k_ref[...] = k * (1.0 + (a - 1.0) * ka_ref[...])
    a_ref[...] = -kk
    b_ref[...] = kk * a


def _rwkv_pre(pr, w0, w2, a0, a2, g2, k_k, k_a):
    m = pr.shape[0]
    tm = min(256, m)
    row_spec = lambda w: pl.BlockSpec((tm, w), lambda i: (i, 0))
    full = lambda a: _const_spec(a.shape)
    out = jax.ShapeDtypeStruct((m, W_A), F32)
    return pl.pallas_call(
        _rwkv_pre_kernel,
        grid=(m // tm,),
        in_specs=[row_spec(RWKV_COLS), full(w0), full(w2), full(a0), full(a2), full(g2),
                  full(k_k), full(k_a)],
        out_specs=[row_spec(W_A)] * 5,
        out_shape=[out] * 5,
        compiler_params=_params(("parallel",)),
        name="rwkv_pre",
    )(pr, w0, w2, a0, a2, g2, k_k, k_a)


def _sublane_allsum(x):
    n = x.shape[0] // SUBLANES
    acc = x[0:SUBLANES]
    for g in range(1, n):
        acc = acc + x[g * SUBLANES:(g + 1) * SUBLANES]
    shift = SUBLANES // 2
    while shift >= 1:
        acc = acc + pltpu.roll(acc, shift, 0)
        shift //= 2
    return acc


def _rwkv_scan_kernel(w_ref, a_ref, b_ref, k_ref, r_ref, v_ref, s0_ref, y_ref, s_ref):
    tc = w_ref.shape[0]
    ni, nkey = s0_ref.shape[0], s0_ref.shape[1]
    ng = nkey // SUBLANES
    pad_rows = y_ref.shape[1] - ni

    @pl.when(pl.program_id(1) == 0)
    def _():
        s_ref[...] = s0_ref[...]

    sub = lax.broadcasted_iota(jnp.int32, (SUBLANES, LANES), 0)
    keys = pl.ds(0, nkey)

    def tile8(x):
        return jnp.concatenate([x] * ng, axis=0)

    def step(t, carry):
        def group(ig, carry2):
            y_acc = jnp.zeros((SUBLANES, LANES), F32)
            for ii in range(SUBLANES):
                i = ig * SUBLANES + ii
                s_i = s_ref[i]
                sa = _sublane_allsum(s_i * a_ref[t, keys])
                v_i = jnp.broadcast_to(v_ref[t, pl.ds(i, 1), :], (SUBLANES, LANES))
                s_new = (s_i * w_ref[t, keys] + tile8(sa) * b_ref[t, keys]
                         + tile8(v_i) * k_ref[t, keys])
                s_ref[i] = s_new
                y_i = _sublane_allsum(s_new * r_ref[t, keys])
                y_acc = jnp.where(sub == ii, y_i, y_acc)
            y_ref[t, pl.ds(pl.multiple_of(ig * SUBLANES, SUBLANES), SUBLANES), :] = y_acc
            return carry2

        lax.fori_loop(0, ni // SUBLANES, group, 0)
        if pad_rows:
            y_ref[t, pl.ds(ni, pad_rows), :] = jnp.zeros((pad_rows, LANES), F32)
        return carry

    lax.fori_loop(0, tc, step, 0)


def _rwkv_scan(w, a, b, k, r, v, s0, tc):
    g, t, pk, _ = w.shape
    pv = v.shape[2]
    _, ni, nkey, _ = s0.shape
    x_spec = pl.BlockSpec((None, tc, pk, LANES), lambda gi, ti: (gi, ti, 0, 0))
    v_spec = pl.BlockSpec((None, tc, pv, LANES), lambda gi, ti: (gi, ti, 0, 0))
    s_spec = pl.BlockSpec((None, ni, nkey, LANES), lambda gi, ti: (gi, 0, 0, 0))
    return pl.pallas_call(
        _rwkv_scan_kernel,
        grid=(g, t // tc),
        in_specs=[x_spec] * 5 + [v_spec, s_spec],
        out_specs=[v_spec, s_spec],
        out_shape=[jax.ShapeDtypeStruct((g, t, pv, LANES), F32),
                   jax.ShapeDtypeStruct((g, ni, nkey, LANES), F32)],
        compiler_params=_params(("parallel", "arbitrary")),
        name="rwkv_scan",
    )(w, a, b, k, r, v, s0)


KEY_PITCH = HD_A + SUBLANES
TIME_TILE = LANES


def _val_pitch(n_rows):
    p = n_rows + SUBLANES
    return p if (p // SUBLANES) % 2 else p + SUBLANES


def _to_scan_kernel(x_ref, g_ref, c_scr, *, value_indexed, pitch):
    nb, tt, _ = x_ref.shape
    chains = nb * H_A
    groups = LANES // chains
    heads_per_tile = LANES // HD_A
    for b in range(nb):
        for hp in range(W_A // LANES):
            tile_t = x_ref[b, :, hp * LANES:(hp + 1) * LANES].T
            for h2 in range(heads_per_tile):
                c = b * H_A + hp * heads_per_tile + h2
                c_scr[c * KEY_PITCH:c * KEY_PITCH + HD_A, :] = tile_t[h2 * HD_A:(h2 + 1) * HD_A]
    chain_rows = lambda j: c_scr[pl.ds(j, chains, stride=KEY_PITCH), :]
    n_out = HD_A // groups if value_indexed else HD_A
    for n in range(n_out):
        if value_indexed:
            rows = [chain_rows(n * groups + r) for r in range(groups)]
        else:
            rows = [chain_rows(n)] * groups
        g_ref[pl.ds(n, tt, stride=pitch), :] = jnp.concatenate(rows, axis=0).T
    for n in range(n_out, pitch):
        g_ref[pl.ds(n, tt, stride=pitch), :] = jnp.zeros((tt, LANES), F32)


def _to_scan(x3, col_block, value_indexed):
    nb, t, _ = x3.shape
    groups = LANES // (nb * H_A)
    pitch = _val_pitch(HD_A // groups) if value_indexed else KEY_PITCH
    out = pl.pallas_call(
        functools.partial(_to_scan_kernel, value_indexed=value_indexed, pitch=pitch),
        grid=(t // TIME_TILE,),
        in_specs=[pl.BlockSpec((nb, TIME_TILE, W_A), lambda i: (0, i, col_block))],
        out_specs=pl.BlockSpec((TIME_TILE * pitch, LANES), lambda i: (i, 0)),
        out_shape=jax.ShapeDtypeStruct((t * pitch, LANES), F32),
        scratch_shapes=[pltpu.VMEM((nb * H_A * KEY_PITCH, TIME_TILE), F32)],
        compiler_params=_params(("parallel",)),
        name="to_scan",
    )(x3)
    return out.reshape(t, pitch, LANES)


def _from_scan_kernel(y_ref, o_ref, c_scr, *, pitch):
    nb, tt, _ = o_ref.shape
    chains = nb * H_A
    groups = LANES // chains
    heads_per_tile = LANES // HD_A
    for n in range(HD_A // groups):
        lanes_t = y_ref[pl.ds(n, tt, stride=pitch), :].T
        for r in range(groups):
            c_scr[pl.ds(n * groups + r, chains, stride=KEY_PITCH), :] = (
                lanes_t[r * chains:(r + 1) * chains])
    for b in range(nb):
        for hp in range(W_A // LANES):
            c0 = b * H_A + hp * heads_per_tile
            tile_t = jnp.concatenate(
                [c_scr[(c0 + h2) * KEY_PITCH:(c0 + h2) * KEY_PITCH + HD_A, :]
                 for h2 in range(heads_per_tile)], axis=0)
            o_ref[b, :, hp * LANES:(hp + 1) * LANES] = tile_t.T


def _from_scan(y, nb):
    t, pitch, _ = y.shape
    return pl.pallas_call(
        functools.partial(_from_scan_kernel, pitch=pitch),
        grid=(t // TIME_TILE,),
        in_specs=[pl.BlockSpec((TIME_TILE * pitch, LANES), lambda i: (i, 0))],
        out_specs=pl.BlockSpec((nb, TIME_TILE, W_A), lambda i: (0, i, 0)),
        out_shape=jax.ShapeDtypeStruct((nb, t, W_A), F32),
        scratch_shapes=[pltpu.VMEM((nb * H_A * KEY_PITCH, TIME_TILE), F32)],
        compiler_params=_params(("parallel",)),
        name="from_scan",
    )(y.reshape(t * pitch, LANES))


def _rwkv_post_kernel(y_ref, pr_ref, k_ref, g_ref, lnw_ref, lnb_ref, rk_ref, o_ref):
    ones = _head_ones(W_A, HD_A)
    y = y_ref[...]
    r = pr_ref[:, 0:W_A]
    v = pr_ref[:, 2 * W_A:3 * W_A]
    mean = _head_sum(y, ones) * (1.0 / HD_A)
    d = y - mean
    var = _head_sum(d * d, ones) * (1.0 / HD_A)
    yn = d * lax.rsqrt(var + GN_EPS) * lnw_ref[...] + lnb_ref[...]
    bonus = _head_sum(r * k_ref[...] * rk_ref[...], ones) * v
    o_ref[...] = (yn + bonus) * g_ref[...]


def _rwkv_post(y, pr, k_mod, g, ln_w, ln_b, r_k):
    m = y.shape[0]
    tm = min(256, m)
    row_spec = lambda w: pl.BlockSpec((tm, w), lambda i: (i, 0))
    vec = _const_spec((1, W_A))
    return pl.pallas_call(
        _rwkv_post_kernel,
        grid=(m // tm,),
        in_specs=[row_spec(W_A), row_spec(RWKV_COLS), row_spec(W_A), row_spec(W_A),
                  vec, vec, vec],
        out_specs=row_spec(W_A),
        out_shape=jax.ShapeDtypeStruct((m, W_A), F32),
        compiler_params=_params(("parallel",)),
        name="rwkv_post",
    )(y, pr, k_mod, g, ln_w, ln_b, r_k)


def _lam_value(lam_ref):
    lv = lam_ref[...]
    s1 = jnp.sum(lv[0:1] * lv[1:2], axis=-1, keepdims=True)
    s2 = jnp.sum(lv[2:3] * lv[3:4], axis=-1, keepdims=True)
    return jnp.exp(s1) - jnp.exp(s2) + LAM_INIT


POS_SPLIT = 16
ONES_ROWS = 16


def _attn_prompt_kernel(q_ref, k_ref, v_ref, lam_ref, sg_ref, o_ref,
                        kaug_ref, vt_ref, m_ref, acc_ref, sa_ref, sb_ref):
    h = pl.program_id(1)
    qi = pl.program_id(2)
    tq = q_ref.shape[0]
    nblk = k_ref.shape[0] // tq
    width = 2 * HD_B
    nt = (((1,), (1,)), ((), ()))
    slope = jnp.exp2(jnp.full((1, width), -8.0 / H_B, F32) * (h + 1).astype(F32))

    lane = lax.broadcasted_iota(jnp.int32, (tq, width), 1)
    pos = lax.broadcasted_iota(jnp.int32, (tq, width), 0)
    pos_hi = (pos // POS_SPLIT).astype(F32)
    pos_lo = (pos % POS_SPLIT).astype(F32)
    one = jnp.ones((tq, width), F32)

    def features(base, vals):
        out = jnp.zeros((tq, width), F32)
        for i, val in enumerate(vals):
            out = jnp.where(lane == base + i, val, out)
        return out

    own = [(lane >= c * HD_B) & (lane < (c + 1) * HD_B) for c in range(2)]
    other = [(1 - c) * HD_B for c in range(2)]

    @pl.when(qi == 0)
    def _():
        for c in range(2):
            k_feat = features(other[c], [POS_SPLIT * pos_hi, pos_lo, -POS_SPLIT * one, -one])
            k_feat = k_feat * slope
            for j in range(nblk):
                kaug_ref[c, j] = jnp.where(own[c], k_ref[j * tq:(j + 1) * tq, :],
                                           k_feat).astype(BF16)
        for j in range(nblk):
            vt_ref[j] = jnp.concatenate(
                [v_ref[j * tq:(j + 1) * tq, :].T, jnp.ones((ONES_ROWS, tq), F32)],
                axis=0).astype(BF16)

    q = q_ref[...] * (HD_B ** -0.5)
    q_aug = [jnp.where(own[c], q, features(other[c], [one, one, pos_hi, pos_lo])).astype(BF16)
             for c in range(2)]

    m_ref[...] = jnp.full_like(m_ref, NEG)
    acc_ref[...] = jnp.zeros_like(acc_ref)

    def score(kj, s_ref):
        for c in range(2):
            s_ref[c] = lax.dot_general(kaug_ref[c, kj], q_aug[c], nt,
                                       preferred_element_type=F32)

    def col_max(s):
        chunk = s.shape[0] // SUBLANES
        part = s[0:chunk]
        for i in range(1, SUBLANES):
            part = jnp.maximum(part, s[i * chunk:(i + 1) * chunk])
        return jnp.max(part, axis=0, keepdims=True)

    def consume(kj, s_ref, masked):
        block_bias = -slope[:, 0:1] * (tq * (qi - kj)).astype(F32)
        v_t = vt_ref[kj]
        for c in range(2):
            s = s_ref[c]
            if masked:
                key_i = lax.broadcasted_iota(jnp.int32, (tq, tq), 0)
                qry_i = lax.broadcasted_iota(jnp.int32, (tq, tq), 1)
                s = jnp.where(key_i <= qry_i, s, NEG)
            m_old = m_ref[c]
            m_new = jnp.maximum(m_old, col_max(s) + block_bias)
            p = jnp.exp(s - (m_new - block_bias)).astype(BF16)
            acc_ref[c] = (jnp.exp(m_old - m_new) * acc_ref[c]
                          + jnp.dot(v_t, p, preferred_element_type=F32))
            m_ref[c] = m_new

    score(0, sa_ref)

    def pair(kk, carry):
        score(2 * kk + 1, sb_ref)
        consume(2 * kk, sa_ref, masked=False)
        score(2 * kk + 2, sa_ref)
        consume(2 * kk + 1, sb_ref, masked=False)
        return carry

    lax.fori_loop(0, qi // 2, pair, 0)

    @pl.when(qi % 2 == 1)
    def _():
        score(qi, sb_ref)
        consume(qi - 1, sa_ref, masked=False)
        consume(qi, sb_ref, masked=True)

    @pl.when(qi % 2 == 0)
    def _():
        consume(qi, sa_ref, masked=True)

    lam = _lam_value(lam_ref)
    a0 = acc_ref[0]
    a1 = acc_ref[1]
    att_t = a0[0:width] / a0[width:width + 1] - lam * (a1[0:width] / a1[width:width + 1])
    o_ref[...] = _rms(att_t.T, sg_ref[...], SUBLN_EPS) * (1.0 - LAM_INIT)


def _attn_prompt(q, k, v, seq, lam_rows, subln_g):
    m = q.shape[0]
    nb = m // seq
    tq = min(512, seq)
    assert tq % POS_SPLIT == 0 and tq // POS_SPLIT <= 256 and seq % tq == 0
    nq = seq // tq
    width = 2 * HD_B
    q_spec = pl.BlockSpec((tq, width), lambda b, h, qi: (b * nq + qi, h))
    kv_spec = pl.BlockSpec((seq, width), lambda b, h, qi: (b, h))
    return pl.pallas_call(
        _attn_prompt_kernel,
        grid=(nb, H_B, nq),
        in_specs=[q_spec, kv_spec, kv_spec,
                  pl.BlockSpec((4, HD_B), lambda *_: (0, 0)),
                  pl.BlockSpec((1, width), lambda *_: (0, 0))],
        out_specs=q_spec,
        out_shape=jax.ShapeDtypeStruct((m, W_B), F32),
        scratch_shapes=[pltpu.VMEM((2, nq, tq, width), BF16),
                        pltpu.VMEM((nq, width + ONES_ROWS, tq), BF16),
                        pltpu.VMEM((2, 1, tq), F32),
                        pltpu.VMEM((2, width + ONES_ROWS, tq), F32),
                        pltpu.VMEM((2, tq, tq), F32),
                        pltpu.VMEM((2, tq, tq), F32)],
        compiler_params=_params(("parallel", "parallel", "arbitrary")),
        name="attn_prompt",
    )(q, k, v, lam_rows, subln_g)


def _attn_sample_kernel(pt_ref, q_ref, kn_ref, vn_ref, lam_ref, sg_ref, *rest,
                        n_pages, past):
    del pt_ref
    k_pages = rest[:n_pages]
    v_pages = rest[n_pages:2 * n_pages]
    o_ref, s_ref = rest[2 * n_pages], rest[2 * n_pages + 1]
    tq = q_ref.shape[0]
    rows = 2 * H_B * tq
    nt = (((1,), (1,)), ((), ()))

    q = q_ref[...] * (HD_B ** -0.5)
    q_rep = jnp.concatenate([q] * (2 * H_B), axis=0)
    r_i = lax.broadcasted_iota(jnp.int32, (rows, W_B), 0)
    c_i = lax.broadcasted_iota(jnp.int32, (rows, W_B), 1)
    q_blk = jnp.where(r_i // tq == c_i // HD_B, q_rep, 0.0).astype(BF16)

    row = lax.broadcasted_iota(jnp.int32, (rows, PAGE_SIZE), 0)
    lane = lax.broadcasted_iota(jnp.int32, (rows, PAGE_SIZE), 1)
    head = row // (2 * tq)
    slope = jnp.exp2(-8.0 * (head + 1).astype(F32) / H_B)
    q_pos = past + row % tq

    def page(ref):
        return jnp.concatenate([ref[pl.ds(h, PAGE_SIZE, stride=H_B), :] for h in range(H_B)],
                               axis=-1).astype(BF16)

    for p in range(n_pages):
        s = lax.dot_general(q_blk, page(k_pages[p]), nt, preferred_element_type=F32)
        dist = q_pos - (p * PAGE_SIZE + lane)
        s_ref[:, p * PAGE_SIZE:(p + 1) * PAGE_SIZE] = s - slope * dist.astype(F32)

    kn = jnp.concatenate(
        [kn_ref[...], jnp.zeros((PAGE_SIZE - tq, W_B), F32)], axis=0).astype(BF16)
    s = lax.dot_general(q_blk, kn, nt, preferred_element_type=F32)
    dist = q_pos - (past + lane)
    s = jnp.where((dist >= 0) & (lane < tq), s - slope * dist.astype(F32), NEG)
    s_ref[:, past:past + PAGE_SIZE] = s

    s = s_ref[...]
    mx = jnp.max(s, axis=-1, keepdims=True)
    p_un = jnp.exp(s - mx)
    pn = p_un / jnp.sum(p_un, axis=-1, keepdims=True)
    lam = _lam_value(lam_ref)
    attn = jnp.concatenate(
        [pn[(2 * h) * tq:(2 * h + 1) * tq] - lam * pn[(2 * h + 1) * tq:(2 * h + 2) * tq]
         for h in range(H_B)], axis=0).astype(BF16)

    acc = jnp.zeros((H_B * tq, W_B), F32)
    for p in range(n_pages):
        acc = acc + jnp.dot(attn[:, p * PAGE_SIZE:(p + 1) * PAGE_SIZE],
                            page(v_pages[p]), preferred_element_type=F32)
    vn = jnp.concatenate(
        [vn_ref[...], jnp.zeros((PAGE_SIZE - tq, W_B), F32)], axis=0).astype(BF16)
    acc = acc + jnp.dot(attn[:, past:past + PAGE_SIZE], vn, preferred_element_type=F32)

    width = 2 * HD_B
    outs = []
    for h in range(H_B):
        att = acc[h * tq:(h + 1) * tq, h * width:(h + 1) * width]
        outs.append(_rms(att, sg_ref[...], SUBLN_EPS) * (1.0 - LAM_INIT))
    o_ref[...] = jnp.concatenate(outs, axis=-1)


def _attn_sample(q, k_new, v_new, cache_k, cache_v, page_table, lam_rows, subln_g):
    bd, tq, _ = q.shape
    n_pages = page_table.shape[1]
    past = n_pages * PAGE_SIZE
    tok_spec = pl.BlockSpec((None, tq, W_B), lambda b, pt: (b, 0, 0))
    page_specs = [pl.BlockSpec((None, PAGE_SIZE * H_B, 2 * HD_B),
                               lambda b, pt, p=p: (pt[b * n_pages + p], 0, 0))
                  for p in range(n_pages)]
    grid_spec = pltpu.PrefetchScalarGridSpec(
        num_scalar_prefetch=1,
        grid=(bd,),
        in_specs=[tok_spec, tok_spec, tok_spec,
                  pl.BlockSpec((4, HD_B), lambda b, pt: (0, 0)),
                  pl.BlockSpec((1, 2 * HD_B), lambda b, pt: (0, 0))]
                 + page_specs + page_specs,
        out_specs=tok_spec,
        scratch_shapes=[pltpu.VMEM((2 * H_B * tq, past + PAGE_SIZE), F32)],
    )
    return pl.pallas_call(
        functools.partial(_attn_sample_kernel, n_pages=n_pages, past=past),
        grid_spec=grid_spec,
        out_shape=jax.ShapeDtypeStruct((bd, tq, W_B), F32),
        compiler_params=_params(("parallel",)),
        name="attn_sample",
    )(page_table.reshape(-1), q, k_new, v_new, lam_rows, subln_g,
      *([cache_k] * n_pages), *([cache_v] * n_pages))


def _merge_ffn_kernel(x_ref, oa_ref, ob_ref, gate_ref, wa_ref, wb_ref, wo_ref,
                      gf_ref, wu_ref, wd_ref, gl_ref, y_ref):
    ga = gate_ref[:, 0:D_MODEL]
    gb = gate_ref[:, D_MODEL:GATE_COLS]
    m = ga * _mm(oa_ref[...], wa_ref[...]) + gb * _mm(ob_ref[...], wb_ref[...])
    x1 = x_ref[...] + _mm(m, wo_ref[...])
    hn = _rms(x1, gf_ref[...], NORM_EPS).astype(BF16)
    acc = x1
    chunk = D_MODEL
    for c in range(D_FF // chunk):
        up = jnp.dot(hn, wu_ref[:, c * chunk:(c + 1) * chunk], preferred_element_type=F32)
        act = jnp.square(jnp.maximum(up, 0.0))
        acc = acc + _mm(act, wd_ref[c * chunk:(c + 1) * chunk, :])
    y_ref[...] = _rms(acc, gl_ref[...], NORM_EPS)


def _merge_ffn(x2d, o_a, o_b, gates, w_br_a, w_br_b, w_out, g_ffn, w_up, w_down, g_final):
    m = x2d.shape[0]
    tm = min(256, m)
    row_spec = lambda w: pl.BlockSpec((tm, w), lambda i: (i, 0))
    full = lambda a: _const_spec(a.shape)
    return pl.pallas_call(
        _merge_ffn_kernel,
        grid=(m // tm,),
        in_specs=[row_spec(D_MODEL), row_spec(W_A), row_spec(W_B), row_spec(GATE_COLS),
                  full(w_br_a), full(w_br_b), full(w_out), full(g_ffn), full(w_up),
                  full(w_down), full(g_final)],
        out_specs=row_spec(D_MODEL),
        out_shape=jax.ShapeDtypeStruct((m, D_MODEL), F32),
        compiler_params=_params(("parallel",)),
        name="merge_ffn",
    )(x2d, o_a, o_b, gates, w_br_a, w_br_b, w_out, g_ffn, w_up, w_down, g_final)


def _rwkv_branch_prompt(pr, dec, k_mod, a_s, b_s, nb, seq):
    groups = LANES // (nb * H_A)
    ni = HD_A // groups
    nat = lambda x: x.reshape(nb, seq, x.shape[-1])
    key = lambda x, col=0: _to_scan(nat(x), col, value_indexed=False)[None]
    v = _to_scan(nat(pr), 2, value_indexed=True)[None]
    s0 = jnp.zeros((1, ni, HD_A, LANES), F32)
    y, s = _rwkv_scan(key(dec), key(a_s), key(b_s), key(k_mod), key(pr, 0), v, s0,
                      tc=min(32, seq))
    y = _from_scan(y[0], nb).reshape(nb * seq, W_A)
    s = s.reshape(ni, HD_A, groups, nb, H_A).transpose(3, 4, 0, 2, 1)
    return y, s.reshape(nb, H_A, HD_A, HD_A)


def _rwkv_branch_sample(pr, dec, k_mod, a_s, b_s, state, nb, seq):
    chains = nb * H_A
    ng = chains // LANES

    def lay(x):
        x = x.reshape(nb, seq, H_A, HD_A).transpose(1, 3, 0, 2).reshape(seq, HD_A, ng, LANES)
        return x.transpose(2, 0, 1, 3)

    s0 = state.transpose(2, 3, 0, 1).reshape(HD_A, HD_A, ng, LANES).transpose(2, 0, 1, 3)
    y, s = _rwkv_scan(lay(dec), lay(a_s), lay(b_s), lay(k_mod), lay(pr[:, 0:W_A]),
                      lay(pr[:, 2 * W_A:3 * W_A]), s0, tc=seq)
    y = y.transpose(1, 2, 0, 3).reshape(seq, HD_A, nb, H_A).transpose(2, 0, 3, 1)
    s = s.transpose(1, 2, 0, 3).reshape(HD_A, HD_A, nb, H_A).transpose(2, 3, 0, 1)
    return y.reshape(nb * seq, W_A), s


def kernel(x_prompt, x_sample, cache_k, cache_v, state_rwkv, state_shift, page_table,
           w_in, mu_shift, w0, w2, a0, a2, g2, k_k, k_a, r_k, ln_x_w, ln_x_b,
           lam_q1, lam_k1, lam_q2, lam_k2, subln_g, w_br_a, w_br_b, w_out,
           g_mix, g_ffn, w_up, w_down, g_final):
    nb, seq, _ = x_prompt.shape
    bd, dseq, _ = x_sample.shape
    w_in_b = w_in[0].astype(BF16)
    wa_b, wb_b, wo_b = (w[0].astype(BF16) for w in (w_br_a, w_br_b, w_out))
    wu_b, wd_b = w_up[0].astype(BF16), w_down[0].astype(BF16)
    r_k2 = r_k.reshape(1, W_A)
    lam_rows = jnp.concatenate([lam_q1, lam_k1, lam_q2, lam_k2], axis=0)
    g_fin = g_final.reshape(1, D_MODEL)

    def rwkv_ops(pr):
        return _rwkv_pre(pr, w0, w2[0], a0, a2[0], g2[0], k_k, k_a)

    def finish(x2d, pr, y, k_mod, g, o_b, gates):
        o_a = _rwkv_post(y, pr, k_mod, g, ln_x_w, ln_x_b, r_k2)
        return _merge_ffn(x2d, o_a, o_b, gates, wa_b, wb_b, wo_b, g_ffn, wu_b, wd_b, g_fin)

    xp = x_prompt.reshape(nb * seq, D_MODEL)
    pr, q, k, v, gates, x_last = _inproj_prompt(xp, seq, g_mix, w_in_b, mu_shift)
    dec, k_mod, a_s, b_s, g = rwkv_ops(pr)
    y, s_prompt = _rwkv_branch_prompt(pr, dec, k_mod, a_s, b_s, nb, seq)
    o_b = _attn_prompt(q, k, v, seq, lam_rows, subln_g)
    y_prompt = finish(xp, pr, y, k_mod, g, o_b, gates).reshape(nb, seq, D_MODEL)
    k_prompt = k.reshape(1, nb, seq, H_B, 2 * HD_B)
    v_prompt = v.reshape(1, nb, seq, H_B, 2 * HD_B)
    shift_prompt = x_last.reshape(1, nb, D_MODEL)

    xs = x_sample.reshape(bd * dseq, D_MODEL)
    shift_pad = jnp.zeros((bd, dseq, D_MODEL), F32).at[:, 0].set(state_shift[0])
    pr, q, k, v, gates, xn = _inproj_sample(xs, dseq, shift_pad.reshape(bd * dseq, D_MODEL),
                                            g_mix, w_in_b, mu_shift)
    dec, k_mod, a_s, b_s, g = rwkv_ops(pr)
    y, s_sample = _rwkv_branch_sample(pr, dec, k_mod, a_s, b_s, state_rwkv[0], bd, dseq)
    tok = lambda t: t.reshape(bd, dseq, W_B)
    pages = lambda c: c[0].reshape(c.shape[1], PAGE_SIZE * H_B, 2 * HD_B)
    o_b = _attn_sample(tok(q), tok(k), tok(v), pages(cache_k), pages(cache_v),
                       page_table, lam_rows, subln_g)
    y_sample = finish(xs, pr, y, k_mod, g, o_b.reshape(bd * dseq, W_B), gates)
    y_sample = y_sample.reshape(bd, dseq, D_MODEL)
    k_sample = k.reshape(1, bd, dseq, H_B, 2 * HD_B)
    v_sample = v.reshape(1, bd, dseq, H_B, 2 * HD_B)
    shift_sample = xn.reshape(bd, dseq, D_MODEL)[:, -1][None]

    return (y_prompt, y_sample, k_prompt, v_prompt, k_sample, v_sample,
            s_prompt[None], s_sample[None], shift_prompt, shift_sample)
```

```python
import functools

import jax
import jax.numpy as jnp
from jax import lax
from jax.experimental import pallas as pl
from jax.experimental.pallas import tpu as pltpu

F32 = jnp.float32
BF16 = jnp.bfloat16

D_MODEL = 1024
H_A = 8
HD_A = 64
W_A = H_A * HD_A
DECAY_LORA = 64
AAA_LORA = 64
GATE_LORA = 128
RWKV_COLS = 3 * W_A + DECAY_LORA + AAA_LORA + GATE_LORA
H_B = 4
HD_B = 64
W_B = H_B * 2 * HD_B
GATE_COLS = 2 * D_MODEL
N_COLS = RWKV_COLS + 3 * W_B + GATE_COLS
D_FF = 4 * D_MODEL
PAGE_SIZE = 128
NORM_EPS = 1e-6
GN_EPS = 64e-5
SUBLN_EPS = 1e-5
NEG = -1e30
LAM_INIT = 0.8 - 0.6 * 1.0

LANES = 128
SUBLANES = 8
VMEM_LIMIT = 56 * 1024 * 1024

OFF_Q = RWKV_COLS
OFF_K = OFF_Q + W_B
OFF_V = OFF_K + W_B
OFF_GA = OFF_V + W_B
OFF_GB = OFF_GA + D_MODEL


def _params(sem, flags=None):
    return pltpu.CompilerParams(dimension_semantics=sem, vmem_limit_bytes=VMEM_LIMIT,
                                flags=flags)


def _const_spec(shape):
    nd = len(shape)
    return pl.BlockSpec(shape, lambda *_: (0,) * nd, pipeline_mode=pl.Buffered(1))


def _rms(x, g, eps):
    return x * lax.rsqrt(jnp.mean(x * x, axis=-1, keepdims=True) + eps) * g


def _mm(a, w):
    return jnp.dot(a.astype(BF16), w, preferred_element_type=F32)


def _head_ones(width, head):
    r = lax.broadcasted_iota(jnp.int32, (width, width), 0) // head
    c = lax.broadcasted_iota(jnp.int32, (width, width), 1) // head
    return (r == c).astype(BF16)


def _head_sum(x, ones):
    hi = x.astype(BF16)
    r1 = x - hi.astype(F32)
    mid = r1.astype(BF16)
    lo = (r1 - mid.astype(F32)).astype(BF16)
    d = lambda a: jnp.dot(a, ones, preferred_element_type=F32)
    return d(hi) + d(mid) + d(lo)


def _proj_tail(xn, w_ref, q_ref, k_ref, v_ref, gate_ref):
    xb = xn.astype(BF16)
    q_ref[...] = jnp.dot(xb, w_ref[:, OFF_Q:OFF_K], preferred_element_type=F32)
    k_ref[...] = jnp.dot(xb, w_ref[:, OFF_K:OFF_V], preferred_element_type=F32)
    v_ref[...] = jnp.dot(xb, w_ref[:, OFF_V:OFF_GA], preferred_element_type=F32)
    gate_ref[...] = jax.nn.sigmoid(
        jnp.dot(xb, w_ref[:, OFF_GA:N_COLS], preferred_element_type=F32))
    return xb


def _inproj_prompt_kernel(x_ref, g_ref, w_ref, mu_ref,
                          pr_ref, q_ref, k_ref, v_ref, gate_ref, xl_ref, carry_ref):
    tm = x_ref.shape[0]
    xn = _rms(x_ref[...], g_ref[...], NORM_EPS)
    xl_ref[0] = xn[tm - 1:tm, :]
    xb = _proj_tail(xn, w_ref, q_ref, k_ref, v_ref, gate_ref)
    pr = jnp.dot(xb, w_ref[:, 0:RWKV_COLS], preferred_element_type=F32)

    @pl.when(pl.program_id(1) == 0)
    def _():
        carry_ref[...] = jnp.zeros_like(carry_ref)

    prev = pltpu.roll(pr, 1, 0)
    row = lax.broadcasted_iota(jnp.int32, pr.shape, 0)
    prev = jnp.where(row == 0, carry_ref[SUBLANES - 1:SUBLANES, :], prev)
    carry_ref[...] = pr[tm - SUBLANES:tm, :]
    pr_ref[...] = pr + (prev - pr) * mu_ref[...]


def _inproj_prompt(x2d, seq, g_mix, w_in, mu):
    m = x2d.shape[0]
    nb = m // seq
    tm = min(256, seq)
    nt = seq // tm
    row_spec = lambda w: pl.BlockSpec((tm, w), lambda b, t: (b * nt + t, 0))
    return pl.pallas_call(
        _inproj_prompt_kernel,
        grid=(nb, nt),
        in_specs=[row_spec(D_MODEL), _const_spec((1, D_MODEL)),
                  _const_spec((D_MODEL, N_COLS)), _const_spec((1, RWKV_COLS))],
        out_specs=[row_spec(RWKV_COLS), row_spec(W_B), row_spec(W_B), row_spec(W_B),
                   row_spec(GATE_COLS),
                   pl.BlockSpec((1, 1, D_MODEL), lambda b, t: (b, 0, 0))],
        out_shape=[jax.ShapeDtypeStruct((m, RWKV_COLS), F32),
                   jax.ShapeDtypeStruct((m, W_B), F32),
                   jax.ShapeDtypeStruct((m, W_B), F32),
                   jax.ShapeDtypeStruct((m, W_B), F32),
                   jax.ShapeDtypeStruct((m, GATE_COLS), F32),
                   jax.ShapeDtypeStruct((nb, 1, D_MODEL), F32)],
        scratch_shapes=[pltpu.VMEM((SUBLANES, RWKV_COLS), F32)],
        compiler_params=_params(("parallel", "arbitrary")),
        name="inproj_prompt",
    )(x2d, g_mix, w_in, mu)


def _inproj_sample_kernel(x_ref, sp_ref, g_ref, w_ref, mu_ref,
                          pr_ref, q_ref, k_ref, v_ref, gate_ref, xn_ref, *, seq):
    xn = _rms(x_ref[...], g_ref[...], NORM_EPS)
    xn_ref[...] = xn
    xb = _proj_tail(xn, w_ref, q_ref, k_ref, v_ref, gate_ref)
    row = lax.broadcasted_iota(jnp.int32, xn.shape, 0)
    xprev = jnp.where(row % seq == 0, sp_ref[...], pltpu.roll(xn, 1, 0))
    w_r = w_ref[:, 0:RWKV_COLS]
    pr = jnp.dot(xb, w_r, preferred_element_type=F32)
    prev = jnp.dot(xprev.astype(BF16), w_r, preferred_element_type=F32)
    pr_ref[...] = pr + (prev - pr) * mu_ref[...]


def _inproj_sample(x2d, seq, shift_pad, g_mix, w_in, mu):
    m = x2d.shape[0]
    tm = min(256, m)
    assert tm % seq == 0 and m % tm == 0
    row_spec = lambda w: pl.BlockSpec((tm, w), lambda i: (i, 0))
    return pl.pallas_call(
        functools.partial(_inproj_sample_kernel, seq=seq),
        grid=(m // tm,),
        in_specs=[row_spec(D_MODEL), row_spec(D_MODEL), _const_spec((1, D_MODEL)),
                  _const_spec((D_MODEL, N_COLS)), _const_spec((1, RWKV_COLS))],
        out_specs=[row_spec(RWKV_COLS), row_spec(W_B), row_spec(W_B), row_spec(W_B),
                   row_spec(GATE_COLS), row_spec(D_MODEL)],
        out_shape=[jax.ShapeDtypeStruct((m, RWKV_COLS), F32),
                   jax.ShapeDtypeStruct((m, W_B), F32),
                   jax.ShapeDtypeStruct((m, W_B), F32),
                   jax.ShapeDtypeStruct((m, W_B), F32),
                   jax.ShapeDtypeStruct((m, GATE_COLS), F32),
                   jax.ShapeDtypeStruct((m, D_MODEL), F32)],
        compiler_params=_params(("parallel",)),
        name="inproj_sample",
    )(x2d, shift_pad, g_mix, w_in, mu)


def _dot_f32(a, b):
    return jnp.dot(a, b, precision=lax.Precision.HIGHEST, preferred_element_type=F32)


def _rwkv_pre_kernel(pr_ref, w0_ref, w2_ref, a0_ref, a2_ref, g2_ref, kk_ref, ka_ref, rk_ref,
                     dec_ref, k_ref, a_ref, b_ref, g_ref, bv_ref):
    o_wd = 3 * W_A
    o_ad = o_wd + DECAY_LORA
    o_gd = o_ad + AAA_LORA
    k = pr_ref[:, W_A:2 * W_A]
    wd = pr_ref[:, o_wd:o_ad]
    ad = pr_ref[:, o_ad:o_gd]
    gd = pr_ref[:, o_gd:RWKV_COLS]
    z = -(w0_ref[...] + _dot_f32(jnp.tanh(wd), w2_ref[...]))
    softplus = jnp.maximum(z, 0.0) + jnp.log1p(jnp.exp(-jnp.abs(z)))
    w = -softplus - 0.5
    dec_ref[...] = jnp.exp(-jnp.exp(w))
    a = jax.nn.sigmoid(a0_ref[...] + _dot_f32(ad, a2_ref[...]))
    g_ref[...] = _dot_f32(jax.nn.sigmoid(gd), g2_ref[...])
    kk = k * kk_ref[...]
    ones = _head_ones(W_A, HD_A)
    norm = jnp.sqrt(_head_sum(kk * kk, ones))
    kk = kk / jnp.maximum(norm, 1e-12)
    k_mod = k * (1.0 + (a - 1.0) * ka_ref[...])
    k_ref[...] = k_mod
    a_ref[...] = -kk
    b_ref[...] = kk * a
    r = pr_ref[:, 0:W_A]
    v = pr_ref[:, 2 * W_A:3 * W_A]
    bv_ref[...] = _head_sum(r * k_mod * rk_ref[...], ones) * v


def _rwkv_pre(pr, w0, w2, a0, a2, g2, k_k, k_a, r_k):
    m = pr.shape[0]
    tm = min(256, m)
    row_spec = lambda w: pl.BlockSpec((tm, w), lambda i: (i, 0))
    full = lambda a: _const_spec(a.shape)
    out = jax.ShapeDtypeStruct((m, W_A), F32)
    return pl.pallas_call(
        _rwkv_pre_kernel,
        grid=(m // tm,),
        in_specs=[row_spec(RWKV_COLS), full(w0), full(w2), full(a0), full(a2), full(g2),
                  full(k_k), full(k_a), full(r_k)],
        out_specs=[row_spec(W_A)] * 6,
        out_shape=[out] * 6,
        compiler_params=_params(("parallel",)),
        name="rwkv_pre",
    )(pr, w0, w2, a0, a2, g2, k_k, k_a, r_k)


def _sublane_allsum(x):
    n = x.shape[0] // SUBLANES
    acc = x[0:SUBLANES]
    for g in range(1, n):
        acc = acc + x[g * SUBLANES:(g + 1) * SUBLANES]
    shift = SUBLANES // 2
    while shift >= 1:
        acc = acc + pltpu.roll(acc, shift, 0)
        shift //= 2
    return acc


def _rwkv_scan_kernel(w_ref, a_ref, b_ref, k_ref, r_ref, v_ref, s0_ref, y_ref, s_ref):
    tc = w_ref.shape[0]
    ni, nkey = s0_ref.shape[0], s0_ref.shape[1]
    ng = nkey // SUBLANES
    pad_rows = y_ref.shape[1] - ni

    @pl.when(pl.program_id(1) == 0)
    def _():
        s_ref[...] = s0_ref[...]

    sub = lax.broadcasted_iota(jnp.int32, (SUBLANES, LANES), 0)
    keys = pl.ds(0, nkey)

    def tile8(x):
        return jnp.concatenate([x] * ng, axis=0)

    def step(t, carry):
        def group(ig, carry2):
            y_acc = jnp.zeros((SUBLANES, LANES), F32)
            for ii in range(SUBLANES):
                i = ig * SUBLANES + ii
                s_i = s_ref[i]
                sa = _sublane_allsum(s_i * a_ref[t, keys])
                v_i = jnp.broadcast_to(v_ref[t, pl.ds(i, 1), :], (SUBLANES, LANES))
                s_new = (s_i * w_ref[t, keys] + tile8(sa) * b_ref[t, keys]
                         + tile8(v_i) * k_ref[t, keys])
                s_ref[i] = s_new
                y_i = _sublane_allsum(s_new * r_ref[t, keys])
                y_acc = jnp.where(sub == ii, y_i, y_acc)
            y_ref[t, pl.ds(pl.multiple_of(ig * SUBLANES, SUBLANES), SUBLANES), :] = y_acc
            return carry2

        n_groups = ni // SUBLANES
        if n_groups <= 2:
            for ig in range(n_groups):
                group(ig, 0)
        else:
            lax.fori_loop(0, n_groups, group, 0)
        if pad_rows:
            y_ref[t, pl.ds(ni, pad_rows), :] = jnp.zeros((pad_rows, LANES), F32)
        return carry

    lax.fori_loop(0, tc, step, 0, unroll=2)


def _rwkv_scan(w, a, b, k, r, v, s0, tc):
    g, t, pk, _ = w.shape
    pv = v.shape[2]
    _, ni, nkey, _ = s0.shape
    x_spec = pl.BlockSpec((None, tc, pk, LANES), lambda gi, ti: (gi, ti, 0, 0))
    v_spec = pl.BlockSpec((None, tc, pv, LANES), lambda gi, ti: (gi, ti, 0, 0))
    s_spec = pl.BlockSpec((None, ni, nkey, LANES), lambda gi, ti: (gi, 0, 0, 0))
    return pl.pallas_call(
        _rwkv_scan_kernel,
        grid=(g, t // tc),
        in_specs=[x_spec] * 5 + [v_spec, s_spec],
        out_specs=[v_spec, s_spec],
        out_shape=[jax.ShapeDtypeStruct((g, t, pv, LANES), F32),
                   jax.ShapeDtypeStruct((g, ni, nkey, LANES), F32)],
        compiler_params=_params(("parallel", "arbitrary")),
        name="rwkv_scan",
    )(w, a, b, k, r, v, s0)


KEY_PITCH = HD_A + SUBLANES
TIME_TILE = LANES


def _val_pitch(n_rows):
    p = n_rows + SUBLANES
    return p if (p // SUBLANES) % 2 else p + SUBLANES


def _to_scan_kernel(x_ref, g_ref, c_scr, *, value_indexed, pitch):
    nb, tt, _ = x_ref.shape
    chains = nb * H_A
    groups = LANES // chains
    heads_per_tile = LANES // HD_A
    for b in range(nb):
        for hp in range(W_A // LANES):
            tile_t = x_ref[b, :, hp * LANES:(hp + 1) * LANES].T
            for h2 in range(heads_per_tile):
                c = b * H_A + hp * heads_per_tile + h2
                c_scr[c * KEY_PITCH:c * KEY_PITCH + HD_A, :] = tile_t[h2 * HD_A:(h2 + 1) * HD_A]
    chain_rows = lambda j: c_scr[pl.ds(j, chains, stride=KEY_PITCH), :]
    n_out = HD_A // groups if value_indexed else HD_A
    for n in range(n_out):
        if value_indexed:
            rows = [chain_rows(n * groups + r) for r in range(groups)]
        else:
            rows = [chain_rows(n)] * groups
        g_ref[pl.ds(n, tt, stride=pitch), :] = jnp.concatenate(rows, axis=0).T
    for n in range(n_out, pitch):
        g_ref[pl.ds(n, tt, stride=pitch), :] = jnp.zeros((tt, LANES), F32)


def _to_scan(x3, col_block, value_indexed):
    nb, t, _ = x3.shape
    groups = LANES // (nb * H_A)
    pitch = _val_pitch(HD_A // groups) if value_indexed else KEY_PITCH
    out = pl.pallas_call(
        functools.partial(_to_scan_kernel, value_indexed=value_indexed, pitch=pitch),
        grid=(t // TIME_TILE,),
        in_specs=[pl.BlockSpec((nb, TIME_TILE, W_A), lambda i: (0, i, col_block))],
        out_specs=pl.BlockSpec((TIME_TILE * pitch, LANES), lambda i: (i, 0)),
        out_shape=jax.ShapeDtypeStruct((t * pitch, LANES), F32),
        scratch_shapes=[pltpu.VMEM((nb * H_A * KEY_PITCH, TIME_TILE), F32)],
        compiler_params=_params(("parallel",)),
        name="to_scan",
    )(x3)
    return out.reshape(t, pitch, LANES)


def _from_scan_kernel(y_ref, lnw_ref, lnb_ref, o_ref, c_scr, *, pitch):
    nb, tt, _ = o_ref.shape
    chains = nb * H_A
    groups = LANES // chains
    heads_per_tile = LANES // HD_A
    for n in range(HD_A // groups):
        lanes_t = y_ref[pl.ds(n, tt, stride=pitch), :].T
        for r in range(groups):
            c_scr[pl.ds(n * groups + r, chains, stride=KEY_PITCH), :] = (
                lanes_t[r * chains:(r + 1) * chains])
    for c in range(chains):
        rows = pl.ds(c * KEY_PITCH, HD_A)
        head = c % H_A
        y = c_scr[rows, :]
        mean = _sublane_allsum(y) * (1.0 / HD_A)
        d = y - jnp.concatenate([mean] * (HD_A // SUBLANES), axis=0)
        var = _sublane_allsum(d * d) * (1.0 / HD_A)
        inv = jnp.concatenate([lax.rsqrt(var + GN_EPS)] * (HD_A // SUBLANES), axis=0)
        c_scr[rows, :] = (d * inv * lnw_ref[head * HD_A:(head + 1) * HD_A, :]
                          + lnb_ref[head * HD_A:(head + 1) * HD_A, :])
    for b in range(nb):
        for hp in range(W_A // LANES):
            c0 = b * H_A + hp * heads_per_tile
            tile_t = jnp.concatenate(
                [c_scr[(c0 + h2) * KEY_PITCH:(c0 + h2) * KEY_PITCH + HD_A, :]
                 for h2 in range(heads_per_tile)], axis=0)
            o_ref[b, :, hp * LANES:(hp + 1) * LANES] = tile_t.T


def _from_scan(y, nb, ln_w, ln_b):
    t, pitch, _ = y.shape
    cols = lambda p: jnp.broadcast_to(p.reshape(W_A, 1), (W_A, TIME_TILE))
    return pl.pallas_call(
        functools.partial(_from_scan_kernel, pitch=pitch),
        grid=(t // TIME_TILE,),
        in_specs=[pl.BlockSpec((TIME_TILE * pitch, LANES), lambda i: (i, 0)),
                  _const_spec((W_A, TIME_TILE)), _const_spec((W_A, TIME_TILE))],
        out_specs=pl.BlockSpec((nb, TIME_TILE, W_A), lambda i: (0, i, 0)),
        out_shape=jax.ShapeDtypeStruct((nb, t, W_A), F32),
        scratch_shapes=[pltpu.VMEM((nb * H_A * KEY_PITCH, TIME_TILE), F32)],
        compiler_params=_params(("parallel",)),
        name="from_scan",
    )(y.reshape(t * pitch, LANES), cols(ln_w), cols(ln_b))


def _group_norm_kernel(y_ref, lnw_ref, lnb_ref, o_ref):
    ones = _head_ones(W_A, HD_A)
    y = y_ref[...]
    mean = _head_sum(y, ones) * (1.0 / HD_A)
    d = y - mean
    var = _head_sum(d * d, ones) * (1.0 / HD_A)
    o_ref[...] = d * lax.rsqrt(var + GN_EPS) * lnw_ref[...] + lnb_ref[...]


def _group_norm(y, ln_w, ln_b):
    m = y.shape[0]
    tm = min(256, m)
    row_spec = pl.BlockSpec((tm, W_A), lambda i: (i, 0))
    vec = _const_spec((1, W_A))
    return pl.pallas_call(
        _group_norm_kernel,
        grid=(m // tm,),
        in_specs=[row_spec, vec, vec],
        out_specs=row_spec,
        out_shape=jax.ShapeDtypeStruct((m, W_A), F32),
        compiler_params=_params(("parallel",)),
        name="group_norm",
    )(y, ln_w, ln_b)


def _lam_value(lam_ref):
    lv = lam_ref[...]
    s1 = jnp.sum(lv[0:1] * lv[1:2], axis=-1, keepdims=True)
    s2 = jnp.sum(lv[2:3] * lv[3:4], axis=-1, keepdims=True)
    return jnp.exp(s1) - jnp.exp(s2) + LAM_INIT


POS_SPLIT = 16
ONES_ROWS = 16


def _attn_prompt_kernel(q_ref, k_ref, v_ref, lam_ref, sg_ref, o_ref,
                        kaug_ref, vt_ref, m_ref, acc_ref, sa_ref, sb_ref):
    h = pl.program_id(1)
    qi = pl.program_id(2)
    tq = q_ref.shape[0]
    nblk = k_ref.shape[0] // tq
    width = 2 * HD_B
    nt = (((1,), (1,)), ((), ()))
    slope = jnp.exp2(jnp.full((1, width), -8.0 / H_B, F32) * (h + 1).astype(F32))

    lane = lax.broadcasted_iota(jnp.int32, (tq, width), 1)
    pos = lax.broadcasted_iota(jnp.int32, (tq, width), 0)
    pos_hi = (pos // POS_SPLIT).astype(F32)
    pos_lo = (pos % POS_SPLIT).astype(F32)
    one = jnp.ones((tq, width), F32)

    def features(base, vals):
        out = jnp.zeros((tq, width), F32)
        for i, val in enumerate(vals):
            out = jnp.where(lane == base + i, val, out)
        return out

    own = [(lane >= c * HD_B) & (lane < (c + 1) * HD_B) for c in range(2)]
    other = [(1 - c) * HD_B for c in range(2)]

    @pl.when(qi == 0)
    def _():
        for c in range(2):
            k_feat = features(other[c], [POS_SPLIT * pos_hi, pos_lo, -POS_SPLIT * one, -one])
            k_feat = k_feat * slope
            for j in range(nblk):
                kaug_ref[c, j] = jnp.where(own[c], k_ref[j * tq:(j + 1) * tq, :],
                                           k_feat).astype(BF16)
        for j in range(nblk):
            vt_ref[j] = jnp.concatenate(
                [v_ref[j * tq:(j + 1) * tq, :].T, jnp.ones((ONES_ROWS, tq), F32)],
                axis=0).astype(BF16)

    q = q_ref[...] * (HD_B ** -0.5)
    q_aug = [jnp.where(own[c], q, features(other[c], [one, one, pos_hi, pos_lo])).astype(BF16)
             for c in range(2)]

    m_ref[...] = jnp.full_like(m_ref, NEG)
    acc_ref[...] = jnp.zeros_like(acc_ref)

    def score(kj, s_ref):
        for c in range(2):
            s_ref[c] = lax.dot_general(kaug_ref[c, kj], q_aug[c], nt,
                                       preferred_element_type=F32)

    def col_max(s):
        chunk = s.shape[0] // SUBLANES
        part = s[0:chunk]
        for i in range(1, SUBLANES):
            part = jnp.maximum(part, s[i * chunk:(i + 1) * chunk])
        return jnp.max(part, axis=0, keepdims=True)

    def consume(kj, s_ref, masked):
        block_bias = -slope[:, 0:1] * (tq * (qi - kj)).astype(F32)
        v_t = vt_ref[kj]
        for c in range(2):
            s = s_ref[c]
            if masked:
                key_i = lax.broadcasted_iota(jnp.int32, (tq, tq), 0)
                qry_i = lax.broadcasted_iota(jnp.int32, (tq, tq), 1)
                s = jnp.where(key_i <= qry_i, s, NEG)
            m_old = m_ref[c]
            m_new = jnp.maximum(m_old, col_max(s) + block_bias)
            p = jnp.exp(s - (m_new - block_bias)).astype(BF16)
            acc_ref[c] = (jnp.exp(m_old - m_new) * acc_ref[c]
                          + jnp.dot(v_t, p, preferred_element_type=F32))
            m_ref[c] = m_new

    score(0, sa_ref)

    def pair(kk, carry):
        score(2 * kk + 1, sb_ref)
        consume(2 * kk, sa_ref, masked=False)
        score(2 * kk + 2, sa_ref)
        consume(2 * kk + 1, sb_ref, masked=False)
        return carry

    lax.fori_loop(0, qi // 2, pair, 0)

    @pl.when(qi % 2 == 1)
    def _():
        score(qi, sb_ref)
        consume(qi - 1, sa_ref, masked=False)
        consume(qi, sb_ref, masked=True)

    @pl.when(qi % 2 == 0)
    def _():
        consume(qi, sa_ref, masked=True)

    lam = _lam_value(lam_ref)
    a0 = acc_ref[0]
    a1 = acc_ref[1]
    att_t = a0[0:width] / a0[width:width + 1] - lam * (a1[0:width] / a1[width:width + 1])
    o_ref[...] = _rms(att_t.T, sg_ref[...], SUBLN_EPS) * (1.0 - LAM_INIT)


def _attn_prompt(q, k, v, seq, lam_rows, subln_g):
    m = q.shape[0]
    nb = m // seq
    tq = min(512, seq)
    assert tq % POS_SPLIT == 0 and tq // POS_SPLIT <= 256 and seq % tq == 0
    nq = seq // tq
    width = 2 * HD_B
    q_spec = pl.BlockSpec((tq, width), lambda b, h, qi: (b * nq + qi, h))
    kv_spec = pl.BlockSpec((seq, width), lambda b, h, qi: (b, h))
    return pl.pallas_call(
        _attn_prompt_kernel,
        grid=(nb, H_B, nq),
        in_specs=[q_spec, kv_spec, kv_spec,
                  pl.BlockSpec((4, HD_B), lambda *_: (0, 0)),
                  pl.BlockSpec((1, width), lambda *_: (0, 0))],
        out_specs=q_spec,
        out_shape=jax.ShapeDtypeStruct((m, W_B), F32),
        scratch_shapes=[pltpu.VMEM((2, nq, tq, width), BF16),
                        pltpu.VMEM((nq, width + ONES_ROWS, tq), BF16),
                        pltpu.VMEM((2, 1, tq), F32),
                        pltpu.VMEM((2, width + ONES_ROWS, tq), F32),
                        pltpu.VMEM((2, tq, tq), F32),
                        pltpu.VMEM((2, tq, tq), F32)],
        compiler_params=_params(("parallel", "parallel", "arbitrary")),
        name="attn_prompt",
    )(q, k, v, lam_rows, subln_g)


def _attn_sample_kernel(pt_ref, q_ref, kn_ref, vn_ref, lam_ref, sg_ref, *rest,
                        n_pages, past):
    del pt_ref
    k_pages = rest[:n_pages]
    v_pages = rest[n_pages:2 * n_pages]
    o_ref, s_ref = rest[2 * n_pages], rest[2 * n_pages + 1]
    tq = q_ref.shape[0]
    rows = 2 * H_B * tq
    nt = (((1,), (1,)), ((), ()))

    q = q_ref[...] * (HD_B ** -0.5)
    q_rep = jnp.concatenate([q] * (2 * H_B), axis=0)
    r_i = lax.broadcasted_iota(jnp.int32, (rows, W_B), 0)
    c_i = lax.broadcasted_iota(jnp.int32, (rows, W_B), 1)
    q_blk = jnp.where(r_i // tq == c_i // HD_B, q_rep, 0.0).astype(BF16)

    row = lax.broadcasted_iota(jnp.int32, (rows, PAGE_SIZE), 0)
    lane = lax.broadcasted_iota(jnp.int32, (rows, PAGE_SIZE), 1)
    head = row // (2 * tq)
    slope = jnp.exp2(-8.0 * (head + 1).astype(F32) / H_B)
    q_pos = past + row % tq

    def page(ref):
        return jnp.concatenate([ref[pl.ds(h, PAGE_SIZE, stride=H_B), :] for h in range(H_B)],
                               axis=-1).astype(BF16)

    for p in range(n_pages):
        s = lax.dot_general(q_blk, page(k_pages[p]), nt, preferred_element_type=F32)
        dist = q_pos - (p * PAGE_SIZE + lane)
        s_ref[:, p * PAGE_SIZE:(p + 1) * PAGE_SIZE] = s - slope * dist.astype(F32)

    kn = jnp.concatenate(
        [kn_ref[...], jnp.zeros((PAGE_SIZE - tq, W_B), F32)], axis=0).astype(BF16)
    s = lax.dot_general(q_blk, kn, nt, preferred_element_type=F32)
    dist = q_pos - (past + lane)
    s = jnp.where((dist >= 0) & (lane < tq), s - slope * dist.astype(F32), NEG)
    s_ref[:, past:past + PAGE_SIZE] = s

    s = s_ref[...]
    mx = jnp.max(s, axis=-1, keepdims=True)
    p_un = jnp.exp(s - mx)
    pn = p_un / jnp.sum(p_un, axis=-1, keepdims=True)
    lam = _lam_value(lam_ref)
    attn = jnp.concatenate(
        [pn[(2 * h) * tq:(2 * h + 1) * tq] - lam * pn[(2 * h + 1) * tq:(2 * h + 2) * tq]
         for h in range(H_B)], axis=0).astype(BF16)

    acc = jnp.zeros((H_B * tq, W_B), F32)
    for p in range(n_pages):
        acc = acc + jnp.dot(attn[:, p * PAGE_SIZE:(p + 1) * PAGE_SIZE],
                            page(v_pages[p]), preferred_element_type=F32)
    vn = jnp.concatenate(
        [vn_ref[...], jnp.zeros((PAGE_SIZE - tq, W_B), F32)], axis=0).astype(BF16)
    acc = acc + jnp.dot(attn[:, past:past + PAGE_SIZE], vn, preferred_element_type=F32)

    width = 2 * HD_B
    outs = []
    for h in range(H_B):
        att = acc[h * tq:(h + 1) * tq, h * width:(h + 1) * width]
        outs.append(_rms(att, sg_ref[...], SUBLN_EPS) * (1.0 - LAM_INIT))
    o_ref[...] = jnp.concatenate(outs, axis=-1)


def _attn_sample(q, k_new, v_new, cache_k, cache_v, page_table, lam_rows, subln_g):
    bd, tq, _ = q.shape
    n_pages = page_table.shape[1]
    past = n_pages * PAGE_SIZE
    tok_spec = pl.BlockSpec((None, tq, W_B), lambda b, pt: (b, 0, 0))
    page_specs = [pl.BlockSpec((None, PAGE_SIZE * H_B, 2 * HD_B),
                               lambda b, pt, p=p: (pt[b * n_pages + p], 0, 0))
                  for p in range(n_pages)]
    grid_spec = pltpu.PrefetchScalarGridSpec(
        num_scalar_prefetch=1,
        grid=(bd,),
        in_specs=[tok_spec, tok_spec, tok_spec,
                  pl.BlockSpec((4, HD_B), lambda b, pt: (0, 0)),
                  pl.BlockSpec((1, 2 * HD_B), lambda b, pt: (0, 0))]
                 + page_specs + page_specs,
        out_specs=tok_spec,
        scratch_shapes=[pltpu.VMEM((2 * H_B * tq, past + PAGE_SIZE), F32)],
    )
    return pl.pallas_call(
        functools.partial(_attn_sample_kernel, n_pages=n_pages, past=past),
        grid_spec=grid_spec,
        out_shape=jax.ShapeDtypeStruct((bd, tq, W_B), F32),
        compiler_params=_params(("parallel",)),
        name="attn_sample",
    )(page_table.reshape(-1), q, k_new, v_new, lam_rows, subln_g,
      *([cache_k] * n_pages), *([cache_v] * n_pages))


def _merge_ffn_kernel(x_ref, yn_ref, bv_ref, g_ref, ob_ref, gate_ref, wa_ref, wb_ref, wo_ref,
                      gf_ref, wu_ref, wd_ref, gl_ref, y_ref):
    ga = gate_ref[:, 0:D_MODEL]
    gb = gate_ref[:, D_MODEL:GATE_COLS]
    o_a = (yn_ref[...] + bv_ref[...]) * g_ref[...]
    m = ga * _mm(o_a, wa_ref[...]) + gb * _mm(ob_ref[...], wb_ref[...])
    x1 = x_ref[...] + _mm(m, wo_ref[...])
    hn = _rms(x1, gf_ref[...], NORM_EPS).astype(BF16)
    acc = x1
    chunk = D_MODEL
    for c in range(D_FF // chunk):
        up = jnp.dot(hn, wu_ref[:, c * chunk:(c + 1) * chunk], preferred_element_type=F32)
        act = jnp.square(jnp.maximum(up, 0.0))
        acc = acc + _mm(act, wd_ref[c * chunk:(c + 1) * chunk, :])
    y_ref[...] = _rms(acc, gl_ref[...], NORM_EPS)


def _merge_ffn(x2d, yn, bv, g, o_b, gates, w_br_a, w_br_b, w_out, g_ffn, w_up, w_down,
               g_final):
    m = x2d.shape[0]
    tm = min(256, m)
    row_spec = lambda w: pl.BlockSpec((tm, w), lambda i: (i, 0))
    full = lambda a: _const_spec(a.shape)
    return pl.pallas_call(
        _merge_ffn_kernel,
        grid=(m // tm,),
        in_specs=[row_spec(D_MODEL), row_spec(W_A), row_spec(W_A), row_spec(W_A),
                  row_spec(W_B), row_spec(GATE_COLS),
                  full(w_br_a), full(w_br_b), full(w_out), full(g_ffn), full(w_up),
                  full(w_down), full(g_final)],
        out_specs=row_spec(D_MODEL),
        out_shape=jax.ShapeDtypeStruct((m, D_MODEL), F32),
        compiler_params=_params(("parallel",)),
        name="merge_ffn",
    )(x2d, yn, bv, g, o_b, gates, w_br_a, w_br_b, w_out, g_ffn, w_up, w_down, g_final)


def _rwkv_branch_prompt(pr, dec, k_mod, a_s, b_s, nb, seq, ln_w, ln_b):
    groups = LANES // (nb * H_A)
    ni = HD_A // groups
    nat = lambda x: x.reshape(nb, seq, x.shape[-1])
    key = lambda x, col=0: _to_scan(nat(x), col, value_indexed=False)[None]
    v = _to_scan(nat(pr), 2, value_indexed=True)[None]
    s0 = jnp.zeros((1, ni, HD_A, LANES), F32)
    y, s = _rwkv_scan(key(dec), key(a_s), key(b_s), key(k_mod), key(pr, 0), v, s0,
                      tc=min(32, seq))
    y = _from_scan(y[0], nb, ln_w, ln_b).reshape(nb * seq, W_A)
    s = s.reshape(ni, HD_A, groups, nb, H_A).transpose(3, 4, 0, 2, 1)
    return y, s.reshape(nb, H_A, HD_A, HD_A)


def _rwkv_branch_sample(pr, dec, k_mod, a_s, b_s, state, nb, seq, ln_w, ln_b):
    chains = nb * H_A
    ng = chains // LANES

    def lay(x):
        x = x.reshape(nb, seq, H_A, HD_A).transpose(1, 3, 0, 2).reshape(seq, HD_A, ng, LANES)
        return x.transpose(2, 0, 1, 3)

    s0 = state.transpose(2, 3, 0, 1).reshape(HD_A, HD_A, ng, LANES).transpose(2, 0, 1, 3)
    y, s = _rwkv_scan(lay(dec), lay(a_s), lay(b_s), lay(k_mod), lay(pr[:, 0:W_A]),
                      lay(pr[:, 2 * W_A:3 * W_A]), s0, tc=seq)
    y = y.transpose(1, 2, 0, 3).reshape(seq, HD_A, nb, H_A).transpose(2, 0, 3, 1)
    s = s.transpose(1, 2, 0, 3).reshape(HD_A, HD_A, nb, H_A).transpose(2, 3, 0, 1)
    return _group_norm(y.reshape(nb * seq, W_A), ln_w, ln_b), s


def kernel(x_prompt, x_sample, cache_k, cache_v, state_rwkv, state_shift, page_table,
           w_in, mu_shift, w0, w2, a0, a2, g2, k_k, k_a, r_k, ln_x_w, ln_x_b,
           lam_q1, lam_k1, lam_q2, lam_k2, subln_g, w_br_a, w_br_b, w_out,
           g_mix, g_ffn, w_up, w_down, g_final):
    nb, seq, _ = x_prompt.shape
    bd, dseq, _ = x_sample.shape
    w_in_b = w_in[0].astype(BF16)
    wa_b, wb_b, wo_b = (w[0].astype(BF16) for w in (w_br_a, w_br_b, w_out))
    wu_b, wd_b = w_up[0].astype(BF16), w_down[0].astype(BF16)
    r_k2 = r_k.reshape(1, W_A)
    lam_rows = jnp.concatenate([lam_q1, lam_k1, lam_q2, lam_k2], axis=0)
    g_fin = g_final.reshape(1, D_MODEL)

    def rwkv_ops(pr):
        return _rwkv_pre(pr, w0, w2[0], a0, a2[0], g2[0], k_k, k_a, r_k2)

    def finish(x2d, yn, bv, g, o_b, gates):
        return _merge_ffn(x2d, yn, bv, g, o_b, gates, wa_b, wb_b, wo_b, g_ffn, wu_b, wd_b,
                          g_fin)

    xp = x_prompt.reshape(nb * seq, D_MODEL)
    pr, q, k, v, gates, x_last = _inproj_prompt(xp, seq, g_mix, w_in_b, mu_shift)
    dec, k_mod, a_s, b_s, g, bv = rwkv_ops(pr)
    yn, s_prompt = _rwkv_branch_prompt(pr, dec, k_mod, a_s, b_s, nb, seq, ln_x_w, ln_x_b)
    o_b = _attn_prompt(q, k, v, seq, lam_rows, subln_g)
    y_prompt = finish(xp, yn, bv, g, o_b, gates).reshape(nb, seq, D_MODEL)
    k_prompt = k.reshape(1, nb, seq, H_B, 2 * HD_B)
    v_prompt = v.reshape(1, nb, seq, H_B, 2 * HD_B)
    shift_prompt = x_last.reshape(1, nb, D_MODEL)

    xs = x_sample.reshape(bd * dseq, D_MODEL)
    shift_pad = jnp.zeros((bd, dseq, D_MODEL), F32).at[:, 0].set(state_shift[0])
    pr, q, k, v, gates, xn = _inproj_sample(xs, dseq, shift_pad.reshape(bd * dseq, D_MODEL),
                                            g_mix, w_in_b, mu_shift)
    dec, k_mod, a_s, b_s, g, bv = rwkv_ops(pr)
    yn, s_sample = _rwkv_branch_sample(pr, dec, k_mod, a_s, b_s, state_rwkv[0], bd, dseq,
                                       ln_x_w, ln_x_b)
    tok = lambda t: t.reshape(bd, dseq, W_B)
    pages = lambda c: c[0].reshape(c.shape[1], PAGE_SIZE * H_B, 2 * HD_B)
    o_b = _attn_sample(tok(q), tok(k), tok(v), pages(cache_k), pages(cache_v),
                       page_table, lam_rows, subln_g)
    y_sample = finish(xs, yn, bv, g, o_b.reshape(bd * dseq, W_B), gates)
    y_sample = y_sample.reshape(bd, dseq, D_MODEL)
    k_sample = k.reshape(1, bd, dseq, H_B, 2 * HD_B)
    v_sample = v.reshape(1, bd, dseq, H_B, 2 * HD_B)
    shift_sample = xn.reshape(bd, dseq, D_MODEL)[:, -1][None]

    return (y_prompt, y_sample, k_prompt, v_prompt, k_sample, v_sample,
            s_prompt[None], s_sample[None], shift_prompt, shift_sample)
```

```python
import functools

import jax
import jax.numpy as jnp
from jax import lax
from jax.experimental import pallas as pl
from jax.experimental.pallas import tpu as pltpu

F32 = jnp.float32
BF16 = jnp.bfloat16

D_MODEL = 1024
H_A = 8
HD_A = 64
W_A = H_A * HD_A
DECAY_LORA = 64
AAA_LORA = 64
GATE_LORA = 128
RWKV_COLS = 3 * W_A + DECAY_LORA + AAA_LORA + GATE_LORA
H_B = 4
HD_B = 64
W_B = H_B * 2 * HD_B
GATE_COLS = 2 * D_MODEL
N_COLS = RWKV_COLS + 3 * W_B + GATE_COLS
D_FF = 4 * D_MODEL
PAGE_SIZE = 128
NORM_EPS = 1e-6
GN_EPS = 64e-5
SUBLN_EPS = 1e-5
NEG = -1e30
LAM_INIT = 0.8 - 0.6 * 1.0

LANES = 128
SUBLANES = 8
VMEM_LIMIT = 56 * 1024 * 1024
SCAN_CHUNK = 32

OFF_Q = RWKV_COLS
OFF_K = OFF_Q + W_B
OFF_V = OFF_K + W_B
OFF_GA = OFF_V + W_B


def _params(sem):
    return pltpu.CompilerParams(dimension_semantics=sem, vmem_limit_bytes=VMEM_LIMIT)


def _const_spec(shape):
    nd = len(shape)
    return pl.BlockSpec(shape, lambda *_: (0,) * nd, pipeline_mode=pl.Buffered(1))


def _rms(x, g, eps):
    return x * lax.rsqrt(jnp.mean(x * x, axis=-1, keepdims=True) + eps) * g


def _mm(a, w):
    return jnp.dot(a.astype(BF16), w, preferred_element_type=F32)


def _head_ones(width, head):
    r = lax.broadcasted_iota(jnp.int32, (width, width), 0) // head
    c = lax.broadcasted_iota(jnp.int32, (width, width), 1) // head
    return (r == c).astype(BF16)


def _head_sum(x, ones):
    hi = x.astype(BF16)
    r1 = x - hi.astype(F32)
    mid = r1.astype(BF16)
    lo = (r1 - mid.astype(F32)).astype(BF16)
    d = lambda a: jnp.dot(a, ones, preferred_element_type=F32)
    return d(hi) + d(mid) + d(lo)


def _proj_tail(xn, w_ref, q_ref, k_ref, v_ref, kt_ref, vt_ref, gate_ref):
    xb = xn.astype(BF16)
    tm = xn.shape[0]
    q_ref[...] = jnp.dot(xb, w_ref[:, OFF_Q:OFF_K], preferred_element_type=F32)
    k = jnp.dot(xb, w_ref[:, OFF_K:OFF_V], preferred_element_type=F32)
    v = jnp.dot(xb, w_ref[:, OFF_V:OFF_GA], preferred_element_type=F32)
    k_ref[...] = k
    v_ref[...] = v
    for h in range(H_B):
        cols = slice(h * 2 * HD_B, (h + 1) * 2 * HD_B)
        kt_ref[pl.ds(h, tm, stride=H_B), :] = k[:, cols]
        vt_ref[pl.ds(h, tm, stride=H_B), :] = v[:, cols]
    gate_ref[...] = jax.nn.sigmoid(
        jnp.dot(xb, w_ref[:, OFF_GA:N_COLS], preferred_element_type=F32))
    return xb


def _inproj_prompt_kernel(x_ref, g_ref, w_ref, mu_ref,
                          pr_ref, q_ref, k_ref, v_ref, kt_ref, vt_ref, gate_ref, xl_ref,
                          carry_ref):
    tm = x_ref.shape[0]
    xn = _rms(x_ref[...], g_ref[...], NORM_EPS)
    xl_ref[0] = xn[tm - 1:tm, :]
    xb = _proj_tail(xn, w_ref, q_ref, k_ref, v_ref, kt_ref, vt_ref, gate_ref)
    pr = jnp.dot(xb, w_ref[:, 0:RWKV_COLS], preferred_element_type=F32)

    @pl.when(pl.program_id(1) == 0)
    def _():
        carry_ref[...] = jnp.zeros_like(carry_ref)

    prev = pltpu.roll(pr, 1, 0)
    row = lax.broadcasted_iota(jnp.int32, pr.shape, 0)
    prev = jnp.where(row == 0, carry_ref[SUBLANES - 1:SUBLANES, :], prev)
    carry_ref[...] = pr[tm - SUBLANES:tm, :]
    pr_ref[...] = pr + (prev - pr) * mu_ref[...]


def _inproj_prompt(x2d, seq, g_mix, w_in, mu):
    m = x2d.shape[0]
    nb = m // seq
    tm = min(256, seq)
    nt = seq // tm
    row_spec = lambda w: pl.BlockSpec((tm, w), lambda b, t: (b * nt + t, 0))
    head_spec = pl.BlockSpec((tm * H_B, 2 * HD_B), lambda b, t: (b * nt + t, 0))
    return pl.pallas_call(
        _inproj_prompt_kernel,
        grid=(nb, nt),
        in_specs=[row_spec(D_MODEL), _const_spec((1, D_MODEL)),
                  _const_spec((D_MODEL, N_COLS)), _const_spec((1, RWKV_COLS))],
        out_specs=[row_spec(RWKV_COLS), row_spec(W_B), row_spec(W_B), row_spec(W_B),
                   head_spec, head_spec, row_spec(GATE_COLS),
                   pl.BlockSpec((1, 1, D_MODEL), lambda b, t: (b, 0, 0))],
        out_shape=[jax.ShapeDtypeStruct((m, RWKV_COLS), F32),
                   jax.ShapeDtypeStruct((m, W_B), F32),
                   jax.ShapeDtypeStruct((m, W_B), F32),
                   jax.ShapeDtypeStruct((m, W_B), F32),
                   jax.ShapeDtypeStruct((m * H_B, 2 * HD_B), F32),
                   jax.ShapeDtypeStruct((m * H_B, 2 * HD_B), F32),
                   jax.ShapeDtypeStruct((m, GATE_COLS), F32),
                   jax.ShapeDtypeStruct((nb, 1, D_MODEL), F32)],
        scratch_shapes=[pltpu.VMEM((SUBLANES, RWKV_COLS), F32)],
        compiler_params=_params(("parallel", "arbitrary")),
        name="inproj_prompt",
    )(x2d, g_mix, w_in, mu)


def _inproj_sample_kernel(x_ref, sp_ref, g_ref, w_ref, mu_ref,
                          pr_ref, q_ref, k_ref, v_ref, kt_ref, vt_ref, gate_ref, xn_ref, *,
                          seq):
    xn = _rms(x_ref[...], g_ref[...], NORM_EPS)
    xn_ref[...] = xn
    xb = _proj_tail(xn, w_ref, q_ref, k_ref, v_ref, kt_ref, vt_ref, gate_ref)
    row = lax.broadcasted_iota(jnp.int32, xn.shape, 0)
    xprev = jnp.where(row % seq == 0, sp_ref[...], pltpu.roll(xn, 1, 0))
    w_r = w_ref[:, 0:RWKV_COLS]
    pr = jnp.dot(xb, w_r, preferred_element_type=F32)
    prev = jnp.dot(xprev.astype(BF16), w_r, preferred_element_type=F32)
    pr_ref[...] = pr + (prev - pr) * mu_ref[...]


def _inproj_sample(x2d, seq, shift_pad, g_mix, w_in, mu):
    m = x2d.shape[0]
    tm = min(256, m)
    assert tm % seq == 0 and m % tm == 0
    row_spec = lambda w: pl.BlockSpec((tm, w), lambda i: (i, 0))
    head_spec = pl.BlockSpec((tm * H_B, 2 * HD_B), lambda i: (i, 0))
    return pl.pallas_call(
        functools.partial(_inproj_sample_kernel, seq=seq),
        grid=(m // tm,),
        in_specs=[row_spec(D_MODEL), row_spec(D_MODEL), _const_spec((1, D_MODEL)),
                  _const_spec((D_MODEL, N_COLS)), _const_spec((1, RWKV_COLS))],
        out_specs=[row_spec(RWKV_COLS), row_spec(W_B), row_spec(W_B), row_spec(W_B),
                   head_spec, head_spec, row_spec(GATE_COLS), row_spec(D_MODEL)],
        out_shape=[jax.ShapeDtypeStruct((m, RWKV_COLS), F32),
                   jax.ShapeDtypeStruct((m, W_B), F32),
                   jax.ShapeDtypeStruct((m, W_B), F32),
                   jax.ShapeDtypeStruct((m, W_B), F32),
                   jax.ShapeDtypeStruct((m * H_B, 2 * HD_B), F32),
                   jax.ShapeDtypeStruct((m * H_B, 2 * HD_B), F32),
                   jax.ShapeDtypeStruct((m, GATE_COLS), F32),
                   jax.ShapeDtypeStruct((m, D_MODEL), F32)],
        compiler_params=_params(("parallel",)),
        name="inproj_sample",
    )(x2d, shift_pad, g_mix, w_in, mu)


def _dot_f32(a, b):
    return jnp.dot(a, b, precision=lax.Precision.HIGHEST, preferred_element_type=F32)


def _rwkv_pre_kernel(pr_ref, w0_ref, w2_ref, a0_ref, a2_ref, g2_ref, kk_ref, ka_ref, rk_ref,
                     a_ref, b_ref, k_ref, r_ref, pc_ref, g_ref, bv_ref, p_scr, *, chunk):
    tm = pr_ref.shape[0]
    o_wd = 3 * W_A
    o_ad = o_wd + DECAY_LORA
    o_gd = o_ad + AAA_LORA
    k = pr_ref[:, W_A:2 * W_A]
    wd = pr_ref[:, o_wd:o_ad]
    ad = pr_ref[:, o_ad:o_gd]
    gd = pr_ref[:, o_gd:RWKV_COLS]
    z = -(w0_ref[...] + _dot_f32(jnp.tanh(wd), w2_ref[...]))
    softplus = jnp.maximum(z, 0.0) + jnp.log1p(jnp.exp(-jnp.abs(z)))
    w = -softplus - 0.5
    log_decay = -jnp.exp(w)
    r_i = lax.broadcasted_iota(jnp.int32, (tm, tm), 0)
    c_i = lax.broadcasted_iota(jnp.int32, (tm, tm), 1)
    tri = ((r_i // chunk == c_i // chunk) & (c_i <= r_i)).astype(BF16)
    hi = log_decay.astype(BF16)
    rest = log_decay - hi.astype(F32)
    mid = rest.astype(BF16)
    lo = (rest - mid.astype(F32)).astype(BF16)
    run = lambda t: jnp.dot(tri, t, preferred_element_type=F32)
    cum = run(hi) + run(mid) + run(lo)
    p_incl = jnp.exp(cum)
    p_inv = jnp.exp(-cum)
    p_prev = jnp.exp(cum - log_decay)
    for s in range(W_A // LANES):
        p_scr[s] = p_incl[:, s * LANES:(s + 1) * LANES]
        pc_ref[:, s * LANES:(s + 1) * LANES] = p_scr[s, pl.ds(chunk - 1, tm // chunk,
                                                                stride=chunk), :]
    a = jax.nn.sigmoid(a0_ref[...] + _dot_f32(ad, a2_ref[...]))
    g_ref[...] = _dot_f32(jax.nn.sigmoid(gd), g2_ref[...])
    kk = k * kk_ref[...]
    ones = _head_ones(W_A, HD_A)
    norm = jnp.sqrt(_head_sum(kk * kk, ones))
    kk = kk / jnp.maximum(norm, 1e-12)
    k_mod = k * (1.0 + (a - 1.0) * ka_ref[...])
    r = pr_ref[:, 0:W_A]
    v = pr_ref[:, 2 * W_A:3 * W_A]
    a_ref[...] = -kk * p_prev
    b_ref[...] = kk * a * p_inv
    k_ref[...] = k_mod * p_inv
    r_ref[...] = r * p_incl
    bv_ref[...] = _head_sum(r * k_mod * rk_ref[...], ones) * v


def _rwkv_pre(pr, w0, w2, a0, a2, g2, k_k, k_a, r_k, chunk):
    m = pr.shape[0]
    tm = min(256, m)
    assert tm % chunk == 0 and (tm // chunk) % SUBLANES == 0
    row_spec = lambda w: pl.BlockSpec((tm, w), lambda i: (i, 0))
    full = lambda a: _const_spec(a.shape)
    out = jax.ShapeDtypeStruct((m, W_A), F32)
    return pl.pallas_call(
        functools.partial(_rwkv_pre_kernel, chunk=chunk),
        grid=(m // tm,),
        in_specs=[row_spec(RWKV_COLS), full(w0), full(w2), full(a0), full(a2), full(g2),
                  full(k_k), full(k_a), full(r_k)],
        out_specs=[row_spec(W_A)] * 4 + [pl.BlockSpec((tm // chunk, W_A), lambda i: (i, 0))]
                  + [row_spec(W_A)] * 2,
        out_shape=[out] * 4 + [jax.ShapeDtypeStruct((m // chunk, W_A), F32)] + [out] * 2,
        scratch_shapes=[pltpu.VMEM((W_A // LANES, tm, LANES), F32)],
        compiler_params=_params(("parallel",)),
        name="rwkv_pre",
    )(pr, w0, w2, a0, a2, g2, k_k, k_a, r_k)


def _sublane_allsum(x):
    n = x.shape[0] // SUBLANES
    acc = x[0:SUBLANES]
    for g in range(1, n):
        acc = acc + x[g * SUBLANES:(g + 1) * SUBLANES]
    shift = SUBLANES // 2
    while shift >= 1:
        acc = acc + pltpu.roll(acc, shift, 0)
        shift //= 2
    return acc


def _rwkv_scan_kernel(a_ref, b_ref, k_ref, r_ref, v_ref, pc_ref, s0_ref, y_ref, s_ref):
    tc = a_ref.shape[0]
    ni, nkey = s0_ref.shape[0], s0_ref.shape[1]
    ng = nkey // SUBLANES
    pad_rows = y_ref.shape[1] - ni

    @pl.when(pl.program_id(1) == 0)
    def _():
        s_ref[...] = s0_ref[...]

    sub = lax.broadcasted_iota(jnp.int32, (SUBLANES, LANES), 0)
    keys = pl.ds(0, nkey)

    def tile8(x):
        return jnp.concatenate([x] * ng, axis=0)

    def step(t, carry):
        def group(ig, carry2):
            y_acc = jnp.zeros((SUBLANES, LANES), F32)
            for ii in range(SUBLANES):
                i = ig * SUBLANES + ii
                s_i = s_ref[i]
                sa = _sublane_allsum(s_i * a_ref[t, keys])
                v_i = jnp.broadcast_to(v_ref[t, pl.ds(i, 1), :], (SUBLANES, LANES))
                s_new = s_i + tile8(sa) * b_ref[t, keys] + tile8(v_i) * k_ref[t, keys]
                s_ref[i] = s_new
                y_i = _sublane_allsum(s_new * r_ref[t, keys])
                y_acc = jnp.where(sub == ii, y_i, y_acc)
            y_ref[t, pl.ds(pl.multiple_of(ig * SUBLANES, SUBLANES), SUBLANES), :] = y_acc
            return carry2

        n_groups = ni // SUBLANES
        if n_groups <= 2:
            for ig in range(n_groups):
                group(ig, 0)
        else:
            lax.fori_loop(0, n_groups, group, 0)
        if pad_rows:
            y_ref[t, pl.ds(ni, pad_rows), :] = jnp.zeros((pad_rows, LANES), F32)
        return carry

    lax.fori_loop(0, tc, step, 0, unroll=2)

    def rescale(i, carry):
        s_ref[i] = s_ref[i] * pc_ref[0, keys]
        return carry

    lax.fori_loop(0, ni, rescale, 0, unroll=8)


def _rwkv_scan(a, b, k, r, v, pc, s0, tc):
    g, t, pk, _ = a.shape
    pv = v.shape[2]
    _, ni, nkey, _ = s0.shape
    x_spec = pl.BlockSpec((None, tc, pk, LANES), lambda gi, ti: (gi, ti, 0, 0))
    v_spec = pl.BlockSpec((None, tc, pv, LANES), lambda gi, ti: (gi, ti, 0, 0))
    s_spec = pl.BlockSpec((None, ni, nkey, LANES), lambda gi, ti: (gi, 0, 0, 0))
    pc_spec = pl.BlockSpec((None, 1, pc.shape[2], LANES), lambda gi, ti: (gi, ti, 0, 0))
    return pl.pallas_call(
        _rwkv_scan_kernel,
        grid=(g, t // tc),
        in_specs=[x_spec] * 4 + [v_spec, pc_spec, s_spec],
        out_specs=[v_spec, s_spec],
        out_shape=[jax.ShapeDtypeStruct((g, t, pv, LANES), F32),
                   jax.ShapeDtypeStruct((g, ni, nkey, LANES), F32)],
        compiler_params=_params(("parallel", "arbitrary")),
        name="rwkv_scan",
    )(a, b, k, r, v, pc, s0)


KEY_PITCH = HD_A + SUBLANES
TIME_TILE = LANES


def _val_pitch(n_rows):
    p = n_rows + SUBLANES
    return p if (p // SUBLANES) % 2 else p + SUBLANES


def _to_scan_kernel(x_ref, g_ref, c_scr, *, value_indexed, pitch):
    nb, tt, _ = x_ref.shape
    chains = nb * H_A
    groups = LANES // chains
    heads_per_tile = LANES // HD_A
    for b in range(nb):
        for hp in range(W_A // LANES):
            tile_t = x_ref[b, :, hp * LANES:(hp + 1) * LANES].T
            for h2 in range(heads_per_tile):
                c = b * H_A + hp * heads_per_tile + h2
                c_scr[c * KEY_PITCH:c * KEY_PITCH + HD_A, :] = tile_t[h2 * HD_A:(h2 + 1) * HD_A]
    chain_rows = lambda j: c_scr[pl.ds(j, chains, stride=KEY_PITCH), :]
    n_out = HD_A // groups if value_indexed else HD_A
    for n in range(n_out):
        if value_indexed:
            rows = [chain_rows(n * groups + r) for r in range(groups)]
        else:
            rows = [chain_rows(n)] * groups
        g_ref[pl.ds(n, tt, stride=pitch), :] = jnp.concatenate(rows, axis=0).T
    for n in range(n_out, pitch):
        g_ref[pl.ds(n, tt, stride=pitch), :] = jnp.zeros((tt, LANES), F32)


def _to_scan(x3, col_block, value_indexed):
    nb, t, _ = x3.shape
    groups = LANES // (nb * H_A)
    pitch = _val_pitch(HD_A // groups) if value_indexed else KEY_PITCH
    out = pl.pallas_call(
        functools.partial(_to_scan_kernel, value_indexed=value_indexed, pitch=pitch),
        grid=(t // TIME_TILE,),
        in_specs=[pl.BlockSpec((nb, TIME_TILE, W_A), lambda i: (0, i, col_block))],
        out_specs=pl.BlockSpec((TIME_TILE * pitch, LANES), lambda i: (i, 0)),
        out_shape=jax.ShapeDtypeStruct((t * pitch, LANES), F32),
        scratch_shapes=[pltpu.VMEM((nb * H_A * KEY_PITCH, TIME_TILE), F32)],
        compiler_params=_params(("parallel",)),
        name="to_scan",
    )(x3)
    return out.reshape(t, pitch, LANES)


def _from_scan_kernel(y_ref, lnw_ref, lnb_ref, o_ref, c_scr, *, pitch):
    nb, tt, _ = o_ref.shape
    chains = nb * H_A
    groups = LANES // chains
    heads_per_tile = LANES // HD_A
    for n in range(HD_A // groups):
        lanes_t = y_ref[pl.ds(n, tt, stride=pitch), :].T
        for r in range(groups):
            c_scr[pl.ds(n * groups + r, chains, stride=KEY_PITCH), :] = (
                lanes_t[r * chains:(r + 1) * chains])
    for c in range(chains):
        rows = pl.ds(c * KEY_PITCH, HD_A)
        head = c % H_A
        y = c_scr[rows, :]
        mean = _sublane_allsum(y) * (1.0 / HD_A)
        d = y - jnp.concatenate([mean] * (HD_A // SUBLANES), axis=0)
        var = _sublane_allsum(d * d) * (1.0 / HD_A)
        inv = jnp.concatenate([lax.rsqrt(var + GN_EPS)] * (HD_A // SUBLANES), axis=0)
        c_scr[rows, :] = (d * inv * lnw_ref[head * HD_A:(head + 1) * HD_A, :]
                          + lnb_ref[head * HD_A:(head + 1) * HD_A, :])
    for b in range(nb):
        for hp in range(W_A // LANES):
            c0 = b * H_A + hp * heads_per_tile
            tile_t = jnp.concatenate(
                [c_scr[(c0 + h2) * KEY_PITCH:(c0 + h2) * KEY_PITCH + HD_A, :]
                 for h2 in range(heads_per_tile)], axis=0)
            o_ref[b, :, hp * LANES:(hp + 1) * LANES] = tile_t.T


def _from_scan(y, nb, ln_w, ln_b):
    t, pitch, _ = y.shape
    cols = lambda p: jnp.broadcast_to(p.reshape(W_A, 1), (W_A, TIME_TILE))
    return pl.pallas_call(
        functools.partial(_from_scan_kernel, pitch=pitch),
        grid=(t // TIME_TILE,),
        in_specs=[pl.BlockSpec((TIME_TILE * pitch, LANES), lambda i: (i, 0)),
                  _const_spec((W_A, TIME_TILE)), _const_spec((W_A, TIME_TILE))],
        out_specs=pl.BlockSpec((nb, TIME_TILE, W_A), lambda i: (0, i, 0)),
        out_shape=jax.ShapeDtypeStruct((nb, t, W_A), F32),
        scratch_shapes=[pltpu.VMEM((nb * H_A * KEY_PITCH, TIME_TILE), F32)],
        compiler_params=_params(("parallel",)),
        name="from_scan",
    )(y.reshape(t * pitch, LANES), cols(ln_w), cols(ln_b))


def _group_norm_kernel(y_ref, lnw_ref, lnb_ref, o_ref):
    ones = _head_ones(W_A, HD_A)
    y = y_ref[...]
    mean = _head_sum(y, ones) * (1.0 / HD_A)
    d = y - mean
    var = _head_sum(d * d, ones) * (1.0 / HD_A)
    o_ref[...] = d * lax.rsqrt(var + GN_EPS) * lnw_ref[...] + lnb_ref[...]


def _group_norm(y, ln_w, ln_b):
    m = y.shape[0]
    tm = min(256, m)
    row_spec = pl.BlockSpec((tm, W_A), lambda i: (i, 0))
    vec = _const_spec((1, W_A))
    return pl.pallas_call(
        _group_norm_kernel,
        grid=(m // tm,),
        in_specs=[row_spec, vec, vec],
        out_specs=row_spec,
        out_shape=jax.ShapeDtypeStruct((m, W_A), F32),
        compiler_params=_params(("parallel",)),
        name="group_norm",
    )(y, ln_w, ln_b)


def _lam_value(lam_ref):
    lv = lam_ref[...]
    s1 = jnp.sum(lv[0:1] * lv[1:2], axis=-1, keepdims=True)
    s2 = jnp.sum(lv[2:3] * lv[3:4], axis=-1, keepdims=True)
    return jnp.exp(s1) - jnp.exp(s2) + LAM_INIT


POS_SPLIT = 16
ONES_ROWS = 16


def _attn_prompt_kernel(q_ref, k_ref, v_ref, lam_ref, sg_ref, o_ref,
                        kaug_ref, vt_ref, m_ref, acc_ref, sa_ref, sb_ref):
    h = pl.program_id(1)
    qi = pl.program_id(2)
    tq = q_ref.shape[0]
    nblk = k_ref.shape[0] // tq
    width = 2 * HD_B
    nt = (((1,), (1,)), ((), ()))
    slope = jnp.exp2(jnp.full((1, width), -8.0 / H_B, F32) * (h + 1).astype(F32))

    lane = lax.broadcasted_iota(jnp.int32, (tq, width), 1)
    pos = lax.broadcasted_iota(jnp.int32, (tq, width), 0)
    pos_hi = (pos // POS_SPLIT).astype(F32)
    pos_lo = (pos % POS_SPLIT).astype(F32)
    one = jnp.ones((tq, width), F32)

    def features(base, vals):
        out = jnp.zeros((tq, width), F32)
        for i, val in enumerate(vals):
            out = jnp.where(lane == base + i, val, out)
        return out

    own = [(lane >= c * HD_B) & (lane < (c + 1) * HD_B) for c in range(2)]
    other = [(1 - c) * HD_B for c in range(2)]

    @pl.when(qi == 0)
    def _():
        for c in range(2):
            k_feat = features(other[c], [POS_SPLIT * pos_hi, pos_lo, -POS_SPLIT * one, -one])
            k_feat = k_feat * slope
            for j in range(nblk):
                kaug_ref[c, j] = jnp.where(own[c], k_ref[j * tq:(j + 1) * tq, :],
                                           k_feat).astype(BF16)
        for j in range(nblk):
            vt_ref[j] = jnp.concatenate(
                [v_ref[j * tq:(j + 1) * tq, :].T, jnp.ones((ONES_ROWS, tq), F32)],
                axis=0).astype(BF16)

    q = q_ref[...] * (HD_B ** -0.5)
    q_aug = [jnp.where(own[c], q, features(other[c], [one, one, pos_hi, pos_lo])).astype(BF16)
             for c in range(2)]

    m_ref[...] = jnp.full_like(m_ref, NEG)
    acc_ref[...] = jnp.zeros_like(acc_ref)

    def score(kj, s_ref):
        for c in range(2):
            s_ref[c] = lax.dot_general(kaug_ref[c, kj], q_aug[c], nt,
                                       preferred_element_type=F32)

    def col_max(s):
        chunk = s.shape[0] // SUBLANES
        part = s[0:chunk]
        for i in range(1, SUBLANES):
            part = jnp.maximum(part, s[i * chunk:(i + 1) * chunk])
        return jnp.max(part, axis=0, keepdims=True)

    def consume(kj, s_ref, masked):
        block_bias = -slope[:, 0:1] * (tq * (qi - kj)).astype(F32)
        v_t = vt_ref[kj]
        for c in range(2):
            s = s_ref[c]
            if masked:
                key_i = lax.broadcasted_iota(jnp.int32, (tq, tq), 0)
                qry_i = lax.broadcasted_iota(jnp.int32, (tq, tq), 1)
                s = jnp.where(key_i <= qry_i, s, NEG)
            m_old = m_ref[c]
            m_new = jnp.maximum(m_old, col_max(s) + block_bias)
            p = jnp.exp(s - (m_new - block_bias)).astype(BF16)
            acc_ref[c] = (jnp.exp(m_old - m_new) * acc_ref[c]
                          + jnp.dot(v_t, p, preferred_element_type=F32))
            m_ref[c] = m_new

    score(0, sa_ref)

    def pair(kk, carry):
        score(2 * kk + 1, sb_ref)
        consume(2 * kk, sa_ref, masked=False)
        score(2 * kk + 2, sa_ref)
        consume(2 * kk + 1, sb_ref, masked=False)
        return carry

    lax.fori_loop(0, qi // 2, pair, 0)

    @pl.when(qi % 2 == 1)
    def _():
        score(qi, sb_ref)
        consume(qi - 1, sa_ref, masked=False)
        consume(qi, sb_ref, masked=True)

    @pl.when(qi % 2 == 0)
    def _():
        consume(qi, sa_ref, masked=True)

    lam = _lam_value(lam_ref)
    a0 = acc_ref[0]
    a1 = acc_ref[1]
    att_t = a0[0:width] / a0[width:width + 1] - lam * (a1[0:width] / a1[width:width + 1])
    o_ref[...] = _rms(att_t.T, sg_ref[...], SUBLN_EPS) * (1.0 - LAM_INIT)


def _attn_prompt(q, k, v, seq, lam_rows, subln_g):
    m = q.shape[0]
    nb = m // seq
    tq = min(512, seq)
    assert tq % POS_SPLIT == 0 and tq // POS_SPLIT <= 256 and seq % tq == 0
    nq = seq // tq
    width = 2 * HD_B
    q_spec = pl.BlockSpec((tq, width), lambda b, h, qi: (b * nq + qi, h))
    kv_spec = pl.BlockSpec((seq, width), lambda b, h, qi: (b, h))
    return pl.pallas_call(
        _attn_prompt_kernel,
        grid=(nb, H_B, nq),
        in_specs=[q_spec, kv_spec, kv_spec,
                  pl.BlockSpec((4, HD_B), lambda *_: (0, 0)),
                  pl.BlockSpec((1, width), lambda *_: (0, 0))],
        out_specs=q_spec,
        out_shape=jax.ShapeDtypeStruct((m, W_B), F32),
        scratch_shapes=[pltpu.VMEM((2, nq, tq, width), BF16),
                        pltpu.VMEM((nq, width + ONES_ROWS, tq), BF16),
                        pltpu.VMEM((2, 1, tq), F32),
                        pltpu.VMEM((2, width + ONES_ROWS, tq), F32),
                        pltpu.VMEM((2, tq, tq), F32),
                        pltpu.VMEM((2, tq, tq), F32)],
        compiler_params=_params(("parallel", "parallel", "arbitrary")),
        name="attn_prompt",
    )(q, k, v, lam_rows, subln_g)


def _attn_sample_kernel(pt_ref, q_ref, kn_ref, vn_ref, lam_ref, sg_ref, *rest,
                        n_pages, past):
    del pt_ref
    k_pages = rest[:n_pages]
    v_pages = rest[n_pages:2 * n_pages]
    o_ref, s_ref = rest[2 * n_pages], rest[2 * n_pages + 1]
    tq = q_ref.shape[0]
    rows = 2 * H_B * tq
    nt = (((1,), (1,)), ((), ()))

    q = q_ref[...] * (HD_B ** -0.5)
    q_rep = jnp.concatenate([q] * (2 * H_B), axis=0)
    r_i = lax.broadcasted_iota(jnp.int32, (rows, W_B), 0)
    c_i = lax.broadcasted_iota(jnp.int32, (rows, W_B), 1)
    q_blk = jnp.where(r_i // tq == c_i // HD_B, q_rep, 0.0).astype(BF16)

    row = lax.broadcasted_iota(jnp.int32, (rows, PAGE_SIZE), 0)
    lane = lax.broadcasted_iota(jnp.int32, (rows, PAGE_SIZE), 1)
    head = row // (2 * tq)
    slope = jnp.exp2(-8.0 * (head + 1).astype(F32) / H_B)
    q_pos = past + row % tq

    def page(ref):
        return jnp.concatenate([ref[pl.ds(h, PAGE_SIZE, stride=H_B), :] for h in range(H_B)],
                               axis=-1).astype(BF16)

    for p in range(n_pages):
        s = lax.dot_general(q_blk, page(k_pages[p]), nt, preferred_element_type=F32)
        dist = q_pos - (p * PAGE_SIZE + lane)
        s_ref[:, p * PAGE_SIZE:(p + 1) * PAGE_SIZE] = s - slope * dist.astype(F32)

    kn = jnp.concatenate(
        [kn_ref[...], jnp.zeros((PAGE_SIZE - tq, W_B), F32)], axis=0).astype(BF16)
    s = lax.dot_general(q_blk, kn, nt, preferred_element_type=F32)
    dist = q_pos - (past + lane)
    s = jnp.where((dist >= 0) & (lane < tq), s - slope * dist.astype(F32), NEG)
    s_ref[:, past:past + PAGE_SIZE] = s

    s = s_ref[...]
    mx = jnp.max(s, axis=-1, keepdims=True)
    p_un = jnp.exp(s - mx)
    pn = p_un / jnp.sum(p_un, axis=-1, keepdims=True)
    lam = _lam_value(lam_ref)
    attn = jnp.concatenate(
        [pn[(2 * h) * tq:(2 * h + 1) * tq] - lam * pn[(2 * h + 1) * tq:(2 * h + 2) * tq]
         for h in range(H_B)], axis=0).astype(BF16)

    acc = jnp.zeros((H_B * tq, W_B), F32)
    for p in range(n_pages):
        acc = acc + jnp.dot(attn[:, p * PAGE_SIZE:(p + 1) * PAGE_SIZE],
                            page(v_pages[p]), preferred_element_type=F32)
    vn = jnp.concatenate(
        [vn_ref[...], jnp.zeros((PAGE_SIZE - tq, W_B), F32)], axis=0).astype(BF16)
    acc = acc + jnp.dot(attn[:, past:past + PAGE_SIZE], vn, preferred_element_type=F32)

    width = 2 * HD_B
    outs = []
    for h in range(H_B):
        att = acc[h * tq:(h + 1) * tq, h * width:(h + 1) * width]
        outs.append(_rms(att, sg_ref[...], SUBLN_EPS) * (1.0 - LAM_INIT))
    o_ref[...] = jnp.concatenate(outs, axis=-1)


def _attn_sample(q, k_new, v_new, cache_k, cache_v, page_table, lam_rows, subln_g):
    bd, tq, _ = q.shape
    n_pages = page_table.shape[1]
    past = n_pages * PAGE_SIZE
    tok_spec = pl.BlockSpec((None, tq, W_B), lambda b, pt: (b, 0, 0))
    page_specs = [pl.BlockSpec((None, PAGE_SIZE * H_B, 2 * HD_B),
                               lambda b, pt, p=p: (pt[b * n_pages + p], 0, 0))
                  for p in range(n_pages)]
    grid_spec = pltpu.PrefetchScalarGridSpec(
        num_scalar_prefetch=1,
        grid=(bd,),
        in_specs=[tok_spec, tok_spec, tok_spec,
                  pl.BlockSpec((4, HD_B), lambda b, pt: (0, 0)),
                  pl.BlockSpec((1, 2 * HD_B), lambda b, pt: (0, 0))]
                 + page_specs + page_specs,
        out_specs=tok_spec,
        scratch_shapes=[pltpu.VMEM((2 * H_B * tq, past + PAGE_SIZE), F32)],
    )
    return pl.pallas_call(
        functools.partial(_attn_sample_kernel, n_pages=n_pages, past=past),
        grid_spec=grid_spec,
        out_shape=jax.ShapeDtypeStruct((bd, tq, W_B), F32),
        compiler_params=_params(("parallel",)),
        name="attn_sample",
    )(page_table.reshape(-1), q, k_new, v_new, lam_rows, subln_g,
      *([cache_k] * n_pages), *([cache_v] * n_pages))


def _merge_ffn_kernel(x_ref, yn_ref, bv_ref, g_ref, ob_ref, gate_ref, wa_ref, wb_ref, wo_ref,
                      gf_ref, wu_ref, wd_ref, gl_ref, y_ref):
    ga = gate_ref[:, 0:D_MODEL]
    gb = gate_ref[:, D_MODEL:GATE_COLS]
    o_a = (yn_ref[...] + bv_ref[...]) * g_ref[...]
    m = ga * _mm(o_a, wa_ref[...]) + gb * _mm(ob_ref[...], wb_ref[...])
    x1 = x_ref[...] + _mm(m, wo_ref[...])
    hn = _rms(x1, gf_ref[...], NORM_EPS).astype(BF16)
    acc = x1
    chunk = D_MODEL
    for c in range(D_FF // chunk):
        up = jnp.dot(hn, wu_ref[:, c * chunk:(c + 1) * chunk], preferred_element_type=F32)
        act = jnp.square(jnp.maximum(up, 0.0))
        acc = acc + _mm(act, wd_ref[c * chunk:(c + 1) * chunk, :])
    y_ref[...] = _rms(acc, gl_ref[...], NORM_EPS)


def _merge_ffn(x2d, yn, bv, g, o_b, gates, w_br_a, w_br_b, w_out, g_ffn, w_up, w_down,
               g_final):
    m = x2d.shape[0]
    tm = min(256, m)
    row_spec = lambda w: pl.BlockSpec((tm, w), lambda i: (i, 0))
    full = lambda a: _const_spec(a.shape)
    return pl.pallas_call(
        _merge_ffn_kernel,
        grid=(m // tm,),
        in_specs=[row_spec(D_MODEL), row_spec(W_A), row_spec(W_A), row_spec(W_A),
                  row_spec(W_B), row_spec(GATE_COLS),
                  full(w_br_a), full(w_br_b), full(w_out), full(g_ffn), full(w_up),
                  full(w_down), full(g_final)],
        out_specs=row_spec(D_MODEL),
        out_shape=jax.ShapeDtypeStruct((m, D_MODEL), F32),
        compiler_params=_params(("parallel",)),
        name="merge_ffn",
    )(x2d, yn, bv, g, o_b, gates, w_br_a, w_br_b, w_out, g_ffn, w_up, w_down, g_final)


def _rwkv_branch_prompt(pr, a_s, b_s, k_s, r_s, pc, nb, seq, tc, ln_w, ln_b):
    groups = LANES // (nb * H_A)
    ni = HD_A // groups
    nat = lambda x: x.reshape(nb, seq, x.shape[-1])
    key = lambda x: _to_scan(nat(x), 0, value_indexed=False)[None]
    v = _to_scan(nat(pr), 2, value_indexed=True)[None]
    s0 = jnp.zeros((1, ni, HD_A, LANES), F32)
    pc = pc.reshape(nb, seq // tc, H_A, HD_A).transpose(1, 3, 0, 2)
    pc = jnp.tile(pc.reshape(1, seq // tc, HD_A, nb * H_A), (1, 1, 1, groups))
    y, s = _rwkv_scan(key(a_s), key(b_s), key(k_s), key(r_s), v, pc, s0, tc=tc)
    y = _from_scan(y[0], nb, ln_w, ln_b).reshape(nb * seq, W_A)
    s = s.reshape(ni, HD_A, groups, nb, H_A).transpose(3, 4, 0, 2, 1)
    return y, s.reshape(nb, H_A, HD_A, HD_A)


def _rwkv_branch_sample(pr, a_s, b_s, k_s, r_s, pc, state, nb, seq, ln_w, ln_b):
    chains = nb * H_A
    ng = chains // LANES

    def lay(x):
        rows = x.shape[0] // nb
        x = x.reshape(nb, rows, H_A, HD_A).transpose(1, 3, 0, 2).reshape(rows, HD_A, ng, LANES)
        return x.transpose(2, 0, 1, 3)

    s0 = state.transpose(2, 3, 0, 1).reshape(HD_A, HD_A, ng, LANES).transpose(2, 0, 1, 3)
    y, s = _rwkv_scan(lay(a_s), lay(b_s), lay(k_s), lay(r_s), lay(pr[:, 2 * W_A:3 * W_A]),
                      lay(pc), s0, tc=seq)
    y = y.transpose(1, 2, 0, 3).reshape(seq, HD_A, nb, H_A).transpose(2, 0, 3, 1)
    s = s.transpose(1, 2, 0, 3).reshape(HD_A, HD_A, nb, H_A).transpose(2, 3, 0, 1)
    return _group_norm(y.reshape(nb * seq, W_A), ln_w, ln_b), s


def kernel(x_prompt, x_sample, cache_k, cache_v, state_rwkv, state_shift, page_table,
           w_in, mu_shift, w0, w2, a0, a2, g2, k_k, k_a, r_k, ln_x_w, ln_x_b,
           lam_q1, lam_k1, lam_q2, lam_k2, subln_g, w_br_a, w_br_b, w_out,
           g_mix, g_ffn, w_up, w_down, g_final):
    nb, seq, _ = x_prompt.shape
    bd, dseq, _ = x_sample.shape
    w_in_b = w_in[0].astype(BF16)
    wa_b, wb_b, wo_b = (w[0].astype(BF16) for w in (w_br_a, w_br_b, w_out))
    wu_b, wd_b = w_up[0].astype(BF16), w_down[0].astype(BF16)
    r_k2 = r_k.reshape(1, W_A)
    lam_rows = jnp.concatenate([lam_q1, lam_k1, lam_q2, lam_k2], axis=0)
    g_fin = g_final.reshape(1, D_MODEL)

    def rwkv_ops(pr, chunk):
        return _rwkv_pre(pr, w0, w2[0], a0, a2[0], g2[0], k_k, k_a, r_k2, chunk)

    def finish(x2d, yn, bv, g, o_b, gates):
        return _merge_ffn(x2d, yn, bv, g, o_b, gates, wa_b, wb_b, wo_b, g_ffn, wu_b, wd_b,
                          g_fin)

    xp = x_prompt.reshape(nb * seq, D_MODEL)
    pr, q, k, v, kt, vt, gates, x_last = _inproj_prompt(xp, seq, g_mix, w_in_b, mu_shift)
    tc = min(SCAN_CHUNK, seq)
    a_s, b_s, k_s, r_s, pc, g, bv = rwkv_ops(pr, tc)
    yn, s_prompt = _rwkv_branch_prompt(pr, a_s, b_s, k_s, r_s, pc, nb, seq, tc,
                                       ln_x_w, ln_x_b)
    o_b = _attn_prompt(q, k, v, seq, lam_rows, subln_g)
    y_prompt = finish(xp, yn, bv, g, o_b, gates).reshape(nb, seq, D_MODEL)
    k_prompt = kt.reshape(1, nb, seq, H_B, 2 * HD_B)
    v_prompt = vt.reshape(1, nb, seq, H_B, 2 * HD_B)
    shift_prompt = x_last.reshape(1, nb, D_MODEL)

    xs = x_sample.reshape(bd * dseq, D_MODEL)
    shift_pad = jnp.zeros((bd, dseq, D_MODEL), F32).at[:, 0].set(state_shift[0])
    pr, q, k, v, kt, vt, gates, xn = _inproj_sample(
        xs, dseq, shift_pad.reshape(bd * dseq, D_MODEL), g_mix, w_in_b, mu_shift)
    a_s, b_s, k_s, r_s, pc, g, bv = rwkv_ops(pr, dseq)
    yn, s_sample = _rwkv_branch_sample(pr, a_s, b_s, k_s, r_s, pc, state_rwkv[0], bd, dseq,
                                       ln_x_w, ln_x_b)
    tok = lambda t: t.reshape(bd, dseq, W_B)
    pages = lambda c: c[0].reshape(c.shape[1], PAGE_SIZE * H_B, 2 * HD_B)
    o_b = _attn_sample(tok(q), tok(k), tok(v), pages(cache_k), pages(cache_v),
                       page_table, lam_rows, subln_g)
    y_sample = finish(xs, yn, bv, g, o_b.reshape(bd * dseq, W_B), gates)
    y_sample = y_sample.reshape(bd, dseq, D_MODEL)
    k_sample = kt.reshape(1, bd, dseq, H_B, 2 * HD_B)
    v_sample = vt.reshape(1, bd, dseq, H_B, 2 * HD_B)
    shift_sample = xn.reshape(bd, dseq, D_MODEL)[:, -1][None]

    return (y_prompt, y_sample, k_prompt, v_prompt, k_sample, v_sample,
            s_prompt[None], s_sample[None], shift_prompt, shift_sample)
```

```python
import functools

import jax
import jax.numpy as jnp
from jax import lax
from jax.experimental import pallas as pl
from jax.experimental.pallas import tpu as pltpu

F32 = jnp.float32
BF16 = jnp.bfloat16

D_MODEL = 1024
H_A = 8
HD_A = 64
W_A = H_A * HD_A
DECAY_LORA = 64
AAA_LORA = 64
GATE_LORA = 128
RWKV_COLS = 3 * W_A + DECAY_LORA + AAA_LORA + GATE_LORA
H_B = 4
HD_B = 64
W_B = H_B * 2 * HD_B
GATE_COLS = 2 * D_MODEL
N_COLS = RWKV_COLS + 3 * W_B + GATE_COLS
D_FF = 4 * D_MODEL
PAGE_SIZE = 128
NORM_EPS = 1e-6
GN_EPS = 64e-5
SUBLN_EPS = 1e-5
NEG = -1e30
LAM_INIT = 0.8 - 0.6 * 1.0

LANES = 128
SUBLANES = 8
MXU_TILE = 256
VMEM_LIMIT = 56 * 1024 * 1024
SCAN_CHUNK = 32
SAMPLE_SEQS_PER_STEP = 2

OFF_Q = RWKV_COLS
OFF_K = OFF_Q + W_B
OFF_V = OFF_K + W_B
OFF_GA = OFF_V + W_B


def _params(sem):
    return pltpu.CompilerParams(dimension_semantics=sem, vmem_limit_bytes=VMEM_LIMIT)


def _const_spec(shape):
    nd = len(shape)
    return pl.BlockSpec(shape, lambda *_: (0,) * nd, pipeline_mode=pl.Buffered(1))


def _rms(x, g, eps):
    return x * lax.rsqrt(jnp.mean(x * x, axis=-1, keepdims=True) + eps) * g


def _mm(a, w):
    return jnp.dot(a.astype(BF16), w, preferred_element_type=F32)


def _split2(x):
    hi = x.astype(BF16)
    lo = (x - hi.astype(F32)).astype(BF16)
    return hi, lo


def _head_sum(x, head):
    width = x.shape[1]
    tile = min(MXU_TILE, width)
    r = lax.broadcasted_iota(jnp.int32, (tile, tile), 0) // head
    c = lax.broadcasted_iota(jnp.int32, (tile, tile), 1) // head
    ones = (r == c).astype(BF16)
    hi, lo = _split2(x)
    parts = []
    for s in range(width // tile):
        cols = slice(s * tile, (s + 1) * tile)
        parts.append(jnp.dot(hi[:, cols], ones, preferred_element_type=F32)
                     + jnp.dot(lo[:, cols], ones, preferred_element_type=F32))
    return jnp.concatenate(parts, axis=-1)


def _proj_tail(xn, w_ref, q_ref, k_ref, v_ref, kt_ref, vt_ref, gate_ref):
    xb = xn.astype(BF16)
    tm = xn.shape[0]
    q_ref[...] = jnp.dot(xb, w_ref[:, OFF_Q:OFF_K], preferred_element_type=F32)
    k = jnp.dot(xb, w_ref[:, OFF_K:OFF_V], preferred_element_type=F32)
    v = jnp.dot(xb, w_ref[:, OFF_V:OFF_GA], preferred_element_type=F32)
    k_ref[...] = k
    v_ref[...] = v
    for h in range(H_B):
        cols = slice(h * 2 * HD_B, (h + 1) * 2 * HD_B)
        kt_ref[pl.ds(h, tm, stride=H_B), :] = k[:, cols]
        vt_ref[pl.ds(h, tm, stride=H_B), :] = v[:, cols]
    gate_ref[...] = jax.nn.sigmoid(
        jnp.dot(xb, w_ref[:, OFF_GA:N_COLS], preferred_element_type=F32))
    return xb


def _inproj_prompt_kernel(x_ref, g_ref, w_ref, mu_ref,
                          pr_ref, q_ref, k_ref, v_ref, kt_ref, vt_ref, gate_ref, xl_ref,
                          carry_ref):
    tm = x_ref.shape[0]
    xn = _rms(x_ref[...], g_ref[...], NORM_EPS)
    xl_ref[0] = xn[tm - 1:tm, :]
    xb = _proj_tail(xn, w_ref, q_ref, k_ref, v_ref, kt_ref, vt_ref, gate_ref)
    pr = jnp.dot(xb, w_ref[:, 0:RWKV_COLS], preferred_element_type=F32)

    @pl.when(pl.program_id(1) == 0)
    def _():
        carry_ref[...] = jnp.zeros_like(carry_ref)

    prev = pltpu.roll(pr, 1, 0)
    row = lax.broadcasted_iota(jnp.int32, pr.shape, 0)
    prev = jnp.where(row == 0, carry_ref[SUBLANES - 1:SUBLANES, :], prev)
    carry_ref[...] = pr[tm - SUBLANES:tm, :]
    pr_ref[...] = pr + (prev - pr) * mu_ref[...]


def _inproj_prompt(x2d, seq, g_mix, w_in, mu):
    m = x2d.shape[0]
    nb = m // seq
    tm = min(256, seq)
    nt = seq // tm
    row_spec = lambda w: pl.BlockSpec((tm, w), lambda b, t: (b * nt + t, 0))
    head_spec = pl.BlockSpec((tm * H_B, 2 * HD_B), lambda b, t: (b * nt + t, 0))
    return pl.pallas_call(
        _inproj_prompt_kernel,
        grid=(nb, nt),
        in_specs=[row_spec(D_MODEL), _const_spec((1, D_MODEL)),
                  _const_spec((D_MODEL, N_COLS)), _const_spec((1, RWKV_COLS))],
        out_specs=[row_spec(RWKV_COLS), row_spec(W_B), row_spec(W_B), row_spec(W_B),
                   head_spec, head_spec, row_spec(GATE_COLS),
                   pl.BlockSpec((1, 1, D_MODEL), lambda b, t: (b, 0, 0))],
        out_shape=[jax.ShapeDtypeStruct((m, RWKV_COLS), F32),
                   jax.ShapeDtypeStruct((m, W_B), F32),
                   jax.ShapeDtypeStruct((m, W_B), F32),
                   jax.ShapeDtypeStruct((m, W_B), F32),
                   jax.ShapeDtypeStruct((m * H_B, 2 * HD_B), F32),
                   jax.ShapeDtypeStruct((m * H_B, 2 * HD_B), F32),
                   jax.ShapeDtypeStruct((m, GATE_COLS), F32),
                   jax.ShapeDtypeStruct((nb, 1, D_MODEL), F32)],
        scratch_shapes=[pltpu.VMEM((SUBLANES, RWKV_COLS), F32)],
        compiler_params=_params(("parallel", "arbitrary")),
        name="inproj_prompt",
    )(x2d, g_mix, w_in, mu)


def _inproj_sample_kernel(x_ref, sp_ref, g_ref, w_ref, mu_ref,
                          pr_ref, q_ref, k_ref, v_ref, kt_ref, vt_ref, gate_ref, xn_ref, *,
                          seq):
    xn = _rms(x_ref[...], g_ref[...], NORM_EPS)
    xn_ref[...] = xn
    xb = _proj_tail(xn, w_ref, q_ref, k_ref, v_ref, kt_ref, vt_ref, gate_ref)
    row = lax.broadcasted_iota(jnp.int32, xn.shape, 0)
    xprev = jnp.where(row % seq == 0, sp_ref[...], pltpu.roll(xn, 1, 0))
    w_r = w_ref[:, 0:RWKV_COLS]
    pr = jnp.dot(xb, w_r, preferred_element_type=F32)
    prev = jnp.dot(xprev.astype(BF16), w_r, preferred_element_type=F32)
    pr_ref[...] = pr + (prev - pr) * mu_ref[...]


def _inproj_sample(x2d, seq, shift_pad, g_mix, w_in, mu):
    m = x2d.shape[0]
    tm = min(256, m)
    assert tm % seq == 0 and m % tm == 0
    row_spec = lambda w: pl.BlockSpec((tm, w), lambda i: (i, 0))
    head_spec = pl.BlockSpec((tm * H_B, 2 * HD_B), lambda i: (i, 0))
    return pl.pallas_call(
        functools.partial(_inproj_sample_kernel, seq=seq),
        grid=(m // tm,),
        in_specs=[row_spec(D_MODEL), row_spec(D_MODEL), _const_spec((1, D_MODEL)),
                  _const_spec((D_MODEL, N_COLS)), _const_spec((1, RWKV_COLS))],
        out_specs=[row_spec(RWKV_COLS), row_spec(W_B), row_spec(W_B), row_spec(W_B),
                   head_spec, head_spec, row_spec(GATE_COLS), row_spec(D_MODEL)],
        out_shape=[jax.ShapeDtypeStruct((m, RWKV_COLS), F32),
                   jax.ShapeDtypeStruct((m, W_B), F32),
                   jax.ShapeDtypeStruct((m, W_B), F32),
                   jax.ShapeDtypeStruct((m, W_B), F32),
                   jax.ShapeDtypeStruct((m * H_B, 2 * HD_B), F32),
                   jax.ShapeDtypeStruct((m * H_B, 2 * HD_B), F32),
                   jax.ShapeDtypeStruct((m, GATE_COLS), F32),
                   jax.ShapeDtypeStruct((m, D_MODEL), F32)],
        compiler_params=_params(("parallel",)),
        name="inproj_sample",
    )(x2d, shift_pad, g_mix, w_in, mu)


def _dot_f32(a, b):
    a_hi, a_lo = _split2(a)
    b_hi, b_lo = _split2(b)
    d = lambda x, y: jnp.dot(x, y, preferred_element_type=F32)
    return d(a_hi, b_hi) + d(a_lo, b_hi) + d(a_hi, b_lo)


def _rwkv_pre_kernel(pr_ref, w0_ref, w2_ref, a0_ref, a2_ref, g2_ref, kk_ref, ka_ref, rk_ref,
                     a_ref, b_ref, k_ref, r_ref, pc_ref, g_ref, bv_ref, p_scr, *, chunk):
    tm = pr_ref.shape[0]
    o_wd = 3 * W_A
    o_ad = o_wd + DECAY_LORA
    o_gd = o_ad + AAA_LORA
    k = pr_ref[:, W_A:2 * W_A]
    wd = pr_ref[:, o_wd:o_ad]
    ad = pr_ref[:, o_ad:o_gd]
    gd = pr_ref[:, o_gd:RWKV_COLS]
    z = -(w0_ref[...] + _dot_f32(jnp.tanh(wd), w2_ref[...]))
    softplus = jnp.maximum(z, 0.0) + jnp.log1p(jnp.exp(-jnp.abs(z)))
    w = -softplus - 0.5
    log_decay = -jnp.exp(w)
    r_i = lax.broadcasted_iota(jnp.int32, (tm, tm), 0)
    c_i = lax.broadcasted_iota(jnp.int32, (tm, tm), 1)
    tri = ((r_i // chunk == c_i // chunk) & (c_i <= r_i)).astype(BF16)
    hi = log_decay.astype(BF16)
    rest = log_decay - hi.astype(F32)
    mid = rest.astype(BF16)
    lo = (rest - mid.astype(F32)).astype(BF16)
    run = lambda t: jnp.dot(tri, t, preferred_element_type=F32)
    cum = run(hi) + run(mid) + run(lo)
    p_incl = jnp.exp(cum)
    p_inv = jnp.exp(-cum)
    p_prev = jnp.exp(cum - log_decay)
    for s in range(W_A // LANES):
        p_scr[s] = p_incl[:, s * LANES:(s + 1) * LANES]
        pc_ref[:, s * LANES:(s + 1) * LANES] = p_scr[s, pl.ds(chunk - 1, tm // chunk,
                                                                stride=chunk), :]
    a = jax.nn.sigmoid(a0_ref[...] + _dot_f32(ad, a2_ref[...]))
    g_ref[...] = _dot_f32(jax.nn.sigmoid(gd), g2_ref[...])
    kk = k * kk_ref[...]
    norm = jnp.sqrt(_head_sum(kk * kk, HD_A))
    kk = kk / jnp.maximum(norm, 1e-12)
    k_mod = k * (1.0 + (a - 1.0) * ka_ref[...])
    r = pr_ref[:, 0:W_A]
    v = pr_ref[:, 2 * W_A:3 * W_A]
    a_ref[...] = -kk * p_prev
    b_ref[...] = kk * a * p_inv
    k_ref[...] = k_mod * p_inv
    r_ref[...] = r * p_incl
    bv_ref[...] = _head_sum(r * k_mod * rk_ref[...], HD_A) * v


def _rwkv_pre(pr, w0, w2, a0, a2, g2, k_k, k_a, r_k, chunk):
    m = pr.shape[0]
    tm = min(256, m)
    assert tm % chunk == 0 and (tm // chunk) % SUBLANES == 0
    row_spec = lambda w: pl.BlockSpec((tm, w), lambda i: (i, 0))
    full = lambda a: _const_spec(a.shape)
    out = jax.ShapeDtypeStruct((m, W_A), F32)
    return pl.pallas_call(
        functools.partial(_rwkv_pre_kernel, chunk=chunk),
        grid=(m // tm,),
        in_specs=[row_spec(RWKV_COLS), full(w0), full(w2), full(a0), full(a2), full(g2),
                  full(k_k), full(k_a), full(r_k)],
        out_specs=[row_spec(W_A)] * 4 + [pl.BlockSpec((tm // chunk, W_A), lambda i: (i, 0))]
                  + [row_spec(W_A)] * 2,
        out_shape=[out] * 4 + [jax.ShapeDtypeStruct((m // chunk, W_A), F32)] + [out] * 2,
        scratch_shapes=[pltpu.VMEM((W_A // LANES, tm, LANES), F32)],
        compiler_params=_params(("parallel",)),
        name="rwkv_pre",
    )(pr, w0, w2, a0, a2, g2, k_k, k_a, r_k)


def _sublane_allsum(x):
    n = x.shape[0] // SUBLANES
    acc = x[0:SUBLANES]
    for g in range(1, n):
        acc = acc + x[g * SUBLANES:(g + 1) * SUBLANES]
    shift = SUBLANES // 2
    while shift >= 1:
        acc = acc + pltpu.roll(acc, shift, 0)
        shift //= 2
    return acc


def _rwkv_scan_kernel(a_ref, b_ref, k_ref, r_ref, v_ref, pc_ref, s0_ref, y_ref, s_ref):
    tc = a_ref.shape[0]
    ni, nkey = s0_ref.shape[0], s0_ref.shape[1]
    ng = nkey // SUBLANES
    pad_rows = y_ref.shape[1] - ni

    @pl.when(pl.program_id(1) == 0)
    def _():
        s_ref[...] = s0_ref[...]

    sub = lax.broadcasted_iota(jnp.int32, (SUBLANES, LANES), 0)
    keys = pl.ds(0, nkey)

    def tile8(x):
        return jnp.concatenate([x] * ng, axis=0)

    def step(t, carry):
        def group(ig, carry2):
            y_acc = jnp.zeros((SUBLANES, LANES), F32)
            for ii in range(SUBLANES):
                i = ig * SUBLANES + ii
                s_i = s_ref[i]
                sa = _sublane_allsum(s_i * a_ref[t, keys])
                v_i = jnp.broadcast_to(v_ref[t, pl.ds(i, 1), :], (SUBLANES, LANES))
                s_new = s_i + tile8(sa) * b_ref[t, keys] + tile8(v_i) * k_ref[t, keys]
                s_ref[i] = s_new
                y_i = _sublane_allsum(s_new * r_ref[t, keys])
                y_acc = jnp.where(sub == ii, y_i, y_acc)
            y_ref[t, pl.ds(pl.multiple_of(ig * SUBLANES, SUBLANES), SUBLANES), :] = y_acc
            return carry2

        n_groups = ni // SUBLANES
        if n_groups <= 2:
            for ig in range(n_groups):
                group(ig, 0)
        else:
            lax.fori_loop(0, n_groups, group, 0)
        if pad_rows:
            y_ref[t, pl.ds(ni, pad_rows), :] = jnp.zeros((pad_rows, LANES), F32)
        return carry

    lax.fori_loop(0, tc, step, 0, unroll=2)

    def rescale(i, carry):
        s_ref[i] = s_ref[i] * pc_ref[0, keys]
        return carry

    lax.fori_loop(0, ni, rescale, 0, unroll=8)


def _rwkv_scan(a, b, k, r, v, pc, s0, tc):
    g, t, pk, _ = a.shape
    pv = v.shape[2]
    _, ni, nkey, _ = s0.shape
    x_spec = pl.BlockSpec((None, tc, pk, LANES), lambda gi, ti: (gi, ti, 0, 0))
    v_spec = pl.BlockSpec((None, tc, pv, LANES), lambda gi, ti: (gi, ti, 0, 0))
    s_spec = pl.BlockSpec((None, ni, nkey, LANES), lambda gi, ti: (gi, 0, 0, 0))
    pc_spec = pl.BlockSpec((None, 1, pc.shape[2], LANES), lambda gi, ti: (gi, ti, 0, 0))
    return pl.pallas_call(
        _rwkv_scan_kernel,
        grid=(g, t // tc),
        in_specs=[x_spec] * 4 + [v_spec, pc_spec, s_spec],
        out_specs=[v_spec, s_spec],
        out_shape=[jax.ShapeDtypeStruct((g, t, pv, LANES), F32),
                   jax.ShapeDtypeStruct((g, ni, nkey, LANES), F32)],
        compiler_params=_params(("parallel", "arbitrary")),
        name="rwkv_scan",
    )(a, b, k, r, v, pc, s0)


KEY_PITCH = HD_A + SUBLANES
TIME_TILE = LANES


def _val_pitch(n_rows):
    p = n_rows + SUBLANES
    return p if (p // SUBLANES) % 2 else p + SUBLANES


def _to_scan_kernel(x_ref, g_ref, c_scr, *, value_indexed, pitch):
    nb, tt, _ = x_ref.shape
    chains = nb * H_A
    groups = LANES // chains
    heads_per_tile = LANES // HD_A
    for b in range(nb):
        for hp in range(W_A // LANES):
            tile_t = x_ref[b, :, hp * LANES:(hp + 1) * LANES].T
            for h2 in range(heads_per_tile):
                c = b * H_A + hp * heads_per_tile + h2
                c_scr[c * KEY_PITCH:c * KEY_PITCH + HD_A, :] = tile_t[h2 * HD_A:(h2 + 1) * HD_A]
    chain_rows = lambda j: c_scr[pl.ds(j, chains, stride=KEY_PITCH), :]
    n_out = HD_A // groups if value_indexed else HD_A
    for n in range(n_out):
        if value_indexed:
            rows = [chain_rows(n * groups + r) for r in range(groups)]
        else:
            rows = [chain_rows(n)] * groups
        g_ref[pl.ds(n, tt, stride=pitch), :] = jnp.concatenate(rows, axis=0).T
    for n in range(n_out, pitch):
        g_ref[pl.ds(n, tt, stride=pitch), :] = jnp.zeros((tt, LANES), F32)


def _to_scan(x3, col_block, value_indexed):
    nb, t, _ = x3.shape
    groups = LANES // (nb * H_A)
    pitch = _val_pitch(HD_A // groups) if value_indexed else KEY_PITCH
    out = pl.pallas_call(
        functools.partial(_to_scan_kernel, value_indexed=value_indexed, pitch=pitch),
        grid=(t // TIME_TILE,),
        in_specs=[pl.BlockSpec((nb, TIME_TILE, W_A), lambda i: (0, i, col_block))],
        out_specs=pl.BlockSpec((TIME_TILE * pitch, LANES), lambda i: (i, 0)),
        out_shape=jax.ShapeDtypeStruct((t * pitch, LANES), F32),
        scratch_shapes=[pltpu.VMEM((nb * H_A * KEY_PITCH, TIME_TILE), F32)],
        compiler_params=_params(("parallel",)),
        name="to_scan",
    )(x3)
    return out.reshape(t, pitch, LANES)


def _from_scan_kernel(y_ref, lnw_ref, lnb_ref, o_ref, c_scr, *, pitch):
    nb, tt, _ = o_ref.shape
    chains = nb * H_A
    groups = LANES // chains
    heads_per_tile = LANES // HD_A
    for n in range(HD_A // groups):
        lanes_t = y_ref[pl.ds(n, tt, stride=pitch), :].T
        for r in range(groups):
            c_scr[pl.ds(n * groups + r, chains, stride=KEY_PITCH), :] = (
                lanes_t[r * chains:(r + 1) * chains])
    for c in range(chains):
        rows = pl.ds(c * KEY_PITCH, HD_A)
        head = c % H_A
        y = c_scr[rows, :]
        mean = _sublane_allsum(y) * (1.0 / HD_A)
        d = y - jnp.concatenate([mean] * (HD_A // SUBLANES), axis=0)
        var = _sublane_allsum(d * d) * (1.0 / HD_A)
        inv = jnp.concatenate([lax.rsqrt(var + GN_EPS)] * (HD_A // SUBLANES), axis=0)
        c_scr[rows, :] = (d * inv * lnw_ref[head * HD_A:(head + 1) * HD_A, :]
                          + lnb_ref[head * HD_A:(head + 1) * HD_A, :])
    for b in range(nb):
        for hp in range(W_A // LANES):
            c0 = b * H_A + hp * heads_per_tile
            tile_t = jnp.concatenate(
                [c_scr[(c0 + h2) * KEY_PITCH:(c0 + h2) * KEY_PITCH + HD_A, :]
                 for h2 in range(heads_per_tile)], axis=0)
            o_ref[b, :, hp * LANES:(hp + 1) * LANES] = tile_t.T


def _from_scan(y, nb, ln_w, ln_b):
    t, pitch, _ = y.shape
    cols = lambda p: jnp.broadcast_to(p.reshape(W_A, 1), (W_A, TIME_TILE))
    return pl.pallas_call(
        functools.partial(_from_scan_kernel, pitch=pitch),
        grid=(t // TIME_TILE,),
        in_specs=[pl.BlockSpec((TIME_TILE * pitch, LANES), lambda i: (i, 0)),
                  _const_spec((W_A, TIME_TILE)), _const_spec((W_A, TIME_TILE))],
        out_specs=pl.BlockSpec((nb, TIME_TILE, W_A), lambda i: (0, i, 0)),
        out_shape=jax.ShapeDtypeStruct((nb, t, W_A), F32),
        scratch_shapes=[pltpu.VMEM((nb * H_A * KEY_PITCH, TIME_TILE), F32)],
        compiler_params=_params(("parallel",)),
        name="from_scan",
    )(y.reshape(t * pitch, LANES), cols(ln_w), cols(ln_b))


def _group_norm_kernel(y_ref, lnw_ref, lnb_ref, o_ref):
    y = y_ref[...]
    mean = _head_sum(y, HD_A) * (1.0 / HD_A)
    d = y - mean
    var = _head_sum(d * d, HD_A) * (1.0 / HD_A)
    o_ref[...] = d * lax.rsqrt(var + GN_EPS) * lnw_ref[...] + lnb_ref[...]


def _group_norm(y, ln_w, ln_b):
    m = y.shape[0]
    tm = min(256, m)
    row_spec = pl.BlockSpec((tm, W_A), lambda i: (i, 0))
    vec = _const_spec((1, W_A))
    return pl.pallas_call(
        _group_norm_kernel,
        grid=(m // tm,),
        in_specs=[row_spec, vec, vec],
        out_specs=row_spec,
        out_shape=jax.ShapeDtypeStruct((m, W_A), F32),
        compiler_params=_params(("parallel",)),
        name="group_norm",
    )(y, ln_w, ln_b)


def _lam_value(lam_ref):
    lv = lam_ref[...]
    s1 = jnp.sum(lv[0:1] * lv[1:2], axis=-1, keepdims=True)
    s2 = jnp.sum(lv[2:3] * lv[3:4], axis=-1, keepdims=True)
    return jnp.exp(s1) - jnp.exp(s2) + LAM_INIT


POS_SPLIT = 16
ONES_ROWS = 16


def _attn_prompt_kernel(q_ref, k_ref, v_ref, lam_ref, sg_ref, o_ref,
                        kaug_ref, vt_ref, m_ref, acc_ref, sa_ref, sb_ref):
    h = pl.program_id(1)
    qi = pl.program_id(2)
    tq = q_ref.shape[0]
    nblk = k_ref.shape[0] // tq
    width = 2 * HD_B
    nt = (((1,), (1,)), ((), ()))
    slope = jnp.exp2(jnp.full((1, width), -8.0 / H_B, F32) * (h + 1).astype(F32))

    lane = lax.broadcasted_iota(jnp.int32, (tq, width), 1)
    pos = lax.broadcasted_iota(jnp.int32, (tq, width), 0)
    pos_hi = (pos // POS_SPLIT).astype(F32)
    pos_lo = (pos % POS_SPLIT).astype(F32)
    one = jnp.ones((tq, width), F32)

    def features(base, vals):
        out = jnp.zeros((tq, width), F32)
        for i, val in enumerate(vals):
            out = jnp.where(lane == base + i, val, out)
        return out

    own = [(lane >= c * HD_B) & (lane < (c + 1) * HD_B) for c in range(2)]
    other = [(1 - c) * HD_B for c in range(2)]

    @pl.when(qi == 0)
    def _():
        for c in range(2):
            k_feat = features(other[c], [POS_SPLIT * pos_hi, pos_lo, -POS_SPLIT * one, -one])
            k_feat = k_feat * slope
            for j in range(nblk):
                kaug_ref[c, j] = jnp.where(own[c], k_ref[j * tq:(j + 1) * tq, :],
                                           k_feat).astype(BF16)
        for j in range(nblk):
            vt_ref[j] = jnp.concatenate(
                [v_ref[j * tq:(j + 1) * tq, :].T, jnp.ones((ONES_ROWS, tq), F32)],
                axis=0).astype(BF16)

    q = q_ref[...] * (HD_B ** -0.5)
    q_aug = [jnp.where(own[c], q, features(other[c], [one, one, pos_hi, pos_lo])).astype(BF16)
             for c in range(2)]

    m_ref[...] = jnp.full_like(m_ref, NEG)
    acc_ref[...] = jnp.zeros_like(acc_ref)

    def score(kj, s_ref):
        for c in range(2):
            s_ref[c] = lax.dot_general(kaug_ref[c, kj], q_aug[c], nt,
                                       preferred_element_type=F32)

    def col_max(s):
        chunk = s.shape[0] // SUBLANES
        part = s[0:chunk]
        for i in range(1, SUBLANES):
            part = jnp.maximum(part, s[i * chunk:(i + 1) * chunk])
        return jnp.max(part, axis=0, keepdims=True)

    def consume(kj, s_ref, masked):
        block_bias = -slope[:, 0:1] * (tq * (qi - kj)).astype(F32)
        v_t = vt_ref[kj]
        for c in range(2):
            s = s_ref[c]
            if masked:
                key_i = lax.broadcasted_iota(jnp.int32, (tq, tq), 0)
                qry_i = lax.broadcasted_iota(jnp.int32, (tq, tq), 1)
                s = jnp.where(key_i <= qry_i, s, NEG)
            m_old = m_ref[c]
            m_new = jnp.maximum(m_old, col_max(s) + block_bias)
            p = jnp.exp(s - (m_new - block_bias)).astype(BF16)
            acc_ref[c] = (jnp.exp(m_old - m_new) * acc_ref[c]
                          + jnp.dot(v_t, p, preferred_element_type=F32))
            m_ref[c] = m_new

    score(0, sa_ref)

    def pair(kk, carry):
        score(2 * kk + 1, sb_ref)
        consume(2 * kk, sa_ref, masked=False)
        score(2 * kk + 2, sa_ref)
        consume(2 * kk + 1, sb_ref, masked=False)
        return carry

    lax.fori_loop(0, qi // 2, pair, 0)

    @pl.when(qi % 2 == 1)
    def _():
        score(qi, sb_ref)
        consume(qi - 1, sa_ref, masked=False)
        consume(qi, sb_ref, masked=True)

    @pl.when(qi % 2 == 0)
    def _():
        consume(qi, sa_ref, masked=True)

    lam = _lam_value(lam_ref)
    a0 = acc_ref[0]
    a1 = acc_ref[1]
    att_t = a0[0:width] / a0[width:width + 1] - lam * (a1[0:width] / a1[width:width + 1])
    o_ref[...] = _rms(att_t.T, sg_ref[...], SUBLN_EPS) * (1.0 - LAM_INIT)


def _attn_prompt(q, k, v, seq, lam_rows, subln_g):
    m = q.shape[0]
    nb = m // seq
    tq = min(512, seq)
    assert tq % POS_SPLIT == 0 and tq // POS_SPLIT <= 256 and seq % tq == 0
    nq = seq // tq
    width = 2 * HD_B
    q_spec = pl.BlockSpec((tq, width), lambda b, h, qi: (b * nq + qi, h))
    kv_spec = pl.BlockSpec((seq, width), lambda b, h, qi: (b, h))
    return pl.pallas_call(
        _attn_prompt_kernel,
        grid=(nb, H_B, nq),
        in_specs=[q_spec, kv_spec, kv_spec,
                  pl.BlockSpec((4, HD_B), lambda *_: (0, 0)),
                  pl.BlockSpec((1, width), lambda *_: (0, 0))],
        out_specs=q_spec,
        out_shape=jax.ShapeDtypeStruct((m, W_B), F32),
        scratch_shapes=[pltpu.VMEM((2, nq, tq, width), BF16),
                        pltpu.VMEM((nq, width + ONES_ROWS, tq), BF16),
                        pltpu.VMEM((2, 1, tq), F32),
                        pltpu.VMEM((2, width + ONES_ROWS, tq), F32),
                        pltpu.VMEM((2, tq, tq), F32),
                        pltpu.VMEM((2, tq, tq), F32)],
        compiler_params=_params(("parallel", "parallel", "arbitrary")),
        name="attn_prompt",
    )(q, k, v, lam_rows, subln_g)


def _attn_sample_kernel(pt_ref, q_ref, kn_ref, vn_ref, lam_ref, sg_ref, *rest,
                        n_pages, past, n_seq):
    del pt_ref
    o_ref, s_ref = rest[2 * n_pages * n_seq], rest[2 * n_pages * n_seq + 1]
    for i in range(n_seq):
        k_pages = rest[i * n_pages:(i + 1) * n_pages]
        v_pages = rest[(n_seq + i) * n_pages:(n_seq + i + 1) * n_pages]
        _attn_sample_one(q_ref.at[i], kn_ref.at[i], vn_ref.at[i], lam_ref, sg_ref,
                         k_pages, v_pages, o_ref.at[i], s_ref.at[i], past)


def _attn_sample_one(q_ref, kn_ref, vn_ref, lam_ref, sg_ref, k_pages, v_pages, o_ref, s_ref,
                     past):
    n_pages = len(k_pages)
    tq = q_ref.shape[0]
    rows = 2 * H_B * tq
    nt = (((1,), (1,)), ((), ()))

    q = q_ref[...] * (HD_B ** -0.5)
    q_rep = jnp.concatenate([q] * (2 * H_B), axis=0)
    r_i = lax.broadcasted_iota(jnp.int32, (rows, W_B), 0)
    c_i = lax.broadcasted_iota(jnp.int32, (rows, W_B), 1)
    q_blk = jnp.where(r_i // tq == c_i // HD_B, q_rep, 0.0).astype(BF16)

    row = lax.broadcasted_iota(jnp.int32, (rows, PAGE_SIZE), 0)
    lane = lax.broadcasted_iota(jnp.int32, (rows, PAGE_SIZE), 1)
    head = row // (2 * tq)
    slope = jnp.exp2(-8.0 * (head + 1).astype(F32) / H_B)
    q_pos = past + row % tq

    def page(ref):
        return jnp.concatenate([ref[pl.ds(h, PAGE_SIZE, stride=H_B), :] for h in range(H_B)],
                               axis=-1).astype(BF16)

    for p in range(n_pages):
        s = lax.dot_general(q_blk, page(k_pages[p]), nt, preferred_element_type=F32)
        dist = q_pos - (p * PAGE_SIZE + lane)
        s_ref[:, p * PAGE_SIZE:(p + 1) * PAGE_SIZE] = s - slope * dist.astype(F32)

    kn = jnp.concatenate(
        [kn_ref[...], jnp.zeros((PAGE_SIZE - tq, W_B), F32)], axis=0).astype(BF16)
    s = lax.dot_general(q_blk, kn, nt, preferred_element_type=F32)
    dist = q_pos - (past + lane)
    s = jnp.where((dist >= 0) & (lane < tq), s - slope * dist.astype(F32), NEG)
    s_ref[:, past:past + PAGE_SIZE] = s

    s = s_ref[...]
    mx = jnp.max(s, axis=-1, keepdims=True)
    p_un = jnp.exp(s - mx)
    pn = p_un / jnp.sum(p_un, axis=-1, keepdims=True)
    lam = _lam_value(lam_ref)
    attn = jnp.concatenate(
        [pn[(2 * h) * tq:(2 * h + 1) * tq] - lam * pn[(2 * h + 1) * tq:(2 * h + 2) * tq]
         for h in range(H_B)], axis=0).astype(BF16)

    acc = jnp.zeros((H_B * tq, W_B), F32)
    for p in range(n_pages):
        acc = acc + jnp.dot(attn[:, p * PAGE_SIZE:(p + 1) * PAGE_SIZE],
                            page(v_pages[p]), preferred_element_type=F32)
    vn = jnp.concatenate(
        [vn_ref[...], jnp.zeros((PAGE_SIZE - tq, W_B), F32)], axis=0).astype(BF16)
    acc = acc + jnp.dot(attn[:, past:past + PAGE_SIZE], vn, preferred_element_type=F32)

    width = 2 * HD_B
    outs = []
    for h in range(H_B):
        att = acc[h * tq:(h + 1) * tq, h * width:(h + 1) * width]
        outs.append(_rms(att, sg_ref[...], SUBLN_EPS) * (1.0 - LAM_INIT))
    o_ref[...] = jnp.concatenate(outs, axis=-1)


def _attn_sample(q, k_new, v_new, cache_k, cache_v, page_table, lam_rows, subln_g):
    bd, tq, _ = q.shape
    n_pages = page_table.shape[1]
    past = n_pages * PAGE_SIZE
    n_seq = SAMPLE_SEQS_PER_STEP if bd % SAMPLE_SEQS_PER_STEP == 0 else 1
    tok_spec = pl.BlockSpec((n_seq, tq, W_B), lambda b, pt: (b, 0, 0))
    page_specs = [pl.BlockSpec((None, PAGE_SIZE * H_B, 2 * HD_B),
                               lambda b, pt, i=i, p=p: (pt[(b * n_seq + i) * n_pages + p], 0, 0))
                  for i in range(n_seq) for p in range(n_pages)]
    grid_spec = pltpu.PrefetchScalarGridSpec(
        num_scalar_prefetch=1,
        grid=(bd // n_seq,),
        in_specs=[tok_spec, tok_spec, tok_spec,
                  pl.BlockSpec((4, HD_B), lambda b, pt: (0, 0)),
                  pl.BlockSpec((1, 2 * HD_B), lambda b, pt: (0, 0))]
                 + page_specs + page_specs,
        out_specs=tok_spec,
        scratch_shapes=[pltpu.VMEM((n_seq, 2 * H_B * tq, past + PAGE_SIZE), F32)],
    )
    return pl.pallas_call(
        functools.partial(_attn_sample_kernel, n_pages=n_pages, past=past, n_seq=n_seq),
        grid_spec=grid_spec,
        out_shape=jax.ShapeDtypeStruct((bd, tq, W_B), F32),
        compiler_params=_params(("parallel",)),
        name="attn_sample",
    )(page_table.reshape(-1), q, k_new, v_new, lam_rows, subln_g,
      *([cache_k] * (n_pages * n_seq)), *([cache_v] * (n_pages * n_seq)))


def _merge_ffn_kernel(x_ref, yn_ref, bv_ref, g_ref, ob_ref, gate_ref, wa_ref, wb_ref, wo_ref,
                      gf_ref, wu_ref, wd_ref, gl_ref, y_ref):
    ga = gate_ref[:, 0:D_MODEL]
    gb = gate_ref[:, D_MODEL:GATE_COLS]
    o_a = (yn_ref[...] + bv_ref[...]) * g_ref[...]
    m = ga * _mm(o_a, wa_ref[...]) + gb * _mm(ob_ref[...], wb_ref[...])
    x1 = x_ref[...] + _mm(m, wo_ref[...])
    hn = _rms(x1, gf_ref[...], NORM_EPS).astype(BF16)
    acc = x1
    chunk = D_MODEL
    for c in range(D_FF // chunk):
        up = jnp.dot(hn, wu_ref[:, c * chunk:(c + 1) * chunk], preferred_element_type=F32)
        act = jnp.square(jnp.maximum(up, 0.0))
        acc = acc + _mm(act, wd_ref[c * chunk:(c + 1) * chunk, :])
    y_ref[...] = _rms(acc, gl_ref[...], NORM_EPS)


def _merge_ffn(x2d, yn, bv, g, o_b, gates, w_br_a, w_br_b, w_out, g_ffn, w_up, w_down,
               g_final):
    m = x2d.shape[0]
    tm = min(256, m)
    row_spec = lambda w: pl.BlockSpec((tm, w), lambda i: (i, 0))
    full = lambda a: _const_spec(a.shape)
    return pl.pallas_call(
        _merge_ffn_kernel,
        grid=(m // tm,),
        in_specs=[row_spec(D_MODEL), row_spec(W_A), row_spec(W_A), row_spec(W_A),
                  row_spec(W_B), row_spec(GATE_COLS),
                  full(w_br_a), full(w_br_b), full(w_out), full(g_ffn), full(w_up),
                  full(w_down), full(g_final)],
        out_specs=row_spec(D_MODEL),
        out_shape=jax.ShapeDtypeStruct((m, D_MODEL), F32),
        compiler_params=_params(("parallel",)),
        name="merge_ffn",
    )(x2d, yn, bv, g, o_b, gates, w_br_a, w_br_b, w_out, g_ffn, w_up, w_down, g_final)


def _rwkv_branch_prompt(pr, a_s, b_s, k_s, r_s, pc, nb, seq, tc, ln_w, ln_b):
    groups = LANES // (nb * H_A)
    ni = HD_A // groups
    nat = lambda x: x.reshape(nb, seq, x.shape[-1])
    key = lambda x: _to_scan(nat(x), 0, value_indexed=False)[None]
    v = _to_scan(nat(pr), 2, value_indexed=True)[None]
    s0 = jnp.zeros((1, ni, HD_A, LANES), F32)
    pc = pc.reshape(nb, seq // tc, H_A, HD_A).transpose(1, 3, 0, 2)
    pc = jnp.tile(pc.reshape(1, seq // tc, HD_A, nb * H_A), (1, 1, 1, groups))
    y, s = _rwkv_scan(key(a_s), key(b_s), key(k_s), key(r_s), v, pc, s0, tc=tc)
    y = _from_scan(y[0], nb, ln_w, ln_b).reshape(nb * seq, W_A)
    s = s.reshape(ni, HD_A, groups, nb, H_A).transpose(3, 4, 0, 2, 1)
    return y, s.reshape(nb, H_A, HD_A, HD_A)


def _rwkv_branch_sample(pr, a_s, b_s, k_s, r_s, pc, state, nb, seq, ln_w, ln_b):
    chains = nb * H_A
    ng = chains // LANES

    def lay(x):
        rows = x.shape[0] // nb
        x = x.reshape(nb, rows, H_A, HD_A).transpose(1, 3, 0, 2).reshape(rows, HD_A, ng, LANES)
        return x.transpose(2, 0, 1, 3)

    s0 = state.transpose(2, 3, 0, 1).reshape(HD_A, HD_A, ng, LANES).transpose(2, 0, 1, 3)
    y, s = _rwkv_scan(lay(a_s), lay(b_s), lay(k_s), lay(r_s), lay(pr[:, 2 * W_A:3 * W_A]),
                      lay(pc), s0, tc=seq)
    y = y.transpose(1, 2, 0, 3).reshape(seq, HD_A, nb, H_A).transpose(2, 0, 3, 1)
    s = s.transpose(1, 2, 0, 3).reshape(HD_A, HD_A, nb, H_A).transpose(2, 3, 0, 1)
    return _group_norm(y.reshape(nb * seq, W_A), ln_w, ln_b), s


def kernel(x_prompt, x_sample, cache_k, cache_v, state_rwkv, state_shift, page_table,
           w_in, mu_shift, w0, w2, a0, a2, g2, k_k, k_a, r_k, ln_x_w, ln_x_b,
           lam_q1, lam_k1, lam_q2, lam_k2, subln_g, w_br_a, w_br_b, w_out,
           g_mix, g_ffn, w_up, w_down, g_final):
    nb, seq, _ = x_prompt.shape
    bd, dseq, _ = x_sample.shape
    w_in_b = w_in[0].astype(BF16)
    wa_b, wb_b, wo_b = (w[0].astype(BF16) for w in (w_br_a, w_br_b, w_out))
    wu_b, wd_b = w_up[0].astype(BF16), w_down[0].astype(BF16)
    r_k2 = r_k.reshape(1, W_A)
    lam_rows = jnp.concatenate([lam_q1, lam_k1, lam_q2, lam_k2], axis=0)
    g_fin = g_final.reshape(1, D_MODEL)

    def rwkv_ops(pr, chunk):
        return _rwkv_pre(pr, w0, w2[0], a0, a2[0], g2[0], k_k, k_a, r_k2, chunk)

    def finish(x2d, yn, bv, g, o_b, gates):
        return _merge_ffn(x2d, yn, bv, g, o_b, gates, wa_b, wb_b, wo_b, g_ffn, wu_b, wd_b,
                          g_fin)

    xp = x_prompt.reshape(nb * seq, D_MODEL)
    pr, q, k, v, kt, vt, gates, x_last = _inproj_prompt(xp, seq, g_mix, w_in_b, mu_shift)
    tc = min(SCAN_CHUNK, seq)
    a_s, b_s, k_s, r_s, pc, g, bv = rwkv_ops(pr, tc)
    yn, s_prompt = _rwkv_branch_prompt(pr, a_s, b_s, k_s, r_s, pc, nb, seq, tc,
                                       ln_x_w, ln_x_b)
    o_b = _attn_prompt(q, k, v, seq, lam_rows, subln_g)
    y_prompt = finish(xp, yn, bv, g, o_b, gates).reshape(nb, seq, D_MODEL)
    k_prompt = kt.reshape(1, nb, seq, H_B, 2 * HD_B)
    v_prompt = vt.reshape(1, nb, seq, H_B, 2 * HD_B)
    shift_prompt = x_last.reshape(1, nb, D_MODEL)

    xs = x_sample.reshape(bd * dseq, D_MODEL)
    shift_pad = jnp.zeros((bd, dseq, D_MODEL), F32).at[:, 0].set(state_shift[0])
    pr, q, k, v, kt, vt, gates, xn = _inproj_sample(
        xs, dseq, shift_pad.reshape(bd * dseq, D_MODEL), g_mix, w_in_b, mu_shift)
    a_s, b_s, k_s, r_s, pc, g, bv = rwkv_ops(pr, dseq)
    yn, s_sample = _rwkv_branch_sample(pr, a_s, b_s, k_s, r_s, pc, state_rwkv[0], bd, dseq,
                                       ln_x_w, ln_x_b)
    tok = lambda t: t.reshape(bd, dseq, W_B)
    pages = lambda c: c[0].reshape(c.shape[1], PAGE_SIZE * H_B, 2 * HD_B)
    o_b = _attn_sample(tok(q), tok(k), tok(v), pages(cache_k), pages(cache_v),
                       page_table, lam_rows, subln_g)
    y_sample = finish(xs, yn, bv, g, o_b.reshape(bd * dseq, W_B), gates)
    y_sample = y_sample.reshape(bd, dseq, D_MODEL)
    k_sample = kt.reshape(1, bd, dseq, H_B, 2 * HD_B)
    v_sample = vt.reshape(1, bd, dseq, H_B, 2 * HD_B)
    shift_sample = xn.reshape(bd, dseq, D_MODEL)[:, -1][None]

    return (y_prompt, y_sample, k_prompt, v_prompt, k_sample, v_sample,
            s_prompt[None], s_sample[None], shift_prompt, shift_sample)
```

```python
import functools

import jax
import jax.numpy as jnp
from jax import lax
from jax.experimental import pallas as pl
from jax.experimental.pallas import tpu as pltpu

F32 = jnp.float32
BF16 = jnp.bfloat16

D_MODEL = 1024
H_A = 8
HD_A = 64
W_A = H_A * HD_A
DECAY_LORA = 64
AAA_LORA = 64
GATE_LORA = 128
RWKV_COLS = 3 * W_A + DECAY_LORA + AAA_LORA + GATE_LORA
H_B = 4
HD_B = 64
W_B = H_B * 2 * HD_B
GATE_COLS = 2 * D_MODEL
N_COLS = RWKV_COLS + 3 * W_B + GATE_COLS
D_FF = 4 * D_MODEL
PAGE_SIZE = 128
NORM_EPS = 1e-6
GN_EPS = 64e-5
SUBLN_EPS = 1e-5
NEG = -1e30
LAM_INIT = 0.8 - 0.6 * 1.0

LANES = 128
SUBLANES = 8
MXU_TILE = 256
VMEM_LIMIT = 56 * 1024 * 1024
SCAN_CHUNK = 32
SAMPLE_SEQS_PER_STEP = 2

OFF_Q = RWKV_COLS
OFF_K = OFF_Q + W_B
OFF_V = OFF_K + W_B
OFF_GA = OFF_V + W_B


def _params(sem):
    return pltpu.CompilerParams(dimension_semantics=sem, vmem_limit_bytes=VMEM_LIMIT)


def _const_spec(shape):
    nd = len(shape)
    return pl.BlockSpec(shape, lambda *_: (0,) * nd, pipeline_mode=pl.Buffered(1))


def _rms(x, g, eps):
    return x * lax.rsqrt(jnp.mean(x * x, axis=-1, keepdims=True) + eps) * g


def _mm(a, w):
    return jnp.dot(a.astype(BF16), w, preferred_element_type=F32)


def _split2(x):
    hi = x.astype(BF16)
    lo = (x - hi.astype(F32)).astype(BF16)
    return hi, lo


def _head_sum(x, head):
    width = x.shape[1]
    tile = min(MXU_TILE, width)
    r = lax.broadcasted_iota(jnp.int32, (tile, tile), 0) // head
    c = lax.broadcasted_iota(jnp.int32, (tile, tile), 1) // head
    ones = (r == c).astype(BF16)
    hi, lo = _split2(x)
    parts = []
    for s in range(width // tile):
        cols = slice(s * tile, (s + 1) * tile)
        parts.append(jnp.dot(hi[:, cols], ones, preferred_element_type=F32)
                     + jnp.dot(lo[:, cols], ones, preferred_element_type=F32))
    return jnp.concatenate(parts, axis=-1)


def _proj_tail(xn, w_ref, q_ref, k_ref, v_ref, kt_ref, vt_ref, gate_ref):
    xb = xn.astype(BF16)
    tm = xn.shape[0]
    q_ref[...] = jnp.dot(xb, w_ref[:, OFF_Q:OFF_K], preferred_element_type=F32)
    k = jnp.dot(xb, w_ref[:, OFF_K:OFF_V], preferred_element_type=F32)
    v = jnp.dot(xb, w_ref[:, OFF_V:OFF_GA], preferred_element_type=F32)
    k_ref[...] = k
    v_ref[...] = v
    for h in range(H_B):
        cols = slice(h * 2 * HD_B, (h + 1) * 2 * HD_B)
        kt_ref[pl.ds(h, tm, stride=H_B), :] = k[:, cols]
        vt_ref[pl.ds(h, tm, stride=H_B), :] = v[:, cols]
    gate_ref[...] = jax.nn.sigmoid(
        jnp.dot(xb, w_ref[:, OFF_GA:N_COLS], preferred_element_type=F32))
    return xb


def _inproj_prompt_kernel(x_ref, g_ref, w_ref, mu_ref,
                          pr_ref, q_ref, k_ref, v_ref, kt_ref, vt_ref, gate_ref, xl_ref,
                          carry_ref):
    tm = x_ref.shape[0]
    xn = _rms(x_ref[...], g_ref[...], NORM_EPS)
    xl_ref[0] = xn[tm - 1:tm, :]
    xb = _proj_tail(xn, w_ref, q_ref, k_ref, v_ref, kt_ref, vt_ref, gate_ref)
    pr = jnp.dot(xb, w_ref[:, 0:RWKV_COLS], preferred_element_type=F32)

    @pl.when(pl.program_id(1) == 0)
    def _():
        carry_ref[...] = jnp.zeros_like(carry_ref)

    prev = pltpu.roll(pr, 1, 0)
    row = lax.broadcasted_iota(jnp.int32, pr.shape, 0)
    prev = jnp.where(row == 0, carry_ref[SUBLANES - 1:SUBLANES, :], prev)
    carry_ref[...] = pr[tm - SUBLANES:tm, :]
    pr_ref[...] = pr + (prev - pr) * mu_ref[...]


def _inproj_prompt(x2d, seq, g_mix, w_in, mu):
    m = x2d.shape[0]
    nb = m // seq
    tm = min(512, seq)
    nt = seq // tm
    row_spec = lambda w: pl.BlockSpec((tm, w), lambda b, t: (b * nt + t, 0))
    head_spec = pl.BlockSpec((tm * H_B, 2 * HD_B), lambda b, t: (b * nt + t, 0))
    return pl.pallas_call(
        _inproj_prompt_kernel,
        grid=(nb, nt),
        in_specs=[row_spec(D_MODEL), _const_spec((1, D_MODEL)),
                  _const_spec((D_MODEL, N_COLS)), _const_spec((1, RWKV_COLS))],
        out_specs=[row_spec(RWKV_COLS), row_spec(W_B), row_spec(W_B), row_spec(W_B),
                   head_spec, head_spec, row_spec(GATE_COLS),
                   pl.BlockSpec((1, 1, D_MODEL), lambda b, t: (b, 0, 0))],
        out_shape=[jax.ShapeDtypeStruct((m, RWKV_COLS), F32),
                   jax.ShapeDtypeStruct((m, W_B), F32),
                   jax.ShapeDtypeStruct((m, W_B), F32),
                   jax.ShapeDtypeStruct((m, W_B), F32),
                   jax.ShapeDtypeStruct((m * H_B, 2 * HD_B), F32),
                   jax.ShapeDtypeStruct((m * H_B, 2 * HD_B), F32),
                   jax.ShapeDtypeStruct((m, GATE_COLS), F32),
                   jax.ShapeDtypeStruct((nb, 1, D_MODEL), F32)],
        scratch_shapes=[pltpu.VMEM((SUBLANES, RWKV_COLS), F32)],
        compiler_params=_params(("parallel", "arbitrary")),
        name="inproj_prompt",
    )(x2d, g_mix, w_in, mu)


def _inproj_sample_kernel(x_ref, sp_ref, g_ref, w_ref, mu_ref,
                          pr_ref, q_ref, k_ref, v_ref, kt_ref, vt_ref, gate_ref, xn_ref, *,
                          seq):
    xn = _rms(x_ref[...], g_ref[...], NORM_EPS)
    xn_ref[...] = xn
    xb = _proj_tail(xn, w_ref, q_ref, k_ref, v_ref, kt_ref, vt_ref, gate_ref)
    row = lax.broadcasted_iota(jnp.int32, xn.shape, 0)
    xprev = jnp.where(row % seq == 0, sp_ref[...], pltpu.roll(xn, 1, 0))
    w_r = w_ref[:, 0:RWKV_COLS]
    pr = jnp.dot(xb, w_r, preferred_element_type=F32)
    prev = jnp.dot(xprev.astype(BF16), w_r, preferred_element_type=F32)
    pr_ref[...] = pr + (prev - pr) * mu_ref[...]


def _inproj_sample(x2d, seq, shift_pad, g_mix, w_in, mu):
    m = x2d.shape[0]
    tm = min(256, m)
    assert tm % seq == 0 and m % tm == 0
    row_spec = lambda w: pl.BlockSpec((tm, w), lambda i: (i, 0))
    head_spec = pl.BlockSpec((tm * H_B, 2 * HD_B), lambda i: (i, 0))
    return pl.pallas_call(
        functools.partial(_inproj_sample_kernel, seq=seq),
        grid=(m // tm,),
        in_specs=[row_spec(D_MODEL), row_spec(D_MODEL), _const_spec((1, D_MODEL)),
                  _const_spec((D_MODEL, N_COLS)), _const_spec((1, RWKV_COLS))],
        out_specs=[row_spec(RWKV_COLS), row_spec(W_B), row_spec(W_B), row_spec(W_B),
                   head_spec, head_spec, row_spec(GATE_COLS), row_spec(D_MODEL)],
        out_shape=[jax.ShapeDtypeStruct((m, RWKV_COLS), F32),
                   jax.ShapeDtypeStruct((m, W_B), F32),
                   jax.ShapeDtypeStruct((m, W_B), F32),
                   jax.ShapeDtypeStruct((m, W_B), F32),
                   jax.ShapeDtypeStruct((m * H_B, 2 * HD_B), F32),
                   jax.ShapeDtypeStruct((m * H_B, 2 * HD_B), F32),
                   jax.ShapeDtypeStruct((m, GATE_COLS), F32),
                   jax.ShapeDtypeStruct((m, D_MODEL), F32)],
        compiler_params=_params(("parallel",)),
        name="inproj_sample",
    )(x2d, shift_pad, g_mix, w_in, mu)


def _dot_f32(a, b):
    a_hi, a_lo = _split2(a)
    b_hi, b_lo = _split2(b)
    d = lambda x, y: jnp.dot(x, y, preferred_element_type=F32)
    return d(a_hi, b_hi) + d(a_lo, b_hi) + d(a_hi, b_lo)


def _pre_math(pr, params, p_scr, chunk):
    w0, w2, a0, a2, g2, k_k, k_a, r_k = params
    tm = pr.shape[0]
    o_wd = 3 * W_A
    o_ad = o_wd + DECAY_LORA
    o_gd = o_ad + AAA_LORA
    k = pr[:, W_A:2 * W_A]
    wd = pr[:, o_wd:o_ad]
    ad = pr[:, o_ad:o_gd]
    gd = pr[:, o_gd:RWKV_COLS]
    z = -(w0 + _dot_f32(jnp.tanh(wd), w2))
    softplus = jnp.maximum(z, 0.0) + jnp.log1p(jnp.exp(-jnp.abs(z)))
    w = -softplus - 0.5
    log_decay = -jnp.exp(w)
    r_i = lax.broadcasted_iota(jnp.int32, (tm, tm), 0)
    c_i = lax.broadcasted_iota(jnp.int32, (tm, tm), 1)
    tri = ((r_i // chunk == c_i // chunk) & (c_i <= r_i)).astype(BF16)
    hi = log_decay.astype(BF16)
    rest = log_decay - hi.astype(F32)
    mid = rest.astype(BF16)
    lo = (rest - mid.astype(F32)).astype(BF16)
    run = lambda t: jnp.dot(tri, t, preferred_element_type=F32)
    cum = run(hi) + run(mid) + run(lo)
    p_incl = jnp.exp(cum)
    p_inv = jnp.exp(-cum)
    p_prev = jnp.exp(cum - log_decay)
    pc = []
    for s in range(W_A // LANES):
        p_scr[s] = p_incl[:, s * LANES:(s + 1) * LANES]
        pc.append(p_scr[s, pl.ds(chunk - 1, tm // chunk, stride=chunk), :])
    pc = jnp.concatenate(pc, axis=-1)
    a = jax.nn.sigmoid(a0 + _dot_f32(ad, a2))
    g = _dot_f32(jax.nn.sigmoid(gd), g2)
    kk = k * k_k
    norm = jnp.sqrt(_head_sum(kk * kk, HD_A))
    kk = kk / jnp.maximum(norm, 1e-12)
    k_mod = k * (1.0 + (a - 1.0) * k_a)
    r = pr[:, 0:W_A]
    v = pr[:, 2 * W_A:3 * W_A]
    bv = _head_sum(r * k_mod * r_k, HD_A) * v
    return -kk * p_prev, kk * a * p_inv, k_mod * p_inv, r * p_incl, pc, g, bv


def _rwkv_pre_kernel(pr_ref, *refs, chunk):
    params = tuple(ref[...] for ref in refs[:8])
    outs, p_scr = refs[8:15], refs[15]
    for out_ref, val in zip(outs, _pre_math(pr_ref[...], params, p_scr, chunk)):
        out_ref[...] = val


def _rwkv_pre(pr, w0, w2, a0, a2, g2, k_k, k_a, r_k, chunk):
    m = pr.shape[0]
    tm = min(256, m)
    assert tm % chunk == 0 and (tm // chunk) % SUBLANES == 0
    row_spec = lambda w: pl.BlockSpec((tm, w), lambda i: (i, 0))
    full = lambda a: _const_spec(a.shape)
    out = jax.ShapeDtypeStruct((m, W_A), F32)
    return pl.pallas_call(
        functools.partial(_rwkv_pre_kernel, chunk=chunk),
        grid=(m // tm,),
        in_specs=[row_spec(RWKV_COLS), full(w0), full(w2), full(a0), full(a2), full(g2),
                  full(k_k), full(k_a), full(r_k)],
        out_specs=[row_spec(W_A)] * 4 + [pl.BlockSpec((tm // chunk, W_A), lambda i: (i, 0))]
                  + [row_spec(W_A)] * 2,
        out_shape=[out] * 4 + [jax.ShapeDtypeStruct((m // chunk, W_A), F32)] + [out] * 2,
        scratch_shapes=[pltpu.VMEM((W_A // LANES, tm, LANES), F32)],
        compiler_params=_params(("parallel",)),
        name="rwkv_pre",
    )(pr, w0, w2, a0, a2, g2, k_k, k_a, r_k)


def _sublane_allsum(x):
    n = x.shape[0] // SUBLANES
    acc = x[0:SUBLANES]
    for g in range(1, n):
        acc = acc + x[g * SUBLANES:(g + 1) * SUBLANES]
    shift = SUBLANES // 2
    while shift >= 1:
        acc = acc + pltpu.roll(acc, shift, 0)
        shift //= 2
    return acc


def _rwkv_scan_kernel(a_ref, b_ref, k_ref, r_ref, v_ref, pc_ref, s0_ref, y_ref, s_ref):
    tc = a_ref.shape[0]
    ni, nkey = s0_ref.shape[0], s0_ref.shape[1]
    ng = nkey // SUBLANES
    pad_rows = y_ref.shape[1] - ni

    @pl.when(pl.program_id(1) == 0)
    def _():
        s_ref[...] = s0_ref[...]

    sub = lax.broadcasted_iota(jnp.int32, (SUBLANES, LANES), 0)
    keys = pl.ds(0, nkey)

    def tile8(x):
        return jnp.concatenate([x] * ng, axis=0)

    def step(t, carry):
        def group(ig, carry2):
            y_acc = jnp.zeros((SUBLANES, LANES), F32)
            for ii in range(SUBLANES):
                i = ig * SUBLANES + ii
                s_i = s_ref[i]
                sa = _sublane_allsum(s_i * a_ref[t, keys])
                v_i = jnp.broadcast_to(v_ref[t, pl.ds(i, 1), :], (SUBLANES, LANES))
                s_new = s_i + tile8(sa) * b_ref[t, keys] + tile8(v_i) * k_ref[t, keys]
                s_ref[i] = s_new
                y_i = _sublane_allsum(s_new * r_ref[t, keys])
                y_acc = jnp.where(sub == ii, y_i, y_acc)
            y_ref[t, pl.ds(pl.multiple_of(ig * SUBLANES, SUBLANES), SUBLANES), :] = y_acc
            return carry2

        n_groups = ni // SUBLANES
        if n_groups <= 2:
            for ig in range(n_groups):
                group(ig, 0)
        else:
            lax.fori_loop(0, n_groups, group, 0)
        if pad_rows:
            y_ref[t, pl.ds(ni, pad_rows), :] = jnp.zeros((pad_rows, LANES), F32)
        return carry

    lax.fori_loop(0, tc, step, 0, unroll=2)

    def rescale(i, carry):
        s_ref[i] = s_ref[i] * pc_ref[0, keys]
        return carry

    lax.fori_loop(0, ni, rescale, 0, unroll=8)


def _rwkv_scan(a, b, k, r, v, pc, s0, tc):
    g, t, pk, _ = a.shape
    pv = v.shape[2]
    _, ni, nkey, _ = s0.shape
    x_spec = pl.BlockSpec((None, tc, pk, LANES), lambda gi, ti: (gi, ti, 0, 0))
    v_spec = pl.BlockSpec((None, tc, pv, LANES), lambda gi, ti: (gi, ti, 0, 0))
    s_spec = pl.BlockSpec((None, ni, nkey, LANES), lambda gi, ti: (gi, 0, 0, 0))
    pc_spec = pl.BlockSpec((None, 1, pc.shape[2], LANES), lambda gi, ti: (gi, ti, 0, 0))
    return pl.pallas_call(
        _rwkv_scan_kernel,
        grid=(g, t // tc),
        in_specs=[x_spec] * 4 + [v_spec, pc_spec, s_spec],
        out_specs=[v_spec, s_spec],
        out_shape=[jax.ShapeDtypeStruct((g, t, pv, LANES), F32),
                   jax.ShapeDtypeStruct((g, ni, nkey, LANES), F32)],
        compiler_params=_params(("parallel", "arbitrary")),
        name="rwkv_scan",
    )(a, b, k, r, v, pc, s0)


KEY_PITCH = HD_A + SUBLANES
TIME_TILE = LANES


def _val_pitch(n_rows):
    p = n_rows + SUBLANES
    return p if (p // SUBLANES) % 2 else p + SUBLANES


def _scan_round1(tile, nb, c_scr):
    heads_per_tile = LANES // HD_A
    for b in range(nb):
        for hp in range(W_A // LANES):
            tile_t = tile(b, hp).T
            for h2 in range(heads_per_tile):
                c = b * H_A + hp * heads_per_tile + h2
                c_scr[c * KEY_PITCH:c * KEY_PITCH + HD_A, :] = tile_t[h2 * HD_A:(h2 + 1) * HD_A]


def _scan_round2(c_scr, nb, g_ref, value_indexed, pitch):
    tt = c_scr.shape[1]
    chains = nb * H_A
    groups = LANES // chains
    chain_rows = lambda j: c_scr[pl.ds(j, chains, stride=KEY_PITCH), :]
    n_out = HD_A // groups if value_indexed else HD_A
    for n in range(n_out):
        if value_indexed:
            rows = [chain_rows(n * groups + r) for r in range(groups)]
        else:
            rows = [chain_rows(n)] * groups
        g_ref[pl.ds(n, tt, stride=pitch), :] = jnp.concatenate(rows, axis=0).T
    for n in range(n_out, pitch):
        g_ref[pl.ds(n, tt, stride=pitch), :] = jnp.zeros((tt, LANES), F32)


def _pre_scan_kernel(pr_ref, *refs, chunk, part, n_tiles):
    nb, tt, _ = pr_ref.shape
    params = tuple(ref[...] for ref in refs[:8])
    n_ops = 2 if part == 0 else 3
    outs = refs[8:-(2 * n_ops + 1)]
    sets = (refs[-(2 * n_ops + 1):-(n_ops + 1)], refs[-(n_ops + 1):-1])
    p_scr = refs[-1]
    i = pl.program_id(0)
    vp = outs[2].shape[0] // tt if part == 1 else None
    lanes = lambda hp: slice(hp * LANES, (hp + 1) * LANES)

    @pl.when(i == 0)
    def _():
        for c_scr in sets[1]:
            c_scr[...] = jnp.zeros_like(c_scr)

    def body(done, fresh):
        for idx in range(2):
            _scan_round2(done[idx], nb, outs[idx], False, KEY_PITCH)
        if part == 1:
            _scan_round2(done[2], nb, outs[2], True, vp)
        vals = [_pre_math(pr_ref[b], params, p_scr, chunk) for b in range(nb)]
        first = 0 if part == 0 else 2
        for idx in range(2):
            _scan_round1(lambda b, hp: vals[b][first + idx][:, lanes(hp)], nb, fresh[idx])
        if part == 1:
            v_cols = lambda hp: slice(2 * W_A + hp * LANES, 2 * W_A + (hp + 1) * LANES)
            _scan_round1(lambda b, hp: pr_ref[b, :, v_cols(hp)], nb, fresh[2])
            pc_ref, g_ref, bv_ref = outs[3:6]
            for b in range(nb):
                pc_ref[b] = vals[b][4]
                g_ref[b] = vals[b][5]
                bv_ref[b] = vals[b][6]

    @pl.when(i % 2 == 0)
    def _():
        body(sets[1], sets[0])

    @pl.when(i % 2 == 1)
    def _():
        body(sets[0], sets[1])


def _pre_scan(pr3, params, chunk, part):
    nb, t, _ = pr3.shape
    groups = LANES // (nb * H_A)
    vp = _val_pitch(HD_A // groups)
    n_tiles = t // TIME_TILE
    cpt = TIME_TILE // chunk
    this = lambda i: jnp.minimum(i, n_tiles - 1)
    prev = lambda i: jnp.maximum(i - 1, 0)
    g_spec = lambda pitch: pl.BlockSpec((TIME_TILE * pitch, LANES), lambda i: (prev(i), 0))
    g_shape = lambda pitch: jax.ShapeDtypeStruct((t * pitch, LANES), F32)
    nat_spec = pl.BlockSpec((nb, TIME_TILE, W_A), lambda i: (0, this(i), 0))
    nat_shape = jax.ShapeDtypeStruct((nb, t, W_A), F32)
    if part == 0:
        out_specs = [g_spec(KEY_PITCH)] * 2
        out_shape = [g_shape(KEY_PITCH)] * 2
    else:
        out_specs = [g_spec(KEY_PITCH)] * 2 + [
            g_spec(vp), pl.BlockSpec((None, nb, cpt, W_A), lambda i: (this(i), 0, 0, 0)),
            nat_spec, nat_spec]
        out_shape = [g_shape(KEY_PITCH)] * 2 + [
            g_shape(vp), jax.ShapeDtypeStruct((n_tiles, nb, cpt, W_A), F32),
            nat_shape, nat_shape]
    n_ops = 2 if part == 0 else 3
    return pl.pallas_call(
        functools.partial(_pre_scan_kernel, chunk=chunk, part=part, n_tiles=n_tiles),
        grid=(n_tiles + 1,),
        in_specs=[pl.BlockSpec((nb, TIME_TILE, RWKV_COLS), lambda i: (0, this(i), 0))]
                 + [_const_spec(p.shape) for p in params],
        out_specs=out_specs,
        out_shape=out_shape,
        scratch_shapes=[pltpu.VMEM((nb * H_A * KEY_PITCH, TIME_TILE), F32)] * (2 * n_ops)
                       + [pltpu.VMEM((W_A // LANES, TIME_TILE, LANES), F32)],
        compiler_params=_params(("arbitrary",)),
        name="pre_scan",
    )(pr3, *params)


def _from_scan_kernel(y_ref, lnw_ref, lnb_ref, o_ref, c_scr, *, pitch):
    nb, tt, _ = o_ref.shape
    chains = nb * H_A
    groups = LANES // chains
    heads_per_tile = LANES // HD_A
    for n in range(HD_A // groups):
        lanes_t = y_ref[pl.ds(n, tt, stride=pitch), :].T
        for r in range(groups):
            c_scr[pl.ds(n * groups + r, chains, stride=KEY_PITCH), :] = (
                lanes_t[r * chains:(r + 1) * chains])
    for c in range(chains):
        rows = pl.ds(c * KEY_PITCH, HD_A)
        head = c % H_A
        y = c_scr[rows, :]
        mean = _sublane_allsum(y) * (1.0 / HD_A)
        d = y - jnp.concatenate([mean] * (HD_A // SUBLANES), axis=0)
        var = _sublane_allsum(d * d) * (1.0 / HD_A)
        inv = jnp.concatenate([lax.rsqrt(var + GN_EPS)] * (HD_A // SUBLANES), axis=0)
        c_scr[rows, :] = (d * inv * lnw_ref[head * HD_A:(head + 1) * HD_A, :]
                          + lnb_ref[head * HD_A:(head + 1) * HD_A, :])
    for b in range(nb):
        for hp in range(W_A // LANES):
            c0 = b * H_A + hp * heads_per_tile
            tile_t = jnp.concatenate(
                [c_scr[(c0 + h2) * KEY_PITCH:(c0 + h2) * KEY_PITCH + HD_A, :]
                 for h2 in range(heads_per_tile)], axis=0)
            o_ref[b, :, hp * LANES:(hp + 1) * LANES] = tile_t.T


def _from_scan(y, nb, ln_w, ln_b):
    t, pitch, _ = y.shape
    cols = lambda p: jnp.broadcast_to(p.reshape(W_A, 1), (W_A, TIME_TILE))
    return pl.pallas_call(
        functools.partial(_from_scan_kernel, pitch=pitch),
        grid=(t // TIME_TILE,),
        in_specs=[pl.BlockSpec((TIME_TILE * pitch, LANES), lambda i: (i, 0)),
                  _const_spec((W_A, TIME_TILE)), _const_spec((W_A, TIME_TILE))],
        out_specs=pl.BlockSpec((nb, TIME_TILE, W_A), lambda i: (0, i, 0)),
        out_shape=jax.ShapeDtypeStruct((nb, t, W_A), F32),
        scratch_shapes=[pltpu.VMEM((nb * H_A * KEY_PITCH, TIME_TILE), F32)],
        compiler_params=_params(("parallel",)),
        name="from_scan",
    )(y.reshape(t * pitch, LANES), cols(ln_w), cols(ln_b))


def _group_norm_kernel(y_ref, lnw_ref, lnb_ref, o_ref):
    y = y_ref[...]
    mean = _head_sum(y, HD_A) * (1.0 / HD_A)
    d = y - mean
    var = _head_sum(d * d, HD_A) * (1.0 / HD_A)
    o_ref[...] = d * lax.rsqrt(var + GN_EPS) * lnw_ref[...] + lnb_ref[...]


def _group_norm(y, ln_w, ln_b):
    m = y.shape[0]
    tm = min(256, m)
    row_spec = pl.BlockSpec((tm, W_A), lambda i: (i, 0))
    vec = _const_spec((1, W_A))
    return pl.pallas_call(
        _group_norm_kernel,
        grid=(m // tm,),
        in_specs=[row_spec, vec, vec],
        out_specs=row_spec,
        out_shape=jax.ShapeDtypeStruct((m, W_A), F32),
        compiler_params=_params(("parallel",)),
        name="group_norm",
    )(y, ln_w, ln_b)


def _lam_value(lam_ref):
    lv = lam_ref[...]
    s1 = jnp.sum(lv[0:1] * lv[1:2], axis=-1, keepdims=True)
    s2 = jnp.sum(lv[2:3] * lv[3:4], axis=-1, keepdims=True)
    return jnp.exp(s1) - jnp.exp(s2) + LAM_INIT


POS_SPLIT = 16
ONES_ROWS = 16


def _attn_prompt_kernel(q_ref, k_ref, v_ref, lam_ref, sg_ref, o_ref,
                        kaug_ref, vt_ref, m_ref, acc_ref, sa_ref, sb_ref):
    h = pl.program_id(1)
    qi = pl.program_id(2)
    tq = q_ref.shape[0]
    nblk = k_ref.shape[0] // tq
    width = 2 * HD_B
    nt = (((1,), (1,)), ((), ()))
    slope = jnp.exp2(jnp.full((1, width), -8.0 / H_B, F32) * (h + 1).astype(F32))

    lane = lax.broadcasted_iota(jnp.int32, (tq, width), 1)
    pos = lax.broadcasted_iota(jnp.int32, (tq, width), 0)
    pos_hi = (pos // POS_SPLIT).astype(F32)
    pos_lo = (pos % POS_SPLIT).astype(F32)
    one = jnp.ones((tq, width), F32)

    def features(base, vals):
        out = jnp.zeros((tq, width), F32)
        for i, val in enumerate(vals):
            out = jnp.where(lane == base + i, val, out)
        return out

    own = [(lane >= c * HD_B) & (lane < (c + 1) * HD_B) for c in range(2)]
    other = [(1 - c) * HD_B for c in range(2)]

    @pl.when(qi == 0)
    def _():
        for c in range(2):
            k_feat = features(other[c], [POS_SPLIT * pos_hi, pos_lo, -POS_SPLIT * one, -one])
            k_feat = k_feat * slope
            for j in range(nblk):
                kaug_ref[c, j] = jnp.where(own[c], k_ref[j * tq:(j + 1) * tq, :],
                                           k_feat).astype(BF16)
        for j in range(nblk):
            vt_ref[j] = jnp.concatenate(
                [v_ref[j * tq:(j + 1) * tq, :].T, jnp.ones((ONES_ROWS, tq), F32)],
                axis=0).astype(BF16)

    q = q_ref[...] * (HD_B ** -0.5)
    q_aug = [jnp.where(own[c], q, features(other[c], [one, one, pos_hi, pos_lo])).astype(BF16)
             for c in range(2)]

    m_ref[...] = jnp.full_like(m_ref, NEG)
    acc_ref[...] = jnp.zeros_like(acc_ref)

    def score(kj, s_ref):
        for c in range(2):
            s_ref[c] = lax.dot_general(kaug_ref[c, kj], q_aug[c], nt,
                                       preferred_element_type=F32)

    def col_max(s):
        chunk = s.shape[0] // SUBLANES
        part = s[0:chunk]
        for i in range(1, SUBLANES):
            part = jnp.maximum(part, s[i * chunk:(i + 1) * chunk])
        return jnp.max(part, axis=0, keepdims=True)

    def consume(kj, s_ref, masked):
        block_bias = -slope[:, 0:1] * (tq * (qi - kj)).astype(F32)
        v_t = vt_ref[kj]
        for c in range(2):
            s = s_ref[c]
            if masked:
                key_i = lax.broadcasted_iota(jnp.int32, (tq, tq), 0)
                qry_i = lax.broadcasted_iota(jnp.int32, (tq, tq), 1)
                s = jnp.where(key_i <= qry_i, s, NEG)
            m_old = m_ref[c]
            m_new = jnp.maximum(m_old, col_max(s) + block_bias)
            p = jnp.exp(s - (m_new - block_bias)).astype(BF16)
            acc_ref[c] = (jnp.exp(m_old - m_new) * acc_ref[c]
                          + jnp.dot(v_t, p, preferred_element_type=F32))
            m_ref[c] = m_new

    score(0, sa_ref)

    def pair(kk, carry):
        score(2 * kk + 1, sb_ref)
        consume(2 * kk, sa_ref, masked=False)
        score(2 * kk + 2, sa_ref)
        consume(2 * kk + 1, sb_ref, masked=False)
        return carry

    lax.fori_loop(0, qi // 2, pair, 0)

    @pl.when(qi % 2 == 1)
    def _():
        score(qi, sb_ref)
        consume(qi - 1, sa_ref, masked=False)
        consume(qi, sb_ref, masked=True)

    @pl.when(qi % 2 == 0)
    def _():
        consume(qi, sa_ref, masked=True)

    lam = _lam_value(lam_ref)
    a0 = acc_ref[0]
    a1 = acc_ref[1]
    att_t = a0[0:width] / a0[width:width + 1] - lam * (a1[0:width] / a1[width:width + 1])
    o_ref[...] = _rms(att_t.T, sg_ref[...], SUBLN_EPS) * (1.0 - LAM_INIT)


def _attn_prompt(q, k, v, seq, lam_rows, subln_g):
    m = q.shape[0]
    nb = m // seq
    tq = min(512, seq)
    assert tq % POS_SPLIT == 0 and tq // POS_SPLIT <= 256 and seq % tq == 0
    nq = seq // tq
    width = 2 * HD_B
    q_spec = pl.BlockSpec((tq, width), lambda b, h, qi: (b * nq + qi, h))
    kv_spec = pl.BlockSpec((seq, width), lambda b, h, qi: (b, h))
    return pl.pallas_call(
        _attn_prompt_kernel,
        grid=(nb, H_B, nq),
        in_specs=[q_spec, kv_spec, kv_spec,
                  pl.BlockSpec((4, HD_B), lambda *_: (0, 0)),
                  pl.BlockSpec((1, width), lambda *_: (0, 0))],
        out_specs=q_spec,
        out_shape=jax.ShapeDtypeStruct((m, W_B), F32),
        scratch_shapes=[pltpu.VMEM((2, nq, tq, width), BF16),
                        pltpu.VMEM((nq, width + ONES_ROWS, tq), BF16),
                        pltpu.VMEM((2, 1, tq), F32),
                        pltpu.VMEM((2, width + ONES_ROWS, tq), F32),
                        pltpu.VMEM((2, tq, tq), F32),
                        pltpu.VMEM((2, tq, tq), F32)],
        compiler_params=_params(("parallel", "parallel", "arbitrary")),
        name="attn_prompt",
    )(q, k, v, lam_rows, subln_g)


def _attn_sample_kernel(pt_ref, q_ref, kn_ref, vn_ref, lam_ref, sg_ref, *rest,
                        n_pages, past, n_seq):
    del pt_ref
    o_ref, s_ref = rest[2 * n_pages * n_seq], rest[2 * n_pages * n_seq + 1]
    for i in range(n_seq):
        k_pages = rest[i * n_pages:(i + 1) * n_pages]
        v_pages = rest[(n_seq + i) * n_pages:(n_seq + i + 1) * n_pages]
        _attn_sample_one(q_ref.at[i], kn_ref.at[i], vn_ref.at[i], lam_ref, sg_ref,
                         k_pages, v_pages, o_ref.at[i], s_ref.at[i], past)


def _attn_sample_one(q_ref, kn_ref, vn_ref, lam_ref, sg_ref, k_pages, v_pages, o_ref, s_ref,
                     past):
    n_pages = len(k_pages)
    tq = q_ref.shape[0]
    rows = 2 * H_B * tq
    nt = (((1,), (1,)), ((), ()))

    q = q_ref[...] * (HD_B ** -0.5)
    q_rep = jnp.concatenate([q] * (2 * H_B), axis=0)
    r_i = lax.broadcasted_iota(jnp.int32, (rows, W_B), 0)
    c_i = lax.broadcasted_iota(jnp.int32, (rows, W_B), 1)
    q_blk = jnp.where(r_i // tq == c_i // HD_B, q_rep, 0.0).astype(BF16)

    row = lax.broadcasted_iota(jnp.int32, (rows, PAGE_SIZE), 0)
    lane = lax.broadcasted_iota(jnp.int32, (rows, PAGE_SIZE), 1)
    head = row // (2 * tq)
    slope = jnp.exp2(-8.0 * (head + 1).astype(F32) / H_B)
    q_pos = past + row % tq

    def page(ref):
        return jnp.concatenate([ref[pl.ds(h, PAGE_SIZE, stride=H_B), :] for h in range(H_B)],
                               axis=-1).astype(BF16)

    for p in range(n_pages):
        s = lax.dot_general(q_blk, page(k_pages[p]), nt, preferred_element_type=F32)
        dist = q_pos - (p * PAGE_SIZE + lane)
        s_ref[:, p * PAGE_SIZE:(p + 1) * PAGE_SIZE] = s - slope * dist.astype(F32)

    kn = jnp.concatenate(
        [kn_ref[...], jnp.zeros((PAGE_SIZE - tq, W_B), F32)], axis=0).astype(BF16)
    s = lax.dot_general(q_blk, kn, nt, preferred_element_type=F32)
    dist = q_pos - (past + lane)
    s = jnp.where((dist >= 0) & (lane < tq), s - slope * dist.astype(F32), NEG)
    s_ref[:, past:past + PAGE_SIZE] = s

    s = s_ref[...]
    mx = jnp.max(s, axis=-1, keepdims=True)
    p_un = jnp.exp(s - mx)
    pn = p_un / jnp.sum(p_un, axis=-1, keepdims=True)
    lam = _lam_value(lam_ref)
    attn = jnp.concatenate(
        [pn[(2 * h) * tq:(2 * h + 1) * tq] - lam * pn[(2 * h + 1) * tq:(2 * h + 2) * tq]
         for h in range(H_B)], axis=0).astype(BF16)

    acc = jnp.zeros((H_B * tq, W_B), F32)
    for p in range(n_pages):
        acc = acc + jnp.dot(attn[:, p * PAGE_SIZE:(p + 1) * PAGE_SIZE],
                            page(v_pages[p]), preferred_element_type=F32)
    vn = jnp.concatenate(
        [vn_ref[...], jnp.zeros((PAGE_SIZE - tq, W_B), F32)], axis=0).astype(BF16)
    acc = acc + jnp.dot(attn[:, past:past + PAGE_SIZE], vn, preferred_element_type=F32)

    width = 2 * HD_B
    outs = []
    for h in range(H_B):
        att = acc[h * tq:(h + 1) * tq, h * width:(h + 1) * width]
        outs.append(_rms(att, sg_ref[...], SUBLN_EPS) * (1.0 - LAM_INIT))
    o_ref[...] = jnp.concatenate(outs, axis=-1)


def _attn_sample(q, k_new, v_new, cache_k, cache_v, page_table, lam_rows, subln_g):
    bd, tq, _ = q.shape
    n_pages = page_table.shape[1]
    past = n_pages * PAGE_SIZE
    n_seq = SAMPLE_SEQS_PER_STEP if bd % SAMPLE_SEQS_PER_STEP == 0 else 1
    tok_spec = pl.BlockSpec((n_seq, tq, W_B), lambda b, pt: (b, 0, 0))
    page_specs = [pl.BlockSpec((None, PAGE_SIZE * H_B, 2 * HD_B),
                               lambda b, pt, i=i, p=p: (pt[(b * n_seq + i) * n_pages + p], 0, 0))
                  for i in range(n_seq) for p in range(n_pages)]
    grid_spec = pltpu.PrefetchScalarGridSpec(
        num_scalar_prefetch=1,
        grid=(bd // n_seq,),
        in_specs=[tok_spec, tok_spec, tok_spec,
                  pl.BlockSpec((4, HD_B), lambda b, pt: (0, 0)),
                  pl.BlockSpec((1, 2 * HD_B), lambda b, pt: (0, 0))]
                 + page_specs + page_specs,
        out_specs=tok_spec,
        scratch_shapes=[pltpu.VMEM((n_seq, 2 * H_B * tq, past + PAGE_SIZE), F32)],
    )
    return pl.pallas_call(
        functools.partial(_attn_sample_kernel, n_pages=n_pages, past=past, n_seq=n_seq),
        grid_spec=grid_spec,
        out_shape=jax.ShapeDtypeStruct((bd, tq, W_B), F32),
        compiler_params=_params(("parallel",)),
        name="attn_sample",
    )(page_table.reshape(-1), q, k_new, v_new, lam_rows, subln_g,
      *([cache_k] * (n_pages * n_seq)), *([cache_v] * (n_pages * n_seq)))


def _merge_ffn_kernel(x_ref, yn_ref, bv_ref, g_ref, ob_ref, gate_ref, wa_ref, wb_ref, wo_ref,
                      gf_ref, wu_ref, wd_ref, gl_ref, y_ref):
    ga = gate_ref[:, 0:D_MODEL]
    gb = gate_ref[:, D_MODEL:GATE_COLS]
    o_a = (yn_ref[...] + bv_ref[...]) * g_ref[...]
    m = ga * _mm(o_a, wa_ref[...]) + gb * _mm(ob_ref[...], wb_ref[...])
    x1 = x_ref[...] + _mm(m, wo_ref[...])
    hn = _rms(x1, gf_ref[...], NORM_EPS).astype(BF16)
    acc = x1
    chunk = D_MODEL
    for c in range(D_FF // chunk):
        up = jnp.dot(hn, wu_ref[:, c * chunk:(c + 1) * chunk], preferred_element_type=F32)
        act = jnp.square(jnp.maximum(up, 0.0))
        acc = acc + _mm(act, wd_ref[c * chunk:(c + 1) * chunk, :])
    y_ref[...] = _rms(acc, gl_ref[...], NORM_EPS)


def _merge_ffn(x2d, yn, bv, g, o_b, gates, w_br_a, w_br_b, w_out, g_ffn, w_up, w_down,
               g_final):
    m = x2d.shape[0]
    tm = min(512, m)
    row_spec = lambda w: pl.BlockSpec((tm, w), lambda i: (i, 0))
    full = lambda a: _const_spec(a.shape)
    return pl.pallas_call(
        _merge_ffn_kernel,
        grid=(m // tm,),
        in_specs=[row_spec(D_MODEL), row_spec(W_A), row_spec(W_A), row_spec(W_A),
                  row_spec(W_B), row_spec(GATE_COLS),
                  full(w_br_a), full(w_br_b), full(w_out), full(g_ffn), full(w_up),
                  full(w_down), full(g_final)],
        out_specs=row_spec(D_MODEL),
        out_shape=jax.ShapeDtypeStruct((m, D_MODEL), F32),
        compiler_params=_params(("parallel",)),
        name="merge_ffn",
    )(x2d, yn, bv, g, o_b, gates, w_br_a, w_br_b, w_out, g_ffn, w_up, w_down, g_final)


def _rwkv_branch_prompt(pr, params, nb, seq, tc, ln_w, ln_b):
    groups = LANES // (nb * H_A)
    ni = HD_A // groups
    pr3 = pr.reshape(nb, seq, RWKV_COLS)
    scan_view = lambda x: x.reshape(1, seq, x.shape[0] // seq, LANES)
    ga, gb = _pre_scan(pr3, params, tc, 0)
    gk, gr, gv, pc, g, bv = _pre_scan(pr3, params, tc, 1)
    pc = pc.transpose(0, 2, 1, 3).reshape(seq // tc, nb, H_A, HD_A).transpose(0, 3, 1, 2)
    pc = jnp.tile(pc.reshape(1, seq // tc, HD_A, nb * H_A), (1, 1, 1, groups))
    s0 = jnp.zeros((1, ni, HD_A, LANES), F32)
    y, s = _rwkv_scan(scan_view(ga), scan_view(gb), scan_view(gk), scan_view(gr),
                      scan_view(gv), pc, s0, tc=tc)
    y = _from_scan(y[0], nb, ln_w, ln_b).reshape(nb * seq, W_A)
    s = s.reshape(ni, HD_A, groups, nb, H_A).transpose(3, 4, 0, 2, 1)
    flat = lambda x: x.reshape(nb * seq, W_A)
    return y, flat(g), flat(bv), s.reshape(nb, H_A, HD_A, HD_A)


def _rwkv_branch_sample(pr, a_s, b_s, k_s, r_s, pc, state, nb, seq, ln_w, ln_b):
    chains = nb * H_A
    ng = chains // LANES

    def lay(x):
        rows = x.shape[0] // nb
        x = x.reshape(nb, rows, H_A, HD_A).transpose(1, 3, 0, 2).reshape(rows, HD_A, ng, LANES)
        return x.transpose(2, 0, 1, 3)

    s0 = state.transpose(2, 3, 0, 1).reshape(HD_A, HD_A, ng, LANES).transpose(2, 0, 1, 3)
    y, s = _rwkv_scan(lay(a_s), lay(b_s), lay(k_s), lay(r_s), lay(pr[:, 2 * W_A:3 * W_A]),
                      lay(pc), s0, tc=seq)
    y = y.transpose(1, 2, 0, 3).reshape(seq, HD_A, nb, H_A).transpose(2, 0, 3, 1)
    s = s.transpose(1, 2, 0, 3).reshape(HD_A, HD_A, nb, H_A).transpose(2, 3, 0, 1)
    return _group_norm(y.reshape(nb * seq, W_A), ln_w, ln_b), s


def kernel(x_prompt, x_sample, cache_k, cache_v, state_rwkv, state_shift, page_table,
           w_in, mu_shift, w0, w2, a0, a2, g2, k_k, k_a, r_k, ln_x_w, ln_x_b,
           lam_q1, lam_k1, lam_q2, lam_k2, subln_g, w_br_a, w_br_b, w_out,
           g_mix, g_ffn, w_up, w_down, g_final):
    nb, seq, _ = x_prompt.shape
    bd, dseq, _ = x_sample.shape
    w_in_b = w_in[0].astype(BF16)
    wa_b, wb_b, wo_b = (w[0].astype(BF16) for w in (w_br_a, w_br_b, w_out))
    wu_b, wd_b = w_up[0].astype(BF16), w_down[0].astype(BF16)
    r_k2 = r_k.reshape(1, W_A)
    lam_rows = jnp.concatenate([lam_q1, lam_k1, lam_q2, lam_k2], axis=0)
    g_fin = g_final.reshape(1, D_MODEL)

    rwkv_params = (w0, w2[0], a0, a2[0], g2[0], k_k, k_a, r_k2)

    def finish(x2d, yn, bv, g, o_b, gates):
        return _merge_ffn(x2d, yn, bv, g, o_b, gates, wa_b, wb_b, wo_b, g_ffn, wu_b, wd_b,
                          g_fin)

    xp = x_prompt.reshape(nb * seq, D_MODEL)
    pr, q, k, v, kt, vt, gates, x_last = _inproj_prompt(xp, seq, g_mix, w_in_b, mu_shift)
    yn, g, bv, s_prompt = _rwkv_branch_prompt(pr, rwkv_params, nb, seq, SCAN_CHUNK,
                                              ln_x_w, ln_x_b)
    o_b = _attn_prompt(q, k, v, seq, lam_rows, subln_g)
    y_prompt = finish(xp, yn, bv, g, o_b, gates).reshape(nb, seq, D_MODEL)
    k_prompt = kt.reshape(1, nb, seq, H_B, 2 * HD_B)
    v_prompt = vt.reshape(1, nb, seq, H_B, 2 * HD_B)
    shift_prompt = x_last.reshape(1, nb, D_MODEL)

    xs = x_sample.reshape(bd * dseq, D_MODEL)
    shift_pad = jnp.zeros((bd, dseq, D_MODEL), F32).at[:, 0].set(state_shift[0])
    pr, q, k, v, kt, vt, gates, xn = _inproj_sample(
        xs, dseq, shift_pad.reshape(bd * dseq, D_MODEL), g_mix, w_in_b, mu_shift)
    a_s, b_s, k_s, r_s, pc, g, bv = _rwkv_pre(pr, *rwkv_params, dseq)
    yn, s_sample = _rwkv_branch_sample(pr, a_s, b_s, k_s, r_s, pc, state_rwkv[0], bd, dseq,
                                       ln_x_w, ln_x_b)
    tok = lambda t: t.reshape(bd, dseq, W_B)
    pages = lambda c: c[0].reshape(c.shape[1], PAGE_SIZE * H_B, 2 * HD_B)
    o_b = _attn_sample(tok(q), tok(k), tok(v), pages(cache_k), pages(cache_v),
                       page_table, lam_rows, subln_g)
    y_sample = finish(xs, yn, bv, g, o_b.reshape(bd * dseq, W_B), gates)
    y_sample = y_sample.reshape(bd, dseq, D_MODEL)
    k_sample = kt.reshape(1, bd, dseq, H_B, 2 * HD_B)
    v_sample = vt.reshape(1, bd, dseq, H_B, 2 * HD_B)
    shift_sample = xn.reshape(bd, dseq, D_MODEL)[:, -1][None]

    return (y_prompt, y_sample, k_prompt, v_prompt, k_sample, v_sample,
            s_prompt[None], s_sample[None], shift_prompt, shift_sample)
```

```python
import functools

import jax
import jax.numpy as jnp
from jax import lax
from jax.experimental import pallas as pl
from jax.experimental.pallas import tpu as pltpu

F32 = jnp.float32
BF16 = jnp.bfloat16

D_MODEL = 1024
H_A = 8
HD_A = 64
W_A = H_A * HD_A
DECAY_LORA = 64
AAA_LORA = 64
GATE_LORA = 128
RWKV_COLS = 3 * W_A + DECAY_LORA + AAA_LORA + GATE_LORA
H_B = 4
HD_B = 64
W_B = H_B * 2 * HD_B
GATE_COLS = 2 * D_MODEL
N_COLS = RWKV_COLS + 3 * W_B + GATE_COLS
D_FF = 4 * D_MODEL
PAGE_SIZE = 128
NORM_EPS = 1e-6
GN_EPS = 64e-5
SUBLN_EPS = 1e-5
NEG = -1e30
LAM_INIT = 0.8 - 0.6 * 1.0

LANES = 128
SUBLANES = 8
MXU_TILE = 256
VMEM_LIMIT = 56 * 1024 * 1024
SCAN_CHUNK = 32
SAMPLE_SEQS_PER_STEP = 2
ATTN_HEADS_PER_STEP = 2

OFF_Q = RWKV_COLS
OFF_K = OFF_Q + W_B
OFF_V = OFF_K + W_B
OFF_GA = OFF_V + W_B


def _params(sem):
    return pltpu.CompilerParams(dimension_semantics=sem, vmem_limit_bytes=VMEM_LIMIT)


def _const_spec(shape):
    nd = len(shape)
    return pl.BlockSpec(shape, lambda *_: (0,) * nd, pipeline_mode=pl.Buffered(1))


def _rms(x, g, eps):
    return x * lax.rsqrt(jnp.mean(x * x, axis=-1, keepdims=True) + eps) * g


def _mm(a, w):
    return jnp.dot(a.astype(BF16), w, preferred_element_type=F32)


def _split2(x):
    hi = x.astype(BF16)
    lo = (x - hi.astype(F32)).astype(BF16)
    return hi, lo


def _head_sum(x, head):
    width = x.shape[1]
    tile = min(MXU_TILE, width)
    r = lax.broadcasted_iota(jnp.int32, (tile, tile), 0) // head
    c = lax.broadcasted_iota(jnp.int32, (tile, tile), 1) // head
    ones = (r == c).astype(BF16)
    hi, lo = _split2(x)
    parts = []
    for s in range(width // tile):
        cols = slice(s * tile, (s + 1) * tile)
        parts.append(jnp.dot(hi[:, cols], ones, preferred_element_type=F32)
                     + jnp.dot(lo[:, cols], ones, preferred_element_type=F32))
    return jnp.concatenate(parts, axis=-1)


def _proj_tail(xn, w_ref, q_ref, k_ref, v_ref, kt_ref, vt_ref, gate_ref):
    xb = xn.astype(BF16)
    tm = xn.shape[0]
    q_ref[...] = jnp.dot(xb, w_ref[:, OFF_Q:OFF_K], preferred_element_type=F32)
    k = jnp.dot(xb, w_ref[:, OFF_K:OFF_V], preferred_element_type=F32)
    v = jnp.dot(xb, w_ref[:, OFF_V:OFF_GA], preferred_element_type=F32)
    k_ref[...] = k
    v_ref[...] = v
    for h in range(H_B):
        cols = slice(h * 2 * HD_B, (h + 1) * 2 * HD_B)
        kt_ref[pl.ds(h, tm, stride=H_B), :] = k[:, cols]
        vt_ref[pl.ds(h, tm, stride=H_B), :] = v[:, cols]
    gate_ref[...] = jax.nn.sigmoid(
        jnp.dot(xb, w_ref[:, OFF_GA:N_COLS], preferred_element_type=F32))
    return xb


def _inproj_prompt_kernel(x_ref, g_ref, w_ref, mu_ref,
                          pr_ref, q_ref, k_ref, v_ref, kt_ref, vt_ref, gate_ref, xl_ref,
                          carry_ref):
    tm = x_ref.shape[0]
    xn = _rms(x_ref[...], g_ref[...], NORM_EPS)
    xl_ref[0] = xn[tm - 1:tm, :]
    xb = _proj_tail(xn, w_ref, q_ref, k_ref, v_ref, kt_ref, vt_ref, gate_ref)
    pr = jnp.dot(xb, w_ref[:, 0:RWKV_COLS], preferred_element_type=F32)

    @pl.when(pl.program_id(1) == 0)
    def _():
        carry_ref[...] = jnp.zeros_like(carry_ref)

    prev = pltpu.roll(pr, 1, 0)
    row = lax.broadcasted_iota(jnp.int32, pr.shape, 0)
    prev = jnp.where(row == 0, carry_ref[SUBLANES - 1:SUBLANES, :], prev)
    carry_ref[...] = pr[tm - SUBLANES:tm, :]
    pr_ref[...] = pr + (prev - pr) * mu_ref[...]


def _inproj_prompt(x2d, seq, g_mix, w_in, mu):
    m = x2d.shape[0]
    nb = m // seq
    tm = min(512, seq)
    nt = seq // tm
    row_spec = lambda w: pl.BlockSpec((tm, w), lambda b, t: (b * nt + t, 0))
    head_spec = pl.BlockSpec((tm * H_B, 2 * HD_B), lambda b, t: (b * nt + t, 0))
    return pl.pallas_call(
        _inproj_prompt_kernel,
        grid=(nb, nt),
        in_specs=[row_spec(D_MODEL), _const_spec((1, D_MODEL)),
                  _const_spec((D_MODEL, N_COLS)), _const_spec((1, RWKV_COLS))],
        out_specs=[row_spec(RWKV_COLS), row_spec(W_B), row_spec(W_B), row_spec(W_B),
                   head_spec, head_spec, row_spec(GATE_COLS),
                   pl.BlockSpec((1, 1, D_MODEL), lambda b, t: (b, 0, 0))],
        out_shape=[jax.ShapeDtypeStruct((m, RWKV_COLS), F32),
                   jax.ShapeDtypeStruct((m, W_B), F32),
                   jax.ShapeDtypeStruct((m, W_B), F32),
                   jax.ShapeDtypeStruct((m, W_B), F32),
                   jax.ShapeDtypeStruct((m * H_B, 2 * HD_B), F32),
                   jax.ShapeDtypeStruct((m * H_B, 2 * HD_B), F32),
                   jax.ShapeDtypeStruct((m, GATE_COLS), F32),
                   jax.ShapeDtypeStruct((nb, 1, D_MODEL), F32)],
        scratch_shapes=[pltpu.VMEM((SUBLANES, RWKV_COLS), F32)],
        compiler_params=_params(("parallel", "arbitrary")),
        name="inproj_prompt",
    )(x2d, g_mix, w_in, mu)


def _inproj_sample_kernel(x_ref, sp_ref, g_ref, w_ref, mu_ref,
                          pr_ref, q_ref, k_ref, v_ref, kt_ref, vt_ref, gate_ref, xn_ref, *,
                          seq):
    xn = _rms(x_ref[...], g_ref[...], NORM_EPS)
    xn_ref[...] = xn
    xb = _proj_tail(xn, w_ref, q_ref, k_ref, v_ref, kt_ref, vt_ref, gate_ref)
    row = lax.broadcasted_iota(jnp.int32, xn.shape, 0)
    xprev = jnp.where(row % seq == 0, sp_ref[...], pltpu.roll(xn, 1, 0))
    w_r = w_ref[:, 0:RWKV_COLS]
    pr = jnp.dot(xb, w_r, preferred_element_type=F32)
    prev = jnp.dot(xprev.astype(BF16), w_r, preferred_element_type=F32)
    pr_ref[...] = pr + (prev - pr) * mu_ref[...]


def _inproj_sample(x2d, seq, shift_pad, g_mix, w_in, mu):
    m = x2d.shape[0]
    tm = min(256, m)
    assert tm % seq == 0 and m % tm == 0
    row_spec = lambda w: pl.BlockSpec((tm, w), lambda i: (i, 0))
    head_spec = pl.BlockSpec((tm * H_B, 2 * HD_B), lambda i: (i, 0))
    return pl.pallas_call(
        functools.partial(_inproj_sample_kernel, seq=seq),
        grid=(m // tm,),
        in_specs=[row_spec(D_MODEL), row_spec(D_MODEL), _const_spec((1, D_MODEL)),
                  _const_spec((D_MODEL, N_COLS)), _const_spec((1, RWKV_COLS))],
        out_specs=[row_spec(RWKV_COLS), row_spec(W_B), row_spec(W_B), row_spec(W_B),
                   head_spec, head_spec, row_spec(GATE_COLS), row_spec(D_MODEL)],
        out_shape=[jax.ShapeDtypeStruct((m, RWKV_COLS), F32),
                   jax.ShapeDtypeStruct((m, W_B), F32),
                   jax.ShapeDtypeStruct((m, W_B), F32),
                   jax.ShapeDtypeStruct((m, W_B), F32),
                   jax.ShapeDtypeStruct((m * H_B, 2 * HD_B), F32),
                   jax.ShapeDtypeStruct((m * H_B, 2 * HD_B), F32),
                   jax.ShapeDtypeStruct((m, GATE_COLS), F32),
                   jax.ShapeDtypeStruct((m, D_MODEL), F32)],
        compiler_params=_params(("parallel",)),
        name="inproj_sample",
    )(x2d, shift_pad, g_mix, w_in, mu)


def _dot_f32(a, b):
    a_hi, a_lo = _split2(a)
    b_hi, b_lo = _split2(b)
    d = lambda x, y: jnp.dot(x, y, preferred_element_type=F32)
    return d(a_hi, b_hi) + d(a_lo, b_hi) + d(a_hi, b_lo)


def _pre_math(pr, params, p_scr, chunk):
    w0, w2, a0, a2, g2, k_k, k_a, r_k = params
    tm = pr.shape[0]
    o_wd = 3 * W_A
    o_ad = o_wd + DECAY_LORA
    o_gd = o_ad + AAA_LORA
    k = pr[:, W_A:2 * W_A]
    wd = pr[:, o_wd:o_ad]
    ad = pr[:, o_ad:o_gd]
    gd = pr[:, o_gd:RWKV_COLS]
    z = -(w0 + _dot_f32(jnp.tanh(wd), w2))
    softplus = jnp.maximum(z, 0.0) + jnp.log1p(jnp.exp(-jnp.abs(z)))
    w = -softplus - 0.5
    log_decay = -jnp.exp(w)
    r_i = lax.broadcasted_iota(jnp.int32, (tm, tm), 0)
    c_i = lax.broadcasted_iota(jnp.int32, (tm, tm), 1)
    tri = ((r_i // chunk == c_i // chunk) & (c_i <= r_i)).astype(BF16)
    hi = log_decay.astype(BF16)
    rest = log_decay - hi.astype(F32)
    mid = rest.astype(BF16)
    lo = (rest - mid.astype(F32)).astype(BF16)
    run = lambda t: jnp.dot(tri, t, preferred_element_type=F32)
    cum = run(hi) + run(mid) + run(lo)
    p_incl = jnp.exp(cum)
    p_inv = jnp.exp(-cum)
    p_prev = jnp.exp(cum - log_decay)
    pc = []
    for s in range(W_A // LANES):
        p_scr[s] = p_incl[:, s * LANES:(s + 1) * LANES]
        pc.append(p_scr[s, pl.ds(chunk - 1, tm // chunk, stride=chunk), :])
    pc = jnp.concatenate(pc, axis=-1)
    a = jax.nn.sigmoid(a0 + _dot_f32(ad, a2))
    g = _dot_f32(jax.nn.sigmoid(gd), g2)
    kk = k * k_k
    norm = jnp.sqrt(_head_sum(kk * kk, HD_A))
    kk = kk / jnp.maximum(norm, 1e-12)
    k_mod = k * (1.0 + (a - 1.0) * k_a)
    r = pr[:, 0:W_A]
    v = pr[:, 2 * W_A:3 * W_A]
    bv = _head_sum(r * k_mod * r_k, HD_A) * v
    return -kk * p_prev, kk * a * p_inv, k_mod * p_inv, r * p_incl, pc, g, bv


def _rwkv_pre_kernel(pr_ref, *refs, chunk):
    params = tuple(ref[...] for ref in refs[:8])
    outs, p_scr = refs[8:15], refs[15]
    for out_ref, val in zip(outs, _pre_math(pr_ref[...], params, p_scr, chunk)):
        out_ref[...] = val


def _rwkv_pre(pr, w0, w2, a0, a2, g2, k_k, k_a, r_k, chunk):
    m = pr.shape[0]
    tm = min(256, m)
    assert tm % chunk == 0 and (tm // chunk) % SUBLANES == 0
    row_spec = lambda w: pl.BlockSpec((tm, w), lambda i: (i, 0))
    full = lambda a: _const_spec(a.shape)
    out = jax.ShapeDtypeStruct((m, W_A), F32)
    return pl.pallas_call(
        functools.partial(_rwkv_pre_kernel, chunk=chunk),
        grid=(m // tm,),
        in_specs=[row_spec(RWKV_COLS), full(w0), full(w2), full(a0), full(a2), full(g2),
                  full(k_k), full(k_a), full(r_k)],
        out_specs=[row_spec(W_A)] * 4 + [pl.BlockSpec((tm // chunk, W_A), lambda i: (i, 0))]
                  + [row_spec(W_A)] * 2,
        out_shape=[out] * 4 + [jax.ShapeDtypeStruct((m // chunk, W_A), F32)] + [out] * 2,
        scratch_shapes=[pltpu.VMEM((W_A // LANES, tm, LANES), F32)],
        compiler_params=_params(("parallel",)),
        name="rwkv_pre",
    )(pr, w0, w2, a0, a2, g2, k_k, k_a, r_k)


def _sublane_allsum(x):
    n = x.shape[0] // SUBLANES
    acc = x[0:SUBLANES]
    for g in range(1, n):
        acc = acc + x[g * SUBLANES:(g + 1) * SUBLANES]
    shift = SUBLANES // 2
    while shift >= 1:
        acc = acc + pltpu.roll(acc, shift, 0)
        shift //= 2
    return acc


def _rwkv_scan_kernel(a_ref, b_ref, k_ref, r_ref, v_ref, pc_ref, s0_ref, y_ref, s_ref):
    tc = a_ref.shape[0]
    ni, nkey = s0_ref.shape[0], s0_ref.shape[1]
    ng = nkey // SUBLANES
    pad_rows = y_ref.shape[1] - ni

    @pl.when(pl.program_id(1) == 0)
    def _():
        s_ref[...] = s0_ref[...]

    sub = lax.broadcasted_iota(jnp.int32, (SUBLANES, LANES), 0)
    keys = pl.ds(0, nkey)

    def tile8(x):
        return jnp.concatenate([x] * ng, axis=0)

    def step(t, carry):
        def group(ig, carry2):
            y_acc = jnp.zeros((SUBLANES, LANES), F32)
            for ii in range(SUBLANES):
                i = ig * SUBLANES + ii
                s_i = s_ref[i]
                sa = _sublane_allsum(s_i * a_ref[t, keys])
                v_i = jnp.broadcast_to(v_ref[t, pl.ds(i, 1), :], (SUBLANES, LANES))
                s_new = s_i + tile8(sa) * b_ref[t, keys] + tile8(v_i) * k_ref[t, keys]
                s_ref[i] = s_new
                y_i = _sublane_allsum(s_new * r_ref[t, keys])
                y_acc = jnp.where(sub == ii, y_i, y_acc)
            y_ref[t, pl.ds(pl.multiple_of(ig * SUBLANES, SUBLANES), SUBLANES), :] = y_acc
            return carry2

        n_groups = ni // SUBLANES
        if n_groups <= 2:
            for ig in range(n_groups):
                group(ig, 0)
        else:
            lax.fori_loop(0, n_groups, group, 0)
        if pad_rows:
            y_ref[t, pl.ds(ni, pad_rows), :] = jnp.zeros((pad_rows, LANES), F32)
        return carry

    lax.fori_loop(0, tc, step, 0, unroll=2)

    def rescale(i, carry):
        s_ref[i] = s_ref[i] * pc_ref[0, keys]
        return carry

    lax.fori_loop(0, ni, rescale, 0, unroll=8)


def _rwkv_scan(a, b, k, r, v, pc, s0, tc):
    g, t, pk, _ = a.shape
    pv = v.shape[2]
    _, ni, nkey, _ = s0.shape
    x_spec = pl.BlockSpec((None, tc, pk, LANES), lambda gi, ti: (gi, ti, 0, 0))
    v_spec = pl.BlockSpec((None, tc, pv, LANES), lambda gi, ti: (gi, ti, 0, 0))
    s_spec = pl.BlockSpec((None, ni, nkey, LANES), lambda gi, ti: (gi, 0, 0, 0))
    pc_spec = pl.BlockSpec((None, 1, pc.shape[2], LANES), lambda gi, ti: (gi, ti, 0, 0))
    return pl.pallas_call(
        _rwkv_scan_kernel,
        grid=(g, t // tc),
        in_specs=[x_spec] * 4 + [v_spec, pc_spec, s_spec],
        out_specs=[v_spec, s_spec],
        out_shape=[jax.ShapeDtypeStruct((g, t, pv, LANES), F32),
                   jax.ShapeDtypeStruct((g, ni, nkey, LANES), F32)],
        compiler_params=_params(("parallel", "arbitrary")),
        name="rwkv_scan",
    )(a, b, k, r, v, pc, s0)


KEY_PITCH = HD_A + SUBLANES
TIME_TILE = LANES


def _val_pitch(n_rows):
    p = n_rows + SUBLANES
    return p if (p // SUBLANES) % 2 else p + SUBLANES


def _scan_round1(tile, nb, c_scr):
    heads_per_tile = LANES // HD_A
    for b in range(nb):
        for hp in range(W_A // LANES):
            tile_t = tile(b, hp).T
            for h2 in range(heads_per_tile):
                c = b * H_A + hp * heads_per_tile + h2
                c_scr[c * KEY_PITCH:c * KEY_PITCH + HD_A, :] = tile_t[h2 * HD_A:(h2 + 1) * HD_A]


def _scan_round2(c_scr, nb, g_ref, value_indexed, pitch):
    tt = c_scr.shape[1]
    chains = nb * H_A
    groups = LANES // chains
    chain_rows = lambda j: c_scr[pl.ds(j, chains, stride=KEY_PITCH), :]
    n_out = HD_A // groups if value_indexed else HD_A
    for n in range(n_out):
        if value_indexed:
            rows = [chain_rows(n * groups + r) for r in range(groups)]
        else:
            rows = [chain_rows(n)] * groups
        g_ref[pl.ds(n, tt, stride=pitch), :] = jnp.concatenate(rows, axis=0).T
    for n in range(n_out, pitch):
        g_ref[pl.ds(n, tt, stride=pitch), :] = jnp.zeros((tt, LANES), F32)


def _pre_scan_kernel(pr_ref, *refs, chunk, part, n_tiles):
    nb, tt, _ = pr_ref.shape
    params = tuple(ref[...] for ref in refs[:8])
    n_ops = 2 if part == 0 else 3
    outs = refs[8:-(2 * n_ops + 1)]
    sets = (refs[-(2 * n_ops + 1):-(n_ops + 1)], refs[-(n_ops + 1):-1])
    p_scr = refs[-1]
    i = pl.program_id(0)
    vp = outs[2].shape[0] // tt if part == 1 else None
    lanes = lambda hp: slice(hp * LANES, (hp + 1) * LANES)

    @pl.when(i == 0)
    def _():
        for c_scr in sets[1]:
            c_scr[...] = jnp.zeros_like(c_scr)

    def body(done, fresh):
        for idx in range(2):
            _scan_round2(done[idx], nb, outs[idx], False, KEY_PITCH)
        if part == 1:
            _scan_round2(done[2], nb, outs[2], True, vp)
        vals = [_pre_math(pr_ref[b], params, p_scr, chunk) for b in range(nb)]
        first = 0 if part == 0 else 2
        for idx in range(2):
            _scan_round1(lambda b, hp: vals[b][first + idx][:, lanes(hp)], nb, fresh[idx])
        if part == 1:
            v_cols = lambda hp: slice(2 * W_A + hp * LANES, 2 * W_A + (hp + 1) * LANES)
            _scan_round1(lambda b, hp: pr_ref[b, :, v_cols(hp)], nb, fresh[2])
            pc_ref, g_ref, bv_ref = outs[3:6]
            for b in range(nb):
                pc_ref[b] = vals[b][4]
                g_ref[b] = vals[b][5]
                bv_ref[b] = vals[b][6]

    @pl.when(i % 2 == 0)
    def _():
        body(sets[1], sets[0])

    @pl.when(i % 2 == 1)
    def _():
        body(sets[0], sets[1])


def _pre_scan(pr3, params, chunk, part):
    nb, t, _ = pr3.shape
    groups = LANES // (nb * H_A)
    vp = _val_pitch(HD_A // groups)
    n_tiles = t // TIME_TILE
    cpt = TIME_TILE // chunk
    this = lambda i: jnp.minimum(i, n_tiles - 1)
    prev = lambda i: jnp.maximum(i - 1, 0)
    g_spec = lambda pitch: pl.BlockSpec((TIME_TILE * pitch, LANES), lambda i: (prev(i), 0))
    g_shape = lambda pitch: jax.ShapeDtypeStruct((t * pitch, LANES), F32)
    nat_spec = pl.BlockSpec((nb, TIME_TILE, W_A), lambda i: (0, this(i), 0))
    nat_shape = jax.ShapeDtypeStruct((nb, t, W_A), F32)
    if part == 0:
        out_specs = [g_spec(KEY_PITCH)] * 2
        out_shape = [g_shape(KEY_PITCH)] * 2
    else:
        out_specs = [g_spec(KEY_PITCH)] * 2 + [
            g_spec(vp), pl.BlockSpec((None, nb, cpt, W_A), lambda i: (this(i), 0, 0, 0)),
            nat_spec, nat_spec]
        out_shape = [g_shape(KEY_PITCH)] * 2 + [
            g_shape(vp), jax.ShapeDtypeStruct((n_tiles, nb, cpt, W_A), F32),
            nat_shape, nat_shape]
    n_ops = 2 if part == 0 else 3
    return pl.pallas_call(
        functools.partial(_pre_scan_kernel, chunk=chunk, part=part, n_tiles=n_tiles),
        grid=(n_tiles + 1,),
        in_specs=[pl.BlockSpec((nb, TIME_TILE, RWKV_COLS), lambda i: (0, this(i), 0))]
                 + [_const_spec(p.shape) for p in params],
        out_specs=out_specs,
        out_shape=out_shape,
        scratch_shapes=[pltpu.VMEM((nb * H_A * KEY_PITCH, TIME_TILE), F32)] * (2 * n_ops)
                       + [pltpu.VMEM((W_A // LANES, TIME_TILE, LANES), F32)],
        compiler_params=_params(("arbitrary",)),
        name="pre_scan",
    )(pr3, *params)


def _from_scan_kernel(y_ref, lnw_ref, lnb_ref, o_ref, c_scr, *, pitch):
    nb, tt, _ = o_ref.shape
    chains = nb * H_A
    groups = LANES // chains
    heads_per_tile = LANES // HD_A
    for n in range(HD_A // groups):
        lanes_t = y_ref[pl.ds(n, tt, stride=pitch), :].T
        for r in range(groups):
            c_scr[pl.ds(n * groups + r, chains, stride=KEY_PITCH), :] = (
                lanes_t[r * chains:(r + 1) * chains])
    for c in range(chains):
        rows = pl.ds(c * KEY_PITCH, HD_A)
        head = c % H_A
        y = c_scr[rows, :]
        mean = _sublane_allsum(y) * (1.0 / HD_A)
        d = y - jnp.concatenate([mean] * (HD_A // SUBLANES), axis=0)
        var = _sublane_allsum(d * d) * (1.0 / HD_A)
        inv = jnp.concatenate([lax.rsqrt(var + GN_EPS)] * (HD_A // SUBLANES), axis=0)
        c_scr[rows, :] = (d * inv * lnw_ref[head * HD_A:(head + 1) * HD_A, :]
                          + lnb_ref[head * HD_A:(head + 1) * HD_A, :])
    for b in range(nb):
        for hp in range(W_A // LANES):
            c0 = b * H_A + hp * heads_per_tile
            tile_t = jnp.concatenate(
                [c_scr[(c0 + h2) * KEY_PITCH:(c0 + h2) * KEY_PITCH + HD_A, :]
                 for h2 in range(heads_per_tile)], axis=0)
            o_ref[b, :, hp * LANES:(hp + 1) * LANES] = tile_t.T


def _from_scan(y, nb, ln_w, ln_b):
    t, pitch, _ = y.shape
    cols = lambda p: jnp.broadcast_to(p.reshape(W_A, 1), (W_A, TIME_TILE))
    return pl.pallas_call(
        functools.partial(_from_scan_kernel, pitch=pitch),
        grid=(t // TIME_TILE,),
        in_specs=[pl.BlockSpec((TIME_TILE * pitch, LANES), lambda i: (i, 0)),
                  _const_spec((W_A, TIME_TILE)), _const_spec((W_A, TIME_TILE))],
        out_specs=pl.BlockSpec((nb, TIME_TILE, W_A), lambda i: (0, i, 0)),
        out_shape=jax.ShapeDtypeStruct((nb, t, W_A), F32),
        scratch_shapes=[pltpu.VMEM((nb * H_A * KEY_PITCH, TIME_TILE), F32)],
        compiler_params=_params(("parallel",)),
        name="from_scan",
    )(y.reshape(t * pitch, LANES), cols(ln_w), cols(ln_b))


def _group_norm_kernel(y_ref, lnw_ref, lnb_ref, o_ref):
    y = y_ref[...]
    mean = _head_sum(y, HD_A) * (1.0 / HD_A)
    d = y - mean
    var = _head_sum(d * d, HD_A) * (1.0 / HD_A)
    o_ref[...] = d * lax.rsqrt(var + GN_EPS) * lnw_ref[...] + lnb_ref[...]


def _group_norm(y, ln_w, ln_b):
    m = y.shape[0]
    tm = min(256, m)
    row_spec = pl.BlockSpec((tm, W_A), lambda i: (i, 0))
    vec = _const_spec((1, W_A))
    return pl.pallas_call(
        _group_norm_kernel,
        grid=(m // tm,),
        in_specs=[row_spec, vec, vec],
        out_specs=row_spec,
        out_shape=jax.ShapeDtypeStruct((m, W_A), F32),
        compiler_params=_params(("parallel",)),
        name="group_norm",
    )(y, ln_w, ln_b)


def _lam_value(lam_ref):
    lv = lam_ref[...]
    s1 = jnp.sum(lv[0:1] * lv[1:2], axis=-1, keepdims=True)
    s2 = jnp.sum(lv[2:3] * lv[3:4], axis=-1, keepdims=True)
    return jnp.exp(s1) - jnp.exp(s2) + LAM_INIT


POS_SPLIT = 16
ONES_ROWS = 16


def _attn_prompt_kernel(q_ref, k_ref, v_ref, lam_ref, sg_ref, o_ref,
                        kaug_ref, vt_ref, m_ref, acc_ref, sa_ref, sb_ref):
    qi = pl.program_id(2)
    tq = q_ref.shape[0]
    nblk = k_ref.shape[0] // tq
    width = 2 * HD_B
    n_heads = q_ref.shape[1] // width
    heads = range(n_heads)
    head_cols = lambda hh: slice(hh * width, (hh + 1) * width)
    nt = (((1,), (1,)), ((), ()))
    head0 = pl.program_id(1) * n_heads
    slope = [jnp.exp2(jnp.full((1, width), -8.0 / H_B, F32) * (head0 + hh + 1).astype(F32))
             for hh in heads]

    lane = lax.broadcasted_iota(jnp.int32, (tq, width), 1)
    pos = lax.broadcasted_iota(jnp.int32, (tq, width), 0)
    pos_hi = (pos // POS_SPLIT).astype(F32)
    pos_lo = (pos % POS_SPLIT).astype(F32)
    one = jnp.ones((tq, width), F32)

    def features(base, vals):
        out = jnp.zeros((tq, width), F32)
        for i, val in enumerate(vals):
            out = jnp.where(lane == base + i, val, out)
        return out

    own = [(lane >= c * HD_B) & (lane < (c + 1) * HD_B) for c in range(2)]
    other = [(1 - c) * HD_B for c in range(2)]

    @pl.when(qi == 0)
    def _():
        for c in range(2):
            k_feat = features(other[c], [POS_SPLIT * pos_hi, pos_lo, -POS_SPLIT * one, -one])
            for hh in heads:
                for j in range(nblk):
                    kaug_ref[hh, c, j] = jnp.where(
                        own[c], k_ref[j * tq:(j + 1) * tq, head_cols(hh)],
                        k_feat * slope[hh]).astype(BF16)
        for hh in heads:
            for j in range(nblk):
                vt_ref[hh, j] = jnp.concatenate(
                    [v_ref[j * tq:(j + 1) * tq, head_cols(hh)].T,
                     jnp.ones((ONES_ROWS, tq), F32)], axis=0).astype(BF16)

    q_feat = [features(other[c], [one, one, pos_hi, pos_lo]) for c in range(2)]
    q_aug = [[jnp.where(own[c], q_ref[:, head_cols(hh)] * (HD_B ** -0.5), q_feat[c])
              .astype(BF16) for c in range(2)] for hh in heads]

    m_ref[...] = jnp.full_like(m_ref, NEG)
    acc_ref[...] = jnp.zeros_like(acc_ref)

    def score(kj, s_ref):
        for hh in heads:
            for c in range(2):
                s_ref[hh, c] = lax.dot_general(kaug_ref[hh, c, kj], q_aug[hh][c], nt,
                                               preferred_element_type=F32)

    def col_max(s):
        chunk = s.shape[0] // SUBLANES
        part = s[0:chunk]
        for i in range(1, SUBLANES):
            part = jnp.maximum(part, s[i * chunk:(i + 1) * chunk])
        return jnp.max(part, axis=0, keepdims=True)

    def consume(kj, s_ref, masked):
        for hh in heads:
            block_bias = -slope[hh][:, 0:1] * (tq * (qi - kj)).astype(F32)
            v_t = vt_ref[hh, kj]
            for c in range(2):
                s = s_ref[hh, c]
                if masked:
                    key_i = lax.broadcasted_iota(jnp.int32, (tq, tq), 0)
                    qry_i = lax.broadcasted_iota(jnp.int32, (tq, tq), 1)
                    s = jnp.where(key_i <= qry_i, s, NEG)
                m_old = m_ref[hh, c]
                m_new = jnp.maximum(m_old, col_max(s) + block_bias)
                p = jnp.exp(s - (m_new - block_bias)).astype(BF16)
                acc_ref[hh, c] = (jnp.exp(m_old - m_new) * acc_ref[hh, c]
                                  + jnp.dot(v_t, p, preferred_element_type=F32))
                m_ref[hh, c] = m_new

    score(0, sa_ref)

    def pair(kk, carry):
        score(2 * kk + 1, sb_ref)
        consume(2 * kk, sa_ref, masked=False)
        score(2 * kk + 2, sa_ref)
        consume(2 * kk + 1, sb_ref, masked=False)
        return carry

    lax.fori_loop(0, qi // 2, pair, 0)

    @pl.when(qi % 2 == 1)
    def _():
        score(qi, sb_ref)
        consume(qi - 1, sa_ref, masked=False)
        consume(qi, sb_ref, masked=True)

    @pl.when(qi % 2 == 0)
    def _():
        consume(qi, sa_ref, masked=True)

    lam = _lam_value(lam_ref)
    for hh in heads:
        a0 = acc_ref[hh, 0]
        a1 = acc_ref[hh, 1]
        att_t = a0[0:width] / a0[width:width + 1] - lam * (a1[0:width] / a1[width:width + 1])
        o_ref[:, head_cols(hh)] = _rms(att_t.T, sg_ref[...], SUBLN_EPS) * (1.0 - LAM_INIT)


def _attn_prompt(q, k, v, seq, lam_rows, subln_g):
    m = q.shape[0]
    nb = m // seq
    tq = min(512, seq)
    assert tq % POS_SPLIT == 0 and tq // POS_SPLIT <= 256 and seq % tq == 0
    nq = seq // tq
    width = 2 * HD_B
    nh = ATTN_HEADS_PER_STEP
    q_spec = pl.BlockSpec((tq, nh * width), lambda b, h, qi: (b * nq + qi, h))
    kv_spec = pl.BlockSpec((seq, nh * width), lambda b, h, qi: (b, h))
    return pl.pallas_call(
        _attn_prompt_kernel,
        grid=(nb, H_B // nh, nq),
        in_specs=[q_spec, kv_spec, kv_spec,
                  pl.BlockSpec((4, HD_B), lambda *_: (0, 0)),
                  pl.BlockSpec((1, width), lambda *_: (0, 0))],
        out_specs=q_spec,
        out_shape=jax.ShapeDtypeStruct((m, W_B), F32),
        scratch_shapes=[pltpu.VMEM((nh, 2, nq, tq, width), BF16),
                        pltpu.VMEM((nh, nq, width + ONES_ROWS, tq), BF16),
                        pltpu.VMEM((nh, 2, 1, tq), F32),
                        pltpu.VMEM((nh, 2, width + ONES_ROWS, tq), F32),
                        pltpu.VMEM((nh, 2, tq, tq), F32),
                        pltpu.VMEM((nh, 2, tq, tq), F32)],
        compiler_params=_params(("parallel", "parallel", "arbitrary")),
        name="attn_prompt",
    )(q, k, v, lam_rows, subln_g)


def _attn_sample_kernel(pt_ref, q_ref, kn_ref, vn_ref, lam_ref, sg_ref, *rest,
                        n_pages, past, n_seq):
    del pt_ref
    o_ref, s_ref = rest[2 * n_pages * n_seq], rest[2 * n_pages * n_seq + 1]
    for i in range(n_seq):
        k_pages = rest[i * n_pages:(i + 1) * n_pages]
        v_pages = rest[(n_seq + i) * n_pages:(n_seq + i + 1) * n_pages]
        _attn_sample_one(q_ref.at[i], kn_ref.at[i], vn_ref.at[i], lam_ref, sg_ref,
                         k_pages, v_pages, o_ref.at[i], s_ref.at[i], past)


def _attn_sample_one(q_ref, kn_ref, vn_ref, lam_ref, sg_ref, k_pages, v_pages, o_ref, s_ref,
                     past):
    n_pages = len(k_pages)
    tq = q_ref.shape[0]
    rows = 2 * H_B * tq
    nt = (((1,), (1,)), ((), ()))

    q = q_ref[...] * (HD_B ** -0.5)
    q_rep = jnp.concatenate([q] * (2 * H_B), axis=0)
    r_i = lax.broadcasted_iota(jnp.int32, (rows, W_B), 0)
    c_i = lax.broadcasted_iota(jnp.int32, (rows, W_B), 1)
    q_blk = jnp.where(r_i // tq == c_i // HD_B, q_rep, 0.0).astype(BF16)

    row = lax.broadcasted_iota(jnp.int32, (rows, PAGE_SIZE), 0)
    lane = lax.broadcasted_iota(jnp.int32, (rows, PAGE_SIZE), 1)
    head = row // (2 * tq)
    slope = jnp.exp2(-8.0 * (head + 1).astype(F32) / H_B)
    q_pos = past + row % tq

    def page(ref):
        return jnp.concatenate([ref[pl.ds(h, PAGE_SIZE, stride=H_B), :] for h in range(H_B)],
                               axis=-1).astype(BF16)

    for p in range(n_pages):
        s = lax.dot_general(q_blk, page(k_pages[p]), nt, preferred_element_type=F32)
        dist = q_pos - (p * PAGE_SIZE + lane)
        s_ref[:, p * PAGE_SIZE:(p + 1) * PAGE_SIZE] = s - slope * dist.astype(F32)

    kn = jnp.concatenate(
        [kn_ref[...], jnp.zeros((PAGE_SIZE - tq, W_B), F32)], axis=0).astype(BF16)
    s = lax.dot_general(q_blk, kn, nt, preferred_element_type=F32)
    dist = q_pos - (past + lane)
    s = jnp.where((dist >= 0) & (lane < tq), s - slope * dist.astype(F32), NEG)
    s_ref[:, past:past + PAGE_SIZE] = s

    s = s_ref[...]
    mx = jnp.max(s, axis=-1, keepdims=True)
    p_un = jnp.exp(s - mx)
    pn = p_un / jnp.sum(p_un, axis=-1, keepdims=True)
    lam = _lam_value(lam_ref)
    attn = jnp.concatenate(
        [pn[(2 * h) * tq:(2 * h + 1) * tq] - lam * pn[(2 * h + 1) * tq:(2 * h + 2) * tq]
         for h in range(H_B)], axis=0).astype(BF16)

    acc = jnp.zeros((H_B * tq, W_B), F32)
    for p in range(n_pages):
        acc = acc + jnp.dot(attn[:, p * PAGE_SIZE:(p + 1) * PAGE_SIZE],
                            page(v_pages[p]), preferred_element_type=F32)
    vn = jnp.concatenate(
        [vn_ref[...], jnp.zeros((PAGE_SIZE - tq, W_B), F32)], axis=0).astype(BF16)
    acc = acc + jnp.dot(attn[:, past:past + PAGE_SIZE], vn, preferred_element_type=F32)

    width = 2 * HD_B
    outs = []
    for h in range(H_B):
        att = acc[h * tq:(h + 1) * tq, h * width:(h + 1) * width]
        outs.append(_rms(att, sg_ref[...], SUBLN_EPS) * (1.0 - LAM_INIT))
    o_ref[...] = jnp.concatenate(outs, axis=-1)


def _attn_sample(q, k_new, v_new, cache_k, cache_v, page_table, lam_rows, subln_g):
    bd, tq, _ = q.shape
    n_pages = page_table.shape[1]
    past = n_pages * PAGE_SIZE
    n_seq = SAMPLE_SEQS_PER_STEP if bd % SAMPLE_SEQS_PER_STEP == 0 else 1
    tok_spec = pl.BlockSpec((n_seq, tq, W_B), lambda b, pt: (b, 0, 0))
    page_specs = [pl.BlockSpec((None, PAGE_SIZE * H_B, 2 * HD_B),
                               lambda b, pt, i=i, p=p: (pt[(b * n_seq + i) * n_pages + p], 0, 0))
                  for i in range(n_seq) for p in range(n_pages)]
    grid_spec = pltpu.PrefetchScalarGridSpec(
        num_scalar_prefetch=1,
        grid=(bd // n_seq,),
        in_specs=[tok_spec, tok_spec, tok_spec,
                  pl.BlockSpec((4, HD_B), lambda b, pt: (0, 0)),
                  pl.BlockSpec((1, 2 * HD_B), lambda b, pt: (0, 0))]
                 + page_specs + page_specs,
        out_specs=tok_spec,
        scratch_shapes=[pltpu.VMEM((n_seq, 2 * H_B * tq, past + PAGE_SIZE), F32)],
    )
    return pl.pallas_call(
        functools.partial(_attn_sample_kernel, n_pages=n_pages, past=past, n_seq=n_seq),
        grid_spec=grid_spec,
        out_shape=jax.ShapeDtypeStruct((bd, tq, W_B), F32),
        compiler_params=_params(("parallel",)),
        name="attn_sample",
    )(page_table.reshape(-1), q, k_new, v_new, lam_rows, subln_g,
      *([cache_k] * (n_pages * n_seq)), *([cache_v] * (n_pages * n_seq)))


def _merge_ffn_kernel(x_ref, yn_ref, bv_ref, g_ref, ob_ref, gate_ref, wa_ref, wb_ref, wo_ref,
                      gf_ref, wu_ref, wd_ref, gl_ref, y_ref):
    ga = gate_ref[:, 0:D_MODEL]
    gb = gate_ref[:, D_MODEL:GATE_COLS]
    o_a = (yn_ref[...] + bv_ref[...]) * g_ref[...]
    m = ga * _mm(o_a, wa_ref[...]) + gb * _mm(ob_ref[...], wb_ref[...])
    x1 = x_ref[...] + _mm(m, wo_ref[...])
    hn = _rms(x1, gf_ref[...], NORM_EPS).astype(BF16)
    acc = x1
    chunk = D_MODEL
    for c in range(D_FF // chunk):
        up = jnp.dot(hn, wu_ref[:, c * chunk:(c + 1) * chunk], preferred_element_type=F32)
        act = jnp.square(jnp.maximum(up, 0.0))
        acc = acc + _mm(act, wd_ref[c * chunk:(c + 1) * chunk, :])
    y_ref[...] = _rms(acc, gl_ref[...], NORM_EPS)


def _merge_ffn(x2d, yn, bv, g, o_b, gates, w_br_a, w_br_b, w_out, g_ffn, w_up, w_down,
               g_final):
    m = x2d.shape[0]
    tm = min(512, m)
    row_spec = lambda w: pl.BlockSpec((tm, w), lambda i: (i, 0))
    full = lambda a: _const_spec(a.shape)
    return pl.pallas_call(
        _merge_ffn_kernel,
        grid=(m // tm,),
        in_specs=[row_spec(D_MODEL), row_spec(W_A), row_spec(W_A), row_spec(W_A),
                  row_spec(W_B), row_spec(GATE_COLS),
                  full(w_br_a), full(w_br_b), full(w_out), full(g_ffn), full(w_up),
                  full(w_down), full(g_final)],
        out_specs=row_spec(D_MODEL),
        out_shape=jax.ShapeDtypeStruct((m, D_MODEL), F32),
        compiler_params=_params(("parallel",)),
        name="merge_ffn",
    )(x2d, yn, bv, g, o_b, gates, w_br_a, w_br_b, w_out, g_ffn, w_up, w_down, g_final)


def _rwkv_branch_prompt(pr, params, nb, seq, tc, ln_w, ln_b):
    groups = LANES // (nb * H_A)
    ni = HD_A // groups
    pr3 = pr.reshape(nb, seq, RWKV_COLS)
    scan_view = lambda x: x.reshape(1, seq, x.shape[0] // seq, LANES)
    ga, gb = _pre_scan(pr3, params, tc, 0)
    gk, gr, gv, pc, g, bv = _pre_scan(pr3, params, tc, 1)
    pc = pc.transpose(0, 2, 1, 3).reshape(seq // tc, nb, H_A, HD_A).transpose(0, 3, 1, 2)
    pc = jnp.tile(pc.reshape(1, seq // tc, HD_A, nb * H_A), (1, 1, 1, groups))
    s0 = jnp.zeros((1, ni, HD_A, LANES), F32)
    y, s = _rwkv_scan(scan_view(ga), scan_view(gb), scan_view(gk), scan_view(gr),
                      scan_view(gv), pc, s0, tc=tc)
    y = _from_scan(y[0], nb, ln_w, ln_b).reshape(nb * seq, W_A)
    s = s.reshape(ni, HD_A, groups, nb, H_A).transpose(3, 4, 0, 2, 1)
    flat = lambda x: x.reshape(nb * seq, W_A)
    return y, flat(g), flat(bv), s.reshape(nb, H_A, HD_A, HD_A)


def _rwkv_branch_sample(pr, a_s, b_s, k_s, r_s, pc, state, nb, seq, ln_w, ln_b):
    chains = nb * H_A
    ng = chains // LANES

    def lay(x):
        rows = x.shape[0] // nb
        x = x.reshape(nb, rows, H_A, HD_A).transpose(1, 3, 0, 2).reshape(rows, HD_A, ng, LANES)
        return x.transpose(2, 0, 1, 3)

    s0 = state.transpose(2, 3, 0, 1).reshape(HD_A, HD_A, ng, LANES).transpose(2, 0, 1, 3)
    y, s = _rwkv_scan(lay(a_s), lay(b_s), lay(k_s), lay(r_s), lay(pr[:, 2 * W_A:3 * W_A]),
                      lay(pc), s0, tc=seq)
    y = y.transpose(1, 2, 0, 3).reshape(seq, HD_A, nb, H_A).transpose(2, 0, 3, 1)
    s = s.transpose(1, 2, 0, 3).reshape(HD_A, HD_A, nb, H_A).transpose(2, 3, 0, 1)
    return _group_norm(y.reshape(nb * seq, W_A), ln_w, ln_b), s


def kernel(x_prompt, x_sample, cache_k, cache_v, state_rwkv, state_shift, page_table,
           w_in, mu_shift, w0, w2, a0, a2, g2, k_k, k_a, r_k, ln_x_w, ln_x_b,
           lam_q1, lam_k1, lam_q2, lam_k2, subln_g, w_br_a, w_br_b, w_out,
           g_mix, g_ffn, w_up, w_down, g_final):
    nb, seq, _ = x_prompt.shape
    bd, dseq, _ = x_sample.shape
    w_in_b = w_in[0].astype(BF16)
    wa_b, wb_b, wo_b = (w[0].astype(BF16) for w in (w_br_a, w_br_b, w_out))
    wu_b, wd_b = w_up[0].astype(BF16), w_down[0].astype(BF16)
    r_k2 = r_k.reshape(1, W_A)
    lam_rows = jnp.concatenate([lam_q1, lam_k1, lam_q2, lam_k2], axis=0)
    g_fin = g_final.reshape(1, D_MODEL)

    rwkv_params = (w0, w2[0], a0, a2[0], g2[0], k_k, k_a, r_k2)

    def finish(x2d, yn, bv, g, o_b, gates):
        return _merge_ffn(x2d, yn, bv, g, o_b, gates, wa_b, wb_b, wo_b, g_ffn, wu_b, wd_b,
                          g_fin)

    xp = x_prompt.reshape(nb * seq, D_MODEL)
    pr, q, k, v, kt, vt, gates, x_last = _inproj_prompt(xp, seq, g_mix, w_in_b, mu_shift)
    yn, g, bv, s_prompt = _rwkv_branch_prompt(pr, rwkv_params, nb, seq, SCAN_CHUNK,
                                              ln_x_w, ln_x_b)
    o_b = _attn_prompt(q, k, v, seq, lam_rows, subln_g)
    y_prompt = finish(xp, yn, bv, g, o_b, gates).reshape(nb, seq, D_MODEL)
    k_prompt = kt.reshape(1, nb, seq, H_B, 2 * HD_B)
    v_prompt = vt.reshape(1, nb, seq, H_B, 2 * HD_B)
    shift_prompt = x_last.reshape(1, nb, D_MODEL)

    xs = x_sample.reshape(bd * dseq, D_MODEL)
    shift_pad = jnp.zeros((bd, dseq, D_MODEL), F32).at[:, 0].set(state_shift[0])
    pr, q, k, v, kt, vt, gates, xn = _inproj_sample(
        xs, dseq, shift_pad.reshape(bd * dseq, D_MODEL), g_mix, w_in_b, mu_shift)
    a_s, b_s, k_s, r_s, pc, g, bv = _rwkv_pre(pr, *rwkv_params, dseq)
    yn, s_sample = _rwkv_branch_sample(pr, a_s, b_s, k_s, r_s, pc, state_rwkv[0], bd, dseq,
                                       ln_x_w, ln_x_b)
    tok = lambda t: t.reshape(bd, dseq, W_B)
    pages = lambda c: c[0].reshape(c.shape[1], PAGE_SIZE * H_B, 2 * HD_B)
    o_b = _attn_sample(tok(q), tok(k), tok(v), pages(cache_k), pages(cache_v),
                       page_table, lam_rows, subln_g)
    y_sample = finish(xs, yn, bv, g, o_b.reshape(bd * dseq, W_B), gates)
    y_sample = y_sample.reshape(bd, dseq, D_MODEL)
    k_sample = kt.reshape(1, bd, dseq, H_B, 2 * HD_B)
    v_sample = vt.reshape(1, bd, dseq, H_B, 2 * HD_B)
    shift_sample = xn.reshape(bd, dseq, D_MODEL)[:, -1][None]

    return (y_prompt, y_sample, k_prompt, v_prompt, k_sample, v_sample,
            s_prompt[None], s_sample[None], shift_prompt, shift_sample)
```

```python
import functools

import jax
import jax.numpy as jnp
import numpy as np
from jax import lax
from jax.experimental import pallas as pl
from jax.experimental.pallas import tpu as pltpu

F32 = jnp.float32
BF16 = jnp.bfloat16

D_MODEL = 1024
H_A = 8
HD_A = 64
W_A = H_A * HD_A
DECAY_LORA = 64
AAA_LORA = 64
GATE_LORA = 128
RWKV_COLS = 3 * W_A + DECAY_LORA + AAA_LORA + GATE_LORA
H_B = 4
HD_B = 64
W_B = H_B * 2 * HD_B
GATE_COLS = 2 * D_MODEL
N_COLS = RWKV_COLS + 3 * W_B + GATE_COLS
D_FF = 4 * D_MODEL
PAGE_SIZE = 128
NORM_EPS = 1e-6
GN_EPS = 64e-5
SUBLN_EPS = 1e-5
NEG = -1e30
LAM_INIT = 0.8 - 0.6 * 1.0

LANES = 128
SUBLANES = 8
MXU_TILE = 256
VMEM_LIMIT = 56 * 1024 * 1024
SCAN_CHUNK = 64
SAMPLE_SEQS_PER_STEP = 2
ATTN_HEADS_PER_STEP = 2

OFF_Q = RWKV_COLS
OFF_K = OFF_Q + W_B
OFF_V = OFF_K + W_B
OFF_GA = OFF_V + W_B


def _params(sem):
    return pltpu.CompilerParams(dimension_semantics=sem, vmem_limit_bytes=VMEM_LIMIT)


def _const_spec(shape):
    nd = len(shape)
    return pl.BlockSpec(shape, lambda *_: (0,) * nd, pipeline_mode=pl.Buffered(1))


def _rms(x, g, eps):
    return x * lax.rsqrt(jnp.mean(x * x, axis=-1, keepdims=True) + eps) * g


def _mm(a, w):
    return jnp.dot(a.astype(BF16), w, preferred_element_type=F32)


def _split2(x):
    hi = x.astype(BF16)
    lo = (x - hi.astype(F32)).astype(BF16)
    return hi, lo


def _head_sum(x, head):
    width = x.shape[1]
    tile = min(MXU_TILE, width)
    r = lax.broadcasted_iota(jnp.int32, (tile, tile), 0) // head
    c = lax.broadcasted_iota(jnp.int32, (tile, tile), 1) // head
    ones = (r == c).astype(BF16)
    hi, lo = _split2(x)
    parts = []
    for s in range(width // tile):
        cols = slice(s * tile, (s + 1) * tile)
        parts.append(jnp.dot(hi[:, cols], ones, preferred_element_type=F32)
                     + jnp.dot(lo[:, cols], ones, preferred_element_type=F32))
    return jnp.concatenate(parts, axis=-1)


def _proj_tail(xn, w_ref, q_ref, k_ref, v_ref, kt_ref, vt_ref, gate_ref):
    xb = xn.astype(BF16)
    tm = xn.shape[0]
    q_ref[...] = jnp.dot(xb, w_ref[:, OFF_Q:OFF_K], preferred_element_type=F32)
    k = jnp.dot(xb, w_ref[:, OFF_K:OFF_V], preferred_element_type=F32)
    v = jnp.dot(xb, w_ref[:, OFF_V:OFF_GA], preferred_element_type=F32)
    k_ref[...] = k
    v_ref[...] = v
    for h in range(H_B):
        cols = slice(h * 2 * HD_B, (h + 1) * 2 * HD_B)
        kt_ref[pl.ds(h, tm, stride=H_B), :] = k[:, cols]
        vt_ref[pl.ds(h, tm, stride=H_B), :] = v[:, cols]
    gate_ref[...] = jax.nn.sigmoid(
        jnp.dot(xb, w_ref[:, OFF_GA:N_COLS], preferred_element_type=F32))
    return xb


def _inproj_prompt_kernel(x_ref, g_ref, w_ref, mu_ref,
                          pr_ref, q_ref, k_ref, v_ref, kt_ref, vt_ref, gate_ref, xl_ref,
                          carry_ref):
    tm = x_ref.shape[0]
    xn = _rms(x_ref[...], g_ref[...], NORM_EPS)
    xl_ref[0] = xn[tm - 1:tm, :]
    xb = _proj_tail(xn, w_ref, q_ref, k_ref, v_ref, kt_ref, vt_ref, gate_ref)
    pr = jnp.dot(xb, w_ref[:, 0:RWKV_COLS], preferred_element_type=F32)

    @pl.when(pl.program_id(1) == 0)
    def _():
        carry_ref[...] = jnp.zeros_like(carry_ref)

    prev = pltpu.roll(pr, 1, 0)
    row = lax.broadcasted_iota(jnp.int32, pr.shape, 0)
    prev = jnp.where(row == 0, carry_ref[SUBLANES - 1:SUBLANES, :], prev)
    carry_ref[...] = pr[tm - SUBLANES:tm, :]
    pr_ref[...] = pr + (prev - pr) * mu_ref[...]


def _inproj_prompt(x2d, seq, g_mix, w_in, mu):
    m = x2d.shape[0]
    nb = m // seq
    tm = min(512, seq)
    nt = seq // tm
    row_spec = lambda w: pl.BlockSpec((tm, w), lambda b, t: (b * nt + t, 0))
    head_spec = pl.BlockSpec((tm * H_B, 2 * HD_B), lambda b, t: (b * nt + t, 0))
    return pl.pallas_call(
        _inproj_prompt_kernel,
        grid=(nb, nt),
        in_specs=[row_spec(D_MODEL), _const_spec((1, D_MODEL)),
                  _const_spec((D_MODEL, N_COLS)), _const_spec((1, RWKV_COLS))],
        out_specs=[row_spec(RWKV_COLS), row_spec(W_B), row_spec(W_B), row_spec(W_B),
                   head_spec, head_spec, row_spec(GATE_COLS),
                   pl.BlockSpec((1, 1, D_MODEL), lambda b, t: (b, 0, 0))],
        out_shape=[jax.ShapeDtypeStruct((m, RWKV_COLS), F32),
                   jax.ShapeDtypeStruct((m, W_B), F32),
                   jax.ShapeDtypeStruct((m, W_B), F32),
                   jax.ShapeDtypeStruct((m, W_B), F32),
                   jax.ShapeDtypeStruct((m * H_B, 2 * HD_B), F32),
                   jax.ShapeDtypeStruct((m * H_B, 2 * HD_B), F32),
                   jax.ShapeDtypeStruct((m, GATE_COLS), F32),
                   jax.ShapeDtypeStruct((nb, 1, D_MODEL), F32)],
        scratch_shapes=[pltpu.VMEM((SUBLANES, RWKV_COLS), F32)],
        compiler_params=_params(("parallel", "arbitrary")),
        name="inproj_prompt",
    )(x2d, g_mix, w_in, mu)


def _inproj_sample_kernel(x_ref, sp_ref, g_ref, w_ref, mu_ref,
                          pr_ref, q_ref, k_ref, v_ref, kt_ref, vt_ref, gate_ref, xn_ref, *,
                          seq):
    xn = _rms(x_ref[...], g_ref[...], NORM_EPS)
    xn_ref[...] = xn
    xb = _proj_tail(xn, w_ref, q_ref, k_ref, v_ref, kt_ref, vt_ref, gate_ref)
    row = lax.broadcasted_iota(jnp.int32, xn.shape, 0)
    xprev = jnp.where(row % seq == 0, sp_ref[...], pltpu.roll(xn, 1, 0))
    w_r = w_ref[:, 0:RWKV_COLS]
    pr = jnp.dot(xb, w_r, preferred_element_type=F32)
    prev = jnp.dot(xprev.astype(BF16), w_r, preferred_element_type=F32)
    pr_ref[...] = pr + (prev - pr) * mu_ref[...]


def _inproj_sample(x2d, seq, shift_pad, g_mix, w_in, mu):
    m = x2d.shape[0]
    tm = min(256, m)
    assert tm % seq == 0 and m % tm == 0
    row_spec = lambda w: pl.BlockSpec((tm, w), lambda i: (i, 0))
    head_spec = pl.BlockSpec((tm * H_B, 2 * HD_B), lambda i: (i, 0))
    return pl.pallas_call(
        functools.partial(_inproj_sample_kernel, seq=seq),
        grid=(m // tm,),
        in_specs=[row_spec(D_MODEL), row_spec(D_MODEL), _const_spec((1, D_MODEL)),
                  _const_spec((D_MODEL, N_COLS)), _const_spec((1, RWKV_COLS))],
        out_specs=[row_spec(RWKV_COLS), row_spec(W_B), row_spec(W_B), row_spec(W_B),
                   head_spec, head_spec, row_spec(GATE_COLS), row_spec(D_MODEL)],
        out_shape=[jax.ShapeDtypeStruct((m, RWKV_COLS), F32),
                   jax.ShapeDtypeStruct((m, W_B), F32),
                   jax.ShapeDtypeStruct((m, W_B), F32),
                   jax.ShapeDtypeStruct((m, W_B), F32),
                   jax.ShapeDtypeStruct((m * H_B, 2 * HD_B), F32),
                   jax.ShapeDtypeStruct((m * H_B, 2 * HD_B), F32),
                   jax.ShapeDtypeStruct((m, GATE_COLS), F32),
                   jax.ShapeDtypeStruct((m, D_MODEL), F32)],
        compiler_params=_params(("parallel",)),
        name="inproj_sample",
    )(x2d, shift_pad, g_mix, w_in, mu)


def _dot_f32(a, b):
    a_hi, a_lo = _split2(a)
    b_hi, b_lo = _split2(b)
    d = lambda x, y: jnp.dot(x, y, preferred_element_type=F32)
    return d(a_hi, b_hi) + d(a_lo, b_hi) + d(a_hi, b_lo)


def _pre_math(pr, params, p_scr, chunk):
    w0, w2, a0, a2, g2, k_k, k_a, r_k = params
    tm = pr.shape[0]
    o_wd = 3 * W_A
    o_ad = o_wd + DECAY_LORA
    o_gd = o_ad + AAA_LORA
    k = pr[:, W_A:2 * W_A]
    wd = pr[:, o_wd:o_ad]
    ad = pr[:, o_ad:o_gd]
    gd = pr[:, o_gd:RWKV_COLS]
    z = -(w0 + _dot_f32(jnp.tanh(wd), w2))
    softplus = jnp.maximum(z, 0.0) + jnp.log1p(jnp.exp(-jnp.abs(z)))
    w = -softplus - 0.5
    log_decay = -jnp.exp(w)
    r_i = lax.broadcasted_iota(jnp.int32, (tm, tm), 0)
    c_i = lax.broadcasted_iota(jnp.int32, (tm, tm), 1)
    tri = ((r_i // chunk == c_i // chunk) & (c_i <= r_i)).astype(BF16)
    hi = log_decay.astype(BF16)
    rest = log_decay - hi.astype(F32)
    mid = rest.astype(BF16)
    lo = (rest - mid.astype(F32)).astype(BF16)
    run = lambda t: jnp.dot(tri, t, preferred_element_type=F32)
    cum = run(hi) + run(mid) + run(lo)
    p_incl = jnp.exp(cum)
    p_inv = jnp.exp(-cum)
    p_prev = jnp.exp(cum - log_decay)
    pc = []
    for s in range(W_A // LANES):
        p_scr[s] = p_incl[:, s * LANES:(s + 1) * LANES]
        pc.append(p_scr[s, pl.ds(chunk - 1, tm // chunk, stride=chunk), :])
    pc = jnp.concatenate(pc, axis=-1)
    a = jax.nn.sigmoid(a0 + _dot_f32(ad, a2))
    g = _dot_f32(jax.nn.sigmoid(gd), g2)
    kk = k * k_k
    norm = jnp.sqrt(_head_sum(kk * kk, HD_A))
    kk = kk / jnp.maximum(norm, 1e-12)
    k_mod = k * (1.0 + (a - 1.0) * k_a)
    r = pr[:, 0:W_A]
    v = pr[:, 2 * W_A:3 * W_A]
    bv = _head_sum(r * k_mod * r_k, HD_A) * v
    return -kk * p_prev, kk * a * p_inv, k_mod * p_inv, r * p_incl, pc, g, bv


def _rwkv_pre_kernel(pr_ref, *refs, chunk):
    params = tuple(ref[...] for ref in refs[:8])
    outs, p_scr = refs[8:15], refs[15]
    for out_ref, val in zip(outs, _pre_math(pr_ref[...], params, p_scr, chunk)):
        out_ref[...] = val


def _rwkv_pre(pr, w0, w2, a0, a2, g2, k_k, k_a, r_k, chunk):
    m = pr.shape[0]
    tm = min(256, m)
    assert tm % chunk == 0 and (tm // chunk) % SUBLANES == 0
    row_spec = lambda w: pl.BlockSpec((tm, w), lambda i: (i, 0))
    full = lambda a: _const_spec(a.shape)
    out = jax.ShapeDtypeStruct((m, W_A), F32)
    return pl.pallas_call(
        functools.partial(_rwkv_pre_kernel, chunk=chunk),
        grid=(m // tm,),
        in_specs=[row_spec(RWKV_COLS), full(w0), full(w2), full(a0), full(a2), full(g2),
                  full(k_k), full(k_a), full(r_k)],
        out_specs=[row_spec(W_A)] * 4 + [pl.BlockSpec((tm // chunk, W_A), lambda i: (i, 0))]
                  + [row_spec(W_A)] * 2,
        out_shape=[out] * 4 + [jax.ShapeDtypeStruct((m // chunk, W_A), F32)] + [out] * 2,
        scratch_shapes=[pltpu.VMEM((W_A // LANES, tm, LANES), F32)],
        compiler_params=_params(("parallel",)),
        name="rwkv_pre",
    )(pr, w0, w2, a0, a2, g2, k_k, k_a, r_k)


def _sublane_allsum(x):
    n = x.shape[0] // SUBLANES
    acc = x[0:SUBLANES]
    for g in range(1, n):
        acc = acc + x[g * SUBLANES:(g + 1) * SUBLANES]
    shift = SUBLANES // 2
    while shift >= 1:
        acc = acc + pltpu.roll(acc, shift, 0)
        shift //= 2
    return acc


def _sublane_groupsum(x):
    acc = x[0:SUBLANES]
    for g in range(1, x.shape[0] // SUBLANES):
        acc = acc + x[g * SUBLANES:(g + 1) * SUBLANES]
    return acc


def _fold_rows(parts, roll, where, sub):
    level = list(parts)
    shift = SUBLANES // 2
    first = True
    while len(level) > 1:
        keep = (sub & shift) != 0
        nxt = []
        for a, b in zip(level[0::2], level[1::2]):
            a2 = a + roll(a, shift)
            b2 = b + roll(b, shift)
            nxt.append(where(keep, a2, b2 if first else roll(b2, shift)))
        level, shift, first = nxt, shift // 2, False
    return level[0]


def _fold_slots():
    tags = [np.tile(np.eye(SUBLANES)[k] / SUBLANES, (SUBLANES, 1)) for k in range(SUBLANES)]
    sub = np.arange(SUBLANES)[:, None]
    out = _fold_rows(tags, lambda x, s: np.roll(x, s, axis=0), np.where, sub)
    assert np.array_equal(np.sort(out.argmax(1)), np.arange(SUBLANES))
    assert np.allclose(out.max(1), 1.0)
    return tuple(int(k) for k in out.argmax(1))


def _rwkv_scan_kernel(a_ref, b_ref, k_ref, r_ref, v_ref, pc_ref, s0_ref, y_ref, s_ref):
    tc = a_ref.shape[0]
    ni, nkey = s0_ref.shape[0], s0_ref.shape[1]
    ng = nkey // SUBLANES
    pad_rows = y_ref.shape[1] - ni

    @pl.when(pl.program_id(1) == 0)
    def _():
        s_ref[...] = s0_ref[...]

    sub = lax.broadcasted_iota(jnp.int32, (SUBLANES, LANES), 0)
    keys = pl.ds(0, nkey)
    slots = _fold_slots()

    def tile8(x):
        return jnp.concatenate([x] * ng, axis=0)

    def step(t, carry):
        def group(ig, carry2):
            y_parts = [None] * SUBLANES
            for ii in range(SUBLANES):
                i = ig * SUBLANES + ii
                s_i = s_ref[i]
                sa = _sublane_allsum(s_i * a_ref[t, keys])
                v_i = jnp.broadcast_to(v_ref[t, pl.ds(i, 1), :], (SUBLANES, LANES))
                s_new = s_i + tile8(sa) * b_ref[t, keys] + tile8(v_i) * k_ref[t, keys]
                s_ref[i] = s_new
                y_parts[slots[ii]] = _sublane_groupsum(s_new * r_ref[t, keys])
            y_acc = _fold_rows(y_parts, lambda x, s: pltpu.roll(x, s, 0), jnp.where, sub)
            y_ref[t, pl.ds(pl.multiple_of(ig * SUBLANES, SUBLANES), SUBLANES), :] = y_acc
            return carry2

        n_groups = ni // SUBLANES
        if n_groups <= 2:
            for ig in range(n_groups):
                group(ig, 0)
        else:
            lax.fori_loop(0, n_groups, group, 0)
        if pad_rows:
            y_ref[t, pl.ds(ni, pad_rows), :] = jnp.zeros((pad_rows, LANES), F32)
        return carry

    lax.fori_loop(0, tc, step, 0, unroll=2)

    def rescale(i, carry):
        s_ref[i] = s_ref[i] * pc_ref[0, keys]
        return carry

    lax.fori_loop(0, ni, rescale, 0, unroll=8)


def _rwkv_scan(a, b, k, r, v, pc, s0, tc):
    g, t, pk, _ = a.shape
    pv = v.shape[2]
    _, ni, nkey, _ = s0.shape
    x_spec = pl.BlockSpec((None, tc, pk, LANES), lambda gi, ti: (gi, ti, 0, 0))
    v_spec = pl.BlockSpec((None, tc, pv, LANES), lambda gi, ti: (gi, ti, 0, 0))
    s_spec = pl.BlockSpec((None, ni, nkey, LANES), lambda gi, ti: (gi, 0, 0, 0))
    pc_spec = pl.BlockSpec((None, 1, pc.shape[2], LANES), lambda gi, ti: (gi, ti, 0, 0))
    return pl.pallas_call(
        _rwkv_scan_kernel,
        grid=(g, t // tc),
        in_specs=[x_spec] * 4 + [v_spec, pc_spec, s_spec],
        out_specs=[v_spec, s_spec],
        out_shape=[jax.ShapeDtypeStruct((g, t, pv, LANES), F32),
                   jax.ShapeDtypeStruct((g, ni, nkey, LANES), F32)],
        compiler_params=_params(("parallel", "arbitrary")),
        name="rwkv_scan",
    )(a, b, k, r, v, pc, s0)


KEY_PITCH = HD_A + SUBLANES
TIME_TILE = LANES


def _val_pitch(n_rows):
    p = n_rows + SUBLANES
    return p if (p // SUBLANES) % 2 else p + SUBLANES


def _scan_round1(tile, nb, c_scr):
    heads_per_tile = LANES // HD_A
    for b in range(nb):
        for hp in range(W_A // LANES):
            tile_t = tile(b, hp).T
            for h2 in range(heads_per_tile):
                c = b * H_A + hp * heads_per_tile + h2
                c_scr[c * KEY_PITCH:c * KEY_PITCH + HD_A, :] = tile_t[h2 * HD_A:(h2 + 1) * HD_A]


def _scan_round2(c_scr, nb, g_ref, value_indexed, pitch):
    tt = c_scr.shape[1]
    chains = nb * H_A
    groups = LANES // chains
    chain_rows = lambda j: c_scr[pl.ds(j, chains, stride=KEY_PITCH), :]
    n_out = HD_A // groups if value_indexed else HD_A
    for n in range(n_out):
        if value_indexed:
            rows = [chain_rows(n * groups + r) for r in range(groups)]
        else:
            rows = [chain_rows(n)] * groups
        g_ref[pl.ds(n, tt, stride=pitch), :] = jnp.concatenate(rows, axis=0).T
    for n in range(n_out, pitch):
        g_ref[pl.ds(n, tt, stride=pitch), :] = jnp.zeros((tt, LANES), F32)


def _pre_scan_kernel(pr_ref, *refs, chunk, part, n_tiles):
    nb, tt, _ = pr_ref.shape
    params = tuple(ref[...] for ref in refs[:8])
    n_ops = 2 if part == 0 else 3
    outs = refs[8:-(2 * n_ops + 1)]
    sets = (refs[-(2 * n_ops + 1):-(n_ops + 1)], refs[-(n_ops + 1):-1])
    p_scr = refs[-1]
    i = pl.program_id(0)
    vp = outs[2].shape[0] // tt if part == 1 else None
    lanes = lambda hp: slice(hp * LANES, (hp + 1) * LANES)

    @pl.when(i == 0)
    def _():
        for c_scr in sets[1]:
            c_scr[...] = jnp.zeros_like(c_scr)

    def body(done, fresh):
        for idx in range(2):
            _scan_round2(done[idx], nb, outs[idx], False, KEY_PITCH)
        if part == 1:
            _scan_round2(done[2], nb, outs[2], True, vp)
        vals = [_pre_math(pr_ref[b], params, p_scr, chunk) for b in range(nb)]
        first = 0 if part == 0 else 2
        for idx in range(2):
            _scan_round1(lambda b, hp: vals[b][first + idx][:, lanes(hp)], nb, fresh[idx])
        if part == 1:
            v_cols = lambda hp: slice(2 * W_A + hp * LANES, 2 * W_A + (hp + 1) * LANES)
            _scan_round1(lambda b, hp: pr_ref[b, :, v_cols(hp)], nb, fresh[2])
            pc_ref, g_ref, bv_ref = outs[3:6]
            for b in range(nb):
                pc_ref[b] = vals[b][4]
                g_ref[b] = vals[b][5]
                bv_ref[b] = vals[b][6]

    @pl.when(i % 2 == 0)
    def _():
        body(sets[1], sets[0])

    @pl.when(i % 2 == 1)
    def _():
        body(sets[0], sets[1])


def _pre_scan(pr3, params, chunk, part):
    nb, t, _ = pr3.shape
    groups = LANES // (nb * H_A)
    vp = _val_pitch(HD_A // groups)
    n_tiles = t // TIME_TILE
    cpt = TIME_TILE // chunk
    this = lambda i: jnp.minimum(i, n_tiles - 1)
    prev = lambda i: jnp.maximum(i - 1, 0)
    g_spec = lambda pitch: pl.BlockSpec((TIME_TILE * pitch, LANES), lambda i: (prev(i), 0))
    g_shape = lambda pitch: jax.ShapeDtypeStruct((t * pitch, LANES), F32)
    nat_spec = pl.BlockSpec((nb, TIME_TILE, W_A), lambda i: (0, this(i), 0))
    nat_shape = jax.ShapeDtypeStruct((nb, t, W_A), F32)
    if part == 0:
        out_specs = [g_spec(KEY_PITCH)] * 2
        out_shape = [g_shape(KEY_PITCH)] * 2
    else:
        out_specs = [g_spec(KEY_PITCH)] * 2 + [
            g_spec(vp), pl.BlockSpec((None, nb, cpt, W_A), lambda i: (this(i), 0, 0, 0)),
            nat_spec, nat_spec]
        out_shape = [g_shape(KEY_PITCH)] * 2 + [
            g_shape(vp), jax.ShapeDtypeStruct((n_tiles, nb, cpt, W_A), F32),
            nat_shape, nat_shape]
    n_ops = 2 if part == 0 else 3
    return pl.pallas_call(
        functools.partial(_pre_scan_kernel, chunk=chunk, part=part, n_tiles=n_tiles),
        grid=(n_tiles + 1,),
        in_specs=[pl.BlockSpec((nb, TIME_TILE, RWKV_COLS), lambda i: (0, this(i), 0))]
                 + [_const_spec(p.shape) for p in params],
        out_specs=out_specs,
        out_shape=out_shape,
        scratch_shapes=[pltpu.VMEM((nb * H_A * KEY_PITCH, TIME_TILE), F32)] * (2 * n_ops)
                       + [pltpu.VMEM((W_A // LANES, TIME_TILE, LANES), F32)],
        compiler_params=_params(("arbitrary",)),
        name="pre_scan",
    )(pr3, *params)


def _from_scan_kernel(y_ref, lnw_ref, lnb_ref, o_ref, c_scr, *, pitch):
    nb, tt, _ = o_ref.shape
    chains = nb * H_A
    groups = LANES // chains
    heads_per_tile = LANES // HD_A
    for n in range(HD_A // groups):
        lanes_t = y_ref[pl.ds(n, tt, stride=pitch), :].T
        for r in range(groups):
            c_scr[pl.ds(n * groups + r, chains, stride=KEY_PITCH), :] = (
                lanes_t[r * chains:(r + 1) * chains])
    for c in range(chains):
        rows = pl.ds(c * KEY_PITCH, HD_A)
        head = c % H_A
        y = c_scr[rows, :]
        mean = _sublane_allsum(y) * (1.0 / HD_A)
        d = y - jnp.concatenate([mean] * (HD_A // SUBLANES), axis=0)
        var = _sublane_allsum(d * d) * (1.0 / HD_A)
        inv = jnp.concatenate([lax.rsqrt(var + GN_EPS)] * (HD_A // SUBLANES), axis=0)
        c_scr[rows, :] = (d * inv * lnw_ref[head * HD_A:(head + 1) * HD_A, :]
                          + lnb_ref[head * HD_A:(head + 1) * HD_A, :])
    for b in range(nb):
        for hp in range(W_A // LANES):
            c0 = b * H_A + hp * heads_per_tile
            tile_t = jnp.concatenate(
                [c_scr[(c0 + h2) * KEY_PITCH:(c0 + h2) * KEY_PITCH + HD_A, :]
                 for h2 in range(heads_per_tile)], axis=0)
            o_ref[b, :, hp * LANES:(hp + 1) * LANES] = tile_t.T


def _from_scan(y, nb, ln_w, ln_b):
    t, pitch, _ = y.shape
    cols = lambda p: jnp.broadcast_to(p.reshape(W_A, 1), (W_A, TIME_TILE))
    return pl.pallas_call(
        functools.partial(_from_scan_kernel, pitch=pitch),
        grid=(t // TIME_TILE,),
        in_specs=[pl.BlockSpec((TIME_TILE * pitch, LANES), lambda i: (i, 0)),
                  _const_spec((W_A, TIME_TILE)), _const_spec((W_A, TIME_TILE))],
        out_specs=pl.BlockSpec((nb, TIME_TILE, W_A), lambda i: (0, i, 0)),
        out_shape=jax.ShapeDtypeStruct((nb, t, W_A), F32),
        scratch_shapes=[pltpu.VMEM((nb * H_A * KEY_PITCH, TIME_TILE), F32)],
        compiler_params=_params(("parallel",)),
        name="from_scan",
    )(y.reshape(t * pitch, LANES), cols(ln_w), cols(ln_b))


def _group_norm_kernel(y_ref, lnw_ref, lnb_ref, o_ref):
    y = y_ref[...]
    mean = _head_sum(y, HD_A) * (1.0 / HD_A)
    d = y - mean
    var = _head_sum(d * d, HD_A) * (1.0 / HD_A)
    o_ref[...] = d * lax.rsqrt(var + GN_EPS) * lnw_ref[...] + lnb_ref[...]


def _group_norm(y, ln_w, ln_b):
    m = y.shape[0]
    tm = min(256, m)
    row_spec = pl.BlockSpec((tm, W_A), lambda i: (i, 0))
    vec = _const_spec((1, W_A))
    return pl.pallas_call(
        _group_norm_kernel,
        grid=(m // tm,),
        in_specs=[row_spec, vec, vec],
        out_specs=row_spec,
        out_shape=jax.ShapeDtypeStruct((m, W_A), F32),
        compiler_params=_params(("parallel",)),
        name="group_norm",
    )(y, ln_w, ln_b)


def _lam_value(lam_ref):
    lv = lam_ref[...]
    s1 = jnp.sum(lv[0:1] * lv[1:2], axis=-1, keepdims=True)
    s2 = jnp.sum(lv[2:3] * lv[3:4], axis=-1, keepdims=True)
    return jnp.exp(s1) - jnp.exp(s2) + LAM_INIT


POS_SPLIT = 16
ONES_ROWS = 16


def _attn_prompt_kernel(q_ref, k_ref, v_ref, lam_ref, sg_ref, o_ref,
                        kaug_ref, vt_ref, m_ref, acc_ref, sa_ref, sb_ref):
    qi = pl.program_id(2)
    tq = q_ref.shape[0]
    nblk = k_ref.shape[0] // tq
    width = 2 * HD_B
    n_heads = q_ref.shape[1] // width
    heads = range(n_heads)
    head_cols = lambda hh: slice(hh * width, (hh + 1) * width)
    nt = (((1,), (1,)), ((), ()))
    head0 = pl.program_id(1) * n_heads
    slope = [jnp.exp2(jnp.full((1, width), -8.0 / H_B, F32) * (head0 + hh + 1).astype(F32))
             for hh in heads]

    lane = lax.broadcasted_iota(jnp.int32, (tq, width), 1)
    pos = lax.broadcasted_iota(jnp.int32, (tq, width), 0)
    pos_hi = (pos // POS_SPLIT).astype(F32)
    pos_lo = (pos % POS_SPLIT).astype(F32)
    one = jnp.ones((tq, width), F32)

    def features(base, vals):
        out = jnp.zeros((tq, width), F32)
        for i, val in enumerate(vals):
            out = jnp.where(lane == base + i, val, out)
        return out

    own = [(lane >= c * HD_B) & (lane < (c + 1) * HD_B) for c in range(2)]
    other = [(1 - c) * HD_B for c in range(2)]

    @pl.when(qi == 0)
    def _():
        for c in range(2):
            k_feat = features(other[c], [POS_SPLIT * pos_hi, pos_lo, -POS_SPLIT * one, -one])
            for hh in heads:
                for j in range(nblk):
                    kaug_ref[hh, c, j] = jnp.where(
                        own[c], k_ref[j * tq:(j + 1) * tq, head_cols(hh)],
                        k_feat * slope[hh]).astype(BF16)
        for hh in heads:
            for j in range(nblk):
                vt_ref[hh, j] = jnp.concatenate(
                    [v_ref[j * tq:(j + 1) * tq, head_cols(hh)].T,
                     jnp.ones((ONES_ROWS, tq), F32)], axis=0).astype(BF16)

    q_feat = [features(other[c], [one, one, pos_hi, pos_lo]) for c in range(2)]
    q_aug = [[jnp.where(own[c], q_ref[:, head_cols(hh)] * (HD_B ** -0.5), q_feat[c])
              .astype(BF16) for c in range(2)] for hh in heads]

    m_ref[...] = jnp.full_like(m_ref, NEG)
    acc_ref[...] = jnp.zeros_like(acc_ref)

    def score(kj, s_ref):
        for hh in heads:
            for c in range(2):
                s_ref[hh, c] = lax.dot_general(kaug_ref[hh, c, kj], q_aug[hh][c], nt,
                                               preferred_element_type=F32)

    def col_max(s):
        chunk = s.shape[0] // SUBLANES
        part = s[0:chunk]
        for i in range(1, SUBLANES):
            part = jnp.maximum(part, s[i * chunk:(i + 1) * chunk])
        return jnp.max(part, axis=0, keepdims=True)

    def consume(kj, s_ref, masked):
        for hh in heads:
            block_bias = -slope[hh][:, 0:1] * (tq * (qi - kj)).astype(F32)
            v_t = vt_ref[hh, kj]
            for c in range(2):
                s = s_ref[hh, c]
                if masked:
                    key_i = lax.broadcasted_iota(jnp.int32, (tq, tq), 0)
                    qry_i = lax.broadcasted_iota(jnp.int32, (tq, tq), 1)
                    s = jnp.where(key_i <= qry_i, s, NEG)
                m_old = m_ref[hh, c]
                m_new = jnp.maximum(m_old, col_max(s) + block_bias)
                p = jnp.exp(s - (m_new - block_bias)).astype(BF16)
                acc_ref[hh, c] = (jnp.exp(m_old - m_new) * acc_ref[hh, c]
                                  + jnp.dot(v_t, p, preferred_element_type=F32))
                m_ref[hh, c] = m_new

    score(0, sa_ref)

    def pair(kk, carry):
        score(2 * kk + 1, sb_ref)
        consume(2 * kk, sa_ref, masked=False)
        score(2 * kk + 2, sa_ref)
        consume(2 * kk + 1, sb_ref, masked=False)
        return carry

    lax.fori_loop(0, qi // 2, pair, 0)

    @pl.when(qi % 2 == 1)
    def _():
        score(qi, sb_ref)
        consume(qi - 1, sa_ref, masked=False)
        consume(qi, sb_ref, masked=True)

    @pl.when(qi % 2 == 0)
    def _():
        consume(qi, sa_ref, masked=True)

    lam = _lam_value(lam_ref)
    for hh in heads:
        a0 = acc_ref[hh, 0]
        a1 = acc_ref[hh, 1]
        att_t = a0[0:width] / a0[width:width + 1] - lam * (a1[0:width] / a1[width:width + 1])
        o_ref[:, head_cols(hh)] = _rms(att_t.T, sg_ref[...], SUBLN_EPS) * (1.0 - LAM_INIT)


def _attn_prompt(q, k, v, seq, lam_rows, subln_g):
    m = q.shape[0]
    nb = m // seq
    tq = min(512, seq)
    assert tq % POS_SPLIT == 0 and tq // POS_SPLIT <= 256 and seq % tq == 0
    nq = seq // tq
    width = 2 * HD_B
    nh = ATTN_HEADS_PER_STEP
    q_spec = pl.BlockSpec((tq, nh * width), lambda b, h, qi: (b * nq + qi, h))
    kv_spec = pl.BlockSpec((seq, nh * width), lambda b, h, qi: (b, h))
    return pl.pallas_call(
        _attn_prompt_kernel,
        grid=(nb, H_B // nh, nq),
        in_specs=[q_spec, kv_spec, kv_spec,
                  pl.BlockSpec((4, HD_B), lambda *_: (0, 0)),
                  pl.BlockSpec((1, width), lambda *_: (0, 0))],
        out_specs=q_spec,
        out_shape=jax.ShapeDtypeStruct((m, W_B), F32),
        scratch_shapes=[pltpu.VMEM((nh, 2, nq, tq, width), BF16),
                        pltpu.VMEM((nh, nq, width + ONES_ROWS, tq), BF16),
                        pltpu.VMEM((nh, 2, 1, tq), F32),
                        pltpu.VMEM((nh, 2, width + ONES_ROWS, tq), F32),
                        pltpu.VMEM((nh, 2, tq, tq), F32),
                        pltpu.VMEM((nh, 2, tq, tq), F32)],
        compiler_params=_params(("parallel", "parallel", "arbitrary")),
        name="attn_prompt",
    )(q, k, v, lam_rows, subln_g)


def _attn_sample_kernel(pt_ref, q_ref, kn_ref, vn_ref, lam_ref, sg_ref, *rest,
                        n_pages, past, n_seq):
    del pt_ref
    o_ref, s_ref = rest[2 * n_pages * n_seq], rest[2 * n_pages * n_seq + 1]
    for i in range(n_seq):
        k_pages = rest[i * n_pages:(i + 1) * n_pages]
        v_pages = rest[(n_seq + i) * n_pages:(n_seq + i + 1) * n_pages]
        _attn_sample_one(q_ref.at[i], kn_ref.at[i], vn_ref.at[i], lam_ref, sg_ref,
                         k_pages, v_pages, o_ref.at[i], s_ref.at[i], past)


def _attn_sample_one(q_ref, kn_ref, vn_ref, lam_ref, sg_ref, k_pages, v_pages, o_ref, s_ref,
                     past):
    n_pages = len(k_pages)
    tq = q_ref.shape[0]
    rows = 2 * H_B * tq
    nt = (((1,), (1,)), ((), ()))

    q = q_ref[...] * (HD_B ** -0.5)
    q_rep = jnp.concatenate([q] * (2 * H_B), axis=0)
    r_i = lax.broadcasted_iota(jnp.int32, (rows, W_B), 0)
    c_i = lax.broadcasted_iota(jnp.int32, (rows, W_B), 1)
    q_blk = jnp.where(r_i // tq == c_i // HD_B, q_rep, 0.0).astype(BF16)

    row = lax.broadcasted_iota(jnp.int32, (rows, PAGE_SIZE), 0)
    lane = lax.broadcasted_iota(jnp.int32, (rows, PAGE_SIZE), 1)
    head = row // (2 * tq)
    slope = jnp.exp2(-8.0 * (head + 1).astype(F32) / H_B)
    q_pos = past + row % tq

    def page(ref):
        return jnp.concatenate([ref[pl.ds(h, PAGE_SIZE, stride=H_B), :] for h in range(H_B)],
                               axis=-1).astype(BF16)

    for p in range(n_pages):
        s = lax.dot_general(q_blk, page(k_pages[p]), nt, preferred_element_type=F32)
        dist = q_pos - (p * PAGE_SIZE + lane)
        s_ref[:, p * PAGE_SIZE:(p + 1) * PAGE_SIZE] = s - slope * dist.astype(F32)

    kn = jnp.concatenate(
        [kn_ref[...], jnp.zeros((PAGE_SIZE - tq, W_B), F32)], axis=0).astype(BF16)
    s = lax.dot_general(q_blk, kn, nt, preferred_element_type=F32)
    dist = q_pos - (past + lane)
    s = jnp.where((dist >= 0) & (lane < tq), s - slope * dist.astype(F32), NEG)
    s_ref[:, past:past + PAGE_SIZE] = s

    s = s_ref[...]
    mx = jnp.max(s, axis=-1, keepdims=True)
    p_un = jnp.exp(s - mx)
    pn = p_un / jnp.sum(p_un, axis=-1, keepdims=True)
    lam = _lam_value(lam_ref)
    attn = jnp.concatenate(
        [pn[(2 * h) * tq:(2 * h + 1) * tq] - lam * pn[(2 * h + 1) * tq:(2 * h + 2) * tq]
         for h in range(H_B)], axis=0).astype(BF16)

    acc = jnp.zeros((H_B * tq, W_B), F32)
    for p in range(n_pages):
        acc = acc + jnp.dot(attn[:, p * PAGE_SIZE:(p + 1) * PAGE_SIZE],
                            page(v_pages[p]), preferred_element_type=F32)
    vn = jnp.concatenate(
        [vn_ref[...], jnp.zeros((PAGE_SIZE - tq, W_B), F32)], axis=0).astype(BF16)
    acc = acc + jnp.dot(attn[:, past:past + PAGE_SIZE], vn, preferred_element_type=F32)

    width = 2 * HD_B
    outs = []
    for h in range(H_B):
        att = acc[h * tq:(h + 1) * tq, h * width:(h + 1) * width]
        outs.append(_rms(att, sg_ref[...], SUBLN_EPS) * (1.0 - LAM_INIT))
    o_ref[...] = jnp.concatenate(outs, axis=-1)


def _attn_sample(q, k_new, v_new, cache_k, cache_v, page_table, lam_rows, subln_g):
    bd, tq, _ = q.shape
    n_pages = page_table.shape[1]
    past = n_pages * PAGE_SIZE
    n_seq = SAMPLE_SEQS_PER_STEP if bd % SAMPLE_SEQS_PER_STEP == 0 else 1
    tok_spec = pl.BlockSpec((n_seq, tq, W_B), lambda b, pt: (b, 0, 0))
    page_specs = [pl.BlockSpec((None, PAGE_SIZE * H_B, 2 * HD_B),
                               lambda b, pt, i=i, p=p: (pt[(b * n_seq + i) * n_pages + p], 0, 0))
                  for i in range(n_seq) for p in range(n_pages)]
    grid_spec = pltpu.PrefetchScalarGridSpec(
        num_scalar_prefetch=1,
        grid=(bd // n_seq,),
        in_specs=[tok_spec, tok_spec, tok_spec,
                  pl.BlockSpec((4, HD_B), lambda b, pt: (0, 0)),
                  pl.BlockSpec((1, 2 * HD_B), lambda b, pt: (0, 0))]
                 + page_specs + page_specs,
        out_specs=tok_spec,
        scratch_shapes=[pltpu.VMEM((n_seq, 2 * H_B * tq, past + PAGE_SIZE), F32)],
    )
    return pl.pallas_call(
        functools.partial(_attn_sample_kernel, n_pages=n_pages, past=past, n_seq=n_seq),
        grid_spec=grid_spec,
        out_shape=jax.ShapeDtypeStruct((bd, tq, W_B), F32),
        compiler_params=_params(("parallel",)),
        name="attn_sample",
    )(page_table.reshape(-1), q, k_new, v_new, lam_rows, subln_g,
      *([cache_k] * (n_pages * n_seq)), *([cache_v] * (n_pages * n_seq)))


def _merge_ffn_kernel(x_ref, yn_ref, bv_ref, g_ref, ob_ref, gate_ref, wa_ref, wb_ref, wo_ref,
                      gf_ref, wu_ref, wd_ref, gl_ref, y_ref):
    ga = gate_ref[:, 0:D_MODEL]
    gb = gate_ref[:, D_MODEL:GATE_COLS]
    o_a = (yn_ref[...] + bv_ref[...]) * g_ref[...]
    m = ga * _mm(o_a, wa_ref[...]) + gb * _mm(ob_ref[...], wb_ref[...])
    x1 = x_ref[...] + _mm(m, wo_ref[...])
    hn = _rms(x1, gf_ref[...], NORM_EPS).astype(BF16)
    acc = x1
    chunk = D_MODEL
    for c in range(D_FF // chunk):
        up = jnp.dot(hn, wu_ref[:, c * chunk:(c + 1) * chunk], preferred_element_type=F32)
        act = jnp.square(jnp.maximum(up, 0.0))
        acc = acc + _mm(act, wd_ref[c * chunk:(c + 1) * chunk, :])
    y_ref[...] = _rms(acc, gl_ref[...], NORM_EPS)


def _merge_ffn(x2d, yn, bv, g, o_b, gates, w_br_a, w_br_b, w_out, g_ffn, w_up, w_down,
               g_final):
    m = x2d.shape[0]
    tm = min(512, m)
    row_spec = lambda w: pl.BlockSpec((tm, w), lambda i: (i, 0))
    full = lambda a: _const_spec(a.shape)
    return pl.pallas_call(
        _merge_ffn_kernel,
        grid=(m // tm,),
        in_specs=[row_spec(D_MODEL), row_spec(W_A), row_spec(W_A), row_spec(W_A),
                  row_spec(W_B), row_spec(GATE_COLS),
                  full(w_br_a), full(w_br_b), full(w_out), full(g_ffn), full(w_up),
                  full(w_down), full(g_final)],
        out_specs=row_spec(D_MODEL),
        out_shape=jax.ShapeDtypeStruct((m, D_MODEL), F32),
        compiler_params=_params(("parallel",)),
        name="merge_ffn",
    )(x2d, yn, bv, g, o_b, gates, w_br_a, w_br_b, w_out, g_ffn, w_up, w_down, g_final)


def _rwkv_branch_prompt(pr, params, nb, seq, tc, ln_w, ln_b):
    groups = LANES // (nb * H_A)
    ni = HD_A // groups
    pr3 = pr.reshape(nb, seq, RWKV_COLS)
    scan_view = lambda x: x.reshape(1, seq, x.shape[0] // seq, LANES)
    ga, gb = _pre_scan(pr3, params, tc, 0)
    gk, gr, gv, pc, g, bv = _pre_scan(pr3, params, tc, 1)
    pc = pc.transpose(0, 2, 1, 3).reshape(seq // tc, nb, H_A, HD_A).transpose(0, 3, 1, 2)
    pc = jnp.tile(pc.reshape(1, seq // tc, HD_A, nb * H_A), (1, 1, 1, groups))
    s0 = jnp.zeros((1, ni, HD_A, LANES), F32)
    y, s = _rwkv_scan(scan_view(ga), scan_view(gb), scan_view(gk), scan_view(gr),
                      scan_view(gv), pc, s0, tc=tc)
    y = _from_scan(y[0], nb, ln_w, ln_b).reshape(nb * seq, W_A)
    s = s.reshape(ni, HD_A, groups, nb, H_A).transpose(3, 4, 0, 2, 1)
    flat = lambda x: x.reshape(nb * seq, W_A)
    return y, flat(g), flat(bv), s.reshape(nb, H_A, HD_A, HD_A)


def _rwkv_branch_sample(pr, a_s, b_s, k_s, r_s, pc, state, nb, seq, ln_w, ln_b):
    chains = nb * H_A
    ng = chains // LANES

    def lay(x):
        rows = x.shape[0] // nb
        x = x.reshape(nb, rows, H_A, HD_A).transpose(1, 3, 0, 2).reshape(rows, HD_A, ng, LANES)
        return x.transpose(2, 0, 1, 3)

    s0 = state.transpose(2, 3, 0, 1).reshape(HD_A, HD_A, ng, LANES).transpose(2, 0, 1, 3)
    y, s = _rwkv_scan(lay(a_s), lay(b_s), lay(k_s), lay(r_s), lay(pr[:, 2 * W_A:3 * W_A]),
                      lay(pc), s0, tc=seq)
    y = y.transpose(1, 2, 0, 3).reshape(seq, HD_A, nb, H_A).transpose(2, 0, 3, 1)
    s = s.transpose(1, 2, 0, 3).reshape(HD_A, HD_A, nb, H_A).transpose(2, 3, 0, 1)
    return _group_norm(y.reshape(nb * seq, W_A), ln_w, ln_b), s


def kernel(x_prompt, x_sample, cache_k, cache_v, state_rwkv, state_shift, page_table,
           w_in, mu_shift, w0, w2, a0, a2, g2, k_k, k_a, r_k, ln_x_w, ln_x_b,
           lam_q1, lam_k1, lam_q2, lam_k2, subln_g, w_br_a, w_br_b, w_out,
           g_mix, g_ffn, w_up, w_down, g_final):
    nb, seq, _ = x_prompt.shape
    bd, dseq, _ = x_sample.shape
    w_in_b = w_in[0].astype(BF16)
    wa_b, wb_b, wo_b = (w[0].astype(BF16) for w in (w_br_a, w_br_b, w_out))
    wu_b, wd_b = w_up[0].astype(BF16), w_down[0].astype(BF16)
    r_k2 = r_k.reshape(1, W_A)
    lam_rows = jnp.concatenate([lam_q1, lam_k1, lam_q2, lam_k2], axis=0)
    g_fin = g_final.reshape(1, D_MODEL)

    rwkv_params = (w0, w2[0], a0, a2[0], g2[0], k_k, k_a, r_k2)

    def finish(x2d, yn, bv, g, o_b, gates):
        return _merge_ffn(x2d, yn, bv, g, o_b, gates, wa_b, wb_b, wo_b, g_ffn, wu_b, wd_b,
                          g_fin)

    xp = x_prompt.reshape(nb * seq, D_MODEL)
    pr, q, k, v, kt, vt, gates, x_last = _inproj_prompt(xp, seq, g_mix, w_in_b, mu_shift)
    yn, g, bv, s_prompt = _rwkv_branch_prompt(pr, rwkv_params, nb, seq, SCAN_CHUNK,
                                              ln_x_w, ln_x_b)
    o_b = _attn_prompt(q, k, v, seq, lam_rows, subln_g)
    y_prompt = finish(xp, yn, bv, g, o_b, gates).reshape(nb, seq, D_MODEL)
    k_prompt = kt.reshape(1, nb, seq, H_B, 2 * HD_B)
    v_prompt = vt.reshape(1, nb, seq, H_B, 2 * HD_B)
    shift_prompt = x_last.reshape(1, nb, D_MODEL)

    xs = x_sample.reshape(bd * dseq, D_MODEL)
    shift_pad = jnp.zeros((bd, dseq, D_MODEL), F32).at[:, 0].set(state_shift[0])
    pr, q, k, v, kt, vt, gates, xn = _inproj_sample(
        xs, dseq, shift_pad.reshape(bd * dseq, D_MODEL), g_mix, w_in_b, mu_shift)
    a_s, b_s, k_s, r_s, pc, g, bv = _rwkv_pre(pr, *rwkv_params, dseq)
    yn, s_sample = _rwkv_branch_sample(pr, a_s, b_s, k_s, r_s, pc, state_rwkv[0], bd, dseq,
                                       ln_x_w, ln_x_b)
    tok = lambda t: t.reshape(bd, dseq, W_B)
    pages = lambda c: c[0].reshape(c.shape[1], PAGE_SIZE * H_B, 2 * HD_B)
    o_b = _attn_sample(tok(q), tok(k), tok(v), pages(cache_k), pages(cache_v),
                       page_table, lam_rows, subln_g)
    y_sample = finish(xs, yn, bv, g, o_b.reshape(bd * dseq, W_B), gates)
    y_sample = y_sample.reshape(bd, dseq, D_MODEL)
    k_sample = kt.reshape(1, bd, dseq, H_B, 2 * HD_B)
    v_sample = vt.reshape(1, bd, dseq, H_B, 2 * HD_B)
    shift_sample = xn.reshape(bd, dseq, D_MODEL)[:, -1][None]

    return (y_prompt, y_sample, k_prompt, v_prompt, k_sample, v_sample,
            s_prompt[None], s_sample[None], shift_prompt, shift_sample)
```

```python
import functools

import jax
import jax.numpy as jnp
import numpy as np
from jax import lax
from jax.experimental import pallas as pl
from jax.experimental.pallas import tpu as pltpu

F32 = jnp.float32
BF16 = jnp.bfloat16

D_MODEL = 1024
H_A = 8
HD_A = 64
W_A = H_A * HD_A
DECAY_LORA = 64
AAA_LORA = 64
GATE_LORA = 128
RWKV_COLS = 3 * W_A + DECAY_LORA + AAA_LORA + GATE_LORA
H_B = 4
HD_B = 64
W_B = H_B * 2 * HD_B
GATE_COLS = 2 * D_MODEL
N_COLS = RWKV_COLS + 3 * W_B + GATE_COLS
D_FF = 4 * D_MODEL
PAGE_SIZE = 128
NORM_EPS = 1e-6
GN_EPS = 64e-5
SUBLN_EPS = 1e-5
NEG = -1e30
LAM_INIT = 0.8 - 0.6 * 1.0

LANES = 128
SUBLANES = 8
MXU_TILE = 256
VMEM_LIMIT = 56 * 1024 * 1024
SCAN_CHUNK = 64
SAMPLE_SEQS_PER_STEP = 2
ATTN_HEADS_PER_STEP = 2
INPROJ_STREAMS = 2

OFF_Q = RWKV_COLS
OFF_K = OFF_Q + W_B
OFF_V = OFF_K + W_B
OFF_GA = OFF_V + W_B


def _params(sem):
    return pltpu.CompilerParams(dimension_semantics=sem, vmem_limit_bytes=VMEM_LIMIT)


def _const_spec(shape):
    nd = len(shape)
    return pl.BlockSpec(shape, lambda *_: (0,) * nd, pipeline_mode=pl.Buffered(1))


def _rms(x, g, eps):
    return x * lax.rsqrt(jnp.mean(x * x, axis=-1, keepdims=True) + eps) * g


def _mm(a, w):
    return jnp.dot(a.astype(BF16), w, preferred_element_type=F32)


def _split2(x):
    hi = x.astype(BF16)
    lo = (x - hi.astype(F32)).astype(BF16)
    return hi, lo


def _head_sum(x, head):
    width = x.shape[1]
    tile = min(MXU_TILE, width)
    r = lax.broadcasted_iota(jnp.int32, (tile, tile), 0) // head
    c = lax.broadcasted_iota(jnp.int32, (tile, tile), 1) // head
    ones = (r == c).astype(BF16)
    hi, lo = _split2(x)
    parts = []
    for s in range(width // tile):
        cols = slice(s * tile, (s + 1) * tile)
        parts.append(jnp.dot(hi[:, cols], ones, preferred_element_type=F32)
                     + jnp.dot(lo[:, cols], ones, preferred_element_type=F32))
    return jnp.concatenate(parts, axis=-1)


def _proj_tail(xn, w_ref, q_ref, k_ref, v_ref, kt_ref, vt_ref, gate_ref, r0=0):
    xb = xn.astype(BF16)
    n = xn.shape[0]
    rows = slice(r0, r0 + n)
    q_ref[rows, :] = jnp.dot(xb, w_ref[:, OFF_Q:OFF_K], preferred_element_type=F32)
    k = jnp.dot(xb, w_ref[:, OFF_K:OFF_V], preferred_element_type=F32)
    v = jnp.dot(xb, w_ref[:, OFF_V:OFF_GA], preferred_element_type=F32)
    k_ref[rows, :] = k
    v_ref[rows, :] = v
    for h in range(H_B):
        cols = slice(h * 2 * HD_B, (h + 1) * 2 * HD_B)
        kt_ref[pl.ds(r0 * H_B + h, n, stride=H_B), :] = k[:, cols]
        vt_ref[pl.ds(r0 * H_B + h, n, stride=H_B), :] = v[:, cols]
    gate_ref[rows, :] = jax.nn.sigmoid(
        jnp.dot(xb, w_ref[:, OFF_GA:N_COLS], preferred_element_type=F32))
    return xb


def _inproj_prompt_kernel(x_ref, g_ref, w_ref, mu_ref,
                          pr_ref, q_ref, k_ref, v_ref, kt_ref, vt_ref, gate_ref, xl_ref,
                          carry_ref):
    tm = x_ref.shape[0]
    n = tm // INPROJ_STREAMS

    @pl.when(pl.program_id(1) == 0)
    def _():
        carry_ref[...] = jnp.zeros_like(carry_ref)

    last = carry_ref[SUBLANES - 1:SUBLANES, :]
    for s in range(INPROJ_STREAMS):
        rows = slice(s * n, (s + 1) * n)
        xn = _rms(x_ref[rows, :], g_ref[...], NORM_EPS)
        xb = _proj_tail(xn, w_ref, q_ref, k_ref, v_ref, kt_ref, vt_ref, gate_ref, r0=s * n)
        pr = jnp.dot(xb, w_ref[:, 0:RWKV_COLS], preferred_element_type=F32)
        prev = pltpu.roll(pr, 1, 0)
        row = lax.broadcasted_iota(jnp.int32, pr.shape, 0)
        prev = jnp.where(row == 0, last, prev)
        pr_ref[rows, :] = pr + (prev - pr) * mu_ref[...]
        last = pr[n - 1:n, :]
    carry_ref[...] = pr[n - SUBLANES:n, :]
    xl_ref[0] = xn[n - 1:n, :]


def _inproj_prompt(x2d, seq, g_mix, w_in, mu):
    m = x2d.shape[0]
    nb = m // seq
    tm = min(512, seq)
    nt = seq // tm
    row_spec = lambda w: pl.BlockSpec((tm, w), lambda b, t: (b * nt + t, 0))
    head_spec = pl.BlockSpec((tm * H_B, 2 * HD_B), lambda b, t: (b * nt + t, 0))
    return pl.pallas_call(
        _inproj_prompt_kernel,
        grid=(nb, nt),
        in_specs=[row_spec(D_MODEL), _const_spec((1, D_MODEL)),
                  _const_spec((D_MODEL, N_COLS)), _const_spec((1, RWKV_COLS))],
        out_specs=[row_spec(RWKV_COLS), row_spec(W_B), row_spec(W_B), row_spec(W_B),
                   head_spec, head_spec, row_spec(GATE_COLS),
                   pl.BlockSpec((1, 1, D_MODEL), lambda b, t: (b, 0, 0))],
        out_shape=[jax.ShapeDtypeStruct((m, RWKV_COLS), F32),
                   jax.ShapeDtypeStruct((m, W_B), F32),
                   jax.ShapeDtypeStruct((m, W_B), F32),
                   jax.ShapeDtypeStruct((m, W_B), F32),
                   jax.ShapeDtypeStruct((m * H_B, 2 * HD_B), F32),
                   jax.ShapeDtypeStruct((m * H_B, 2 * HD_B), F32),
                   jax.ShapeDtypeStruct((m, GATE_COLS), F32),
                   jax.ShapeDtypeStruct((nb, 1, D_MODEL), F32)],
        scratch_shapes=[pltpu.VMEM((SUBLANES, RWKV_COLS), F32)],
        compiler_params=_params(("parallel", "arbitrary")),
        name="inproj_prompt",
    )(x2d, g_mix, w_in, mu)


def _inproj_sample_kernel(x_ref, sp_ref, g_ref, w_ref, mu_ref,
                          pr_ref, q_ref, k_ref, v_ref, kt_ref, vt_ref, gate_ref, xn_ref, *,
                          seq):
    xn = _rms(x_ref[...], g_ref[...], NORM_EPS)
    xn_ref[...] = xn
    xb = _proj_tail(xn, w_ref, q_ref, k_ref, v_ref, kt_ref, vt_ref, gate_ref)
    row = lax.broadcasted_iota(jnp.int32, xn.shape, 0)
    xprev = jnp.where(row % seq == 0, sp_ref[...], pltpu.roll(xn, 1, 0))
    w_r = w_ref[:, 0:RWKV_COLS]
    pr = jnp.dot(xb, w_r, preferred_element_type=F32)
    prev = jnp.dot(xprev.astype(BF16), w_r, preferred_element_type=F32)
    pr_ref[...] = pr + (prev - pr) * mu_ref[...]


def _inproj_sample(x2d, seq, shift_pad, g_mix, w_in, mu):
    m = x2d.shape[0]
    tm = min(256, m)
    assert tm % seq == 0 and m % tm == 0
    row_spec = lambda w: pl.BlockSpec((tm, w), lambda i: (i, 0))
    head_spec = pl.BlockSpec((tm * H_B, 2 * HD_B), lambda i: (i, 0))
    return pl.pallas_call(
        functools.partial(_inproj_sample_kernel, seq=seq),
        grid=(m // tm,),
        in_specs=[row_spec(D_MODEL), row_spec(D_MODEL), _const_spec((1, D_MODEL)),
                  _const_spec((D_MODEL, N_COLS)), _const_spec((1, RWKV_COLS))],
        out_specs=[row_spec(RWKV_COLS), row_spec(W_B), row_spec(W_B), row_spec(W_B),
                   head_spec, head_spec, row_spec(GATE_COLS), row_spec(D_MODEL)],
        out_shape=[jax.ShapeDtypeStruct((m, RWKV_COLS), F32),
                   jax.ShapeDtypeStruct((m, W_B), F32),
                   jax.ShapeDtypeStruct((m, W_B), F32),
                   jax.ShapeDtypeStruct((m, W_B), F32),
                   jax.ShapeDtypeStruct((m * H_B, 2 * HD_B), F32),
                   jax.ShapeDtypeStruct((m * H_B, 2 * HD_B), F32),
                   jax.ShapeDtypeStruct((m, GATE_COLS), F32),
                   jax.ShapeDtypeStruct((m, D_MODEL), F32)],
        compiler_params=_params(("parallel",)),
        name="inproj_sample",
    )(x2d, shift_pad, g_mix, w_in, mu)


def _dot_f32(a, b):
    a_hi, a_lo = _split2(a)
    b_hi, b_lo = _split2(b)
    d = lambda x, y: jnp.dot(x, y, preferred_element_type=F32)
    return d(a_hi, b_hi) + d(a_lo, b_hi) + d(a_hi, b_lo)


def _pre_math(pr, params, p_scr, chunk):
    w0, w2, a0, a2, g2, k_k, k_a, r_k = params
    tm = pr.shape[0]
    o_wd = 3 * W_A
    o_ad = o_wd + DECAY_LORA
    o_gd = o_ad + AAA_LORA
    k = pr[:, W_A:2 * W_A]
    wd = pr[:, o_wd:o_ad]
    ad = pr[:, o_ad:o_gd]
    gd = pr[:, o_gd:RWKV_COLS]
    z = -(w0 + _dot_f32(jnp.tanh(wd), w2))
    softplus = jnp.maximum(z, 0.0) + jnp.log1p(jnp.exp(-jnp.abs(z)))
    w = -softplus - 0.5
    log_decay = -jnp.exp(w)
    r_i = lax.broadcasted_iota(jnp.int32, (tm, tm), 0)
    c_i = lax.broadcasted_iota(jnp.int32, (tm, tm), 1)
    tri = ((r_i // chunk == c_i // chunk) & (c_i <= r_i)).astype(BF16)
    hi = log_decay.astype(BF16)
    rest = log_decay - hi.astype(F32)
    mid = rest.astype(BF16)
    lo = (rest - mid.astype(F32)).astype(BF16)
    run = lambda t: jnp.dot(tri, t, preferred_element_type=F32)
    cum = run(hi) + run(mid) + run(lo)
    p_incl = jnp.exp(cum)
    p_inv = jnp.exp(-cum)
    p_prev = jnp.exp(cum - log_decay)
    pc = []
    for s in range(W_A // LANES):
        p_scr[s] = p_incl[:, s * LANES:(s + 1) * LANES]
        pc.append(p_scr[s, pl.ds(chunk - 1, tm // chunk, stride=chunk), :])
    pc = jnp.concatenate(pc, axis=-1)
    a = jax.nn.sigmoid(a0 + _dot_f32(ad, a2))
    g = _dot_f32(jax.nn.sigmoid(gd), g2)
    kk = k * k_k
    norm = jnp.sqrt(_head_sum(kk * kk, HD_A))
    kk = kk / jnp.maximum(norm, 1e-12)
    k_mod = k * (1.0 + (a - 1.0) * k_a)
    r = pr[:, 0:W_A]
    v = pr[:, 2 * W_A:3 * W_A]
    bv = _head_sum(r * k_mod * r_k, HD_A) * v
    return -kk * p_prev, kk * a * p_inv, k_mod * p_inv, r * p_incl, pc, g, bv


def _rwkv_pre_kernel(pr_ref, *refs, chunk):
    params = tuple(ref[...] for ref in refs[:8])
    outs, p_scr = refs[8:15], refs[15]
    for out_ref, val in zip(outs, _pre_math(pr_ref[...], params, p_scr, chunk)):
        out_ref[...] = val


def _rwkv_pre(pr, w0, w2, a0, a2, g2, k_k, k_a, r_k, chunk):
    m = pr.shape[0]
    tm = min(256, m)
    assert tm % chunk == 0 and (tm // chunk) % SUBLANES == 0
    row_spec = lambda w: pl.BlockSpec((tm, w), lambda i: (i, 0))
    full = lambda a: _const_spec(a.shape)
    out = jax.ShapeDtypeStruct((m, W_A), F32)
    return pl.pallas_call(
        functools.partial(_rwkv_pre_kernel, chunk=chunk),
        grid=(m // tm,),
        in_specs=[row_spec(RWKV_COLS), full(w0), full(w2), full(a0), full(a2), full(g2),
                  full(k_k), full(k_a), full(r_k)],
        out_specs=[row_spec(W_A)] * 4 + [pl.BlockSpec((tm // chunk, W_A), lambda i: (i, 0))]
                  + [row_spec(W_A)] * 2,
        out_shape=[out] * 4 + [jax.ShapeDtypeStruct((m // chunk, W_A), F32)] + [out] * 2,
        scratch_shapes=[pltpu.VMEM((W_A // LANES, tm, LANES), F32)],
        compiler_params=_params(("parallel",)),
        name="rwkv_pre",
    )(pr, w0, w2, a0, a2, g2, k_k, k_a, r_k)


def _sublane_allsum(x):
    n = x.shape[0] // SUBLANES
    acc = x[0:SUBLANES]
    for g in range(1, n):
        acc = acc + x[g * SUBLANES:(g + 1) * SUBLANES]
    shift = SUBLANES // 2
    while shift >= 1:
        acc = acc + pltpu.roll(acc, shift, 0)
        shift //= 2
    return acc


def _sublane_groupsum(x):
    acc = x[0:SUBLANES]
    for g in range(1, x.shape[0] // SUBLANES):
        acc = acc + x[g * SUBLANES:(g + 1) * SUBLANES]
    return acc


def _fold_rows(parts, roll, where, sub):
    level = list(parts)
    shift = SUBLANES // 2
    first = True
    while len(level) > 1:
        keep = (sub & shift) != 0
        nxt = []
        for a, b in zip(level[0::2], level[1::2]):
            a2 = a + roll(a, shift)
            b2 = b + roll(b, shift)
            nxt.append(where(keep, a2, b2 if first else roll(b2, shift)))
        level, shift, first = nxt, shift // 2, False
    return level[0]


def _fold_slots():
    tags = [np.tile(np.eye(SUBLANES)[k] / SUBLANES, (SUBLANES, 1)) for k in range(SUBLANES)]
    sub = np.arange(SUBLANES)[:, None]
    out = _fold_rows(tags, lambda x, s: np.roll(x, s, axis=0), np.where, sub)
    assert np.array_equal(np.sort(out.argmax(1)), np.arange(SUBLANES))
    assert np.allclose(out.max(1), 1.0)
    return tuple(int(k) for k in out.argmax(1))


def _rwkv_scan_kernel(a_ref, b_ref, k_ref, r_ref, v_ref, pc_ref, s0_ref, y_ref, s_ref):
    tc = a_ref.shape[0]
    ni, nkey = s0_ref.shape[0], s0_ref.shape[1]
    ng = nkey // SUBLANES
    pad_rows = y_ref.shape[1] - ni

    @pl.when(pl.program_id(1) == 0)
    def _():
        s_ref[...] = s0_ref[...]

    sub = lax.broadcasted_iota(jnp.int32, (SUBLANES, LANES), 0)
    keys = pl.ds(0, nkey)
    slots = _fold_slots()

    def tile8(x):
        return jnp.concatenate([x] * ng, axis=0)

    def step(t, carry):
        def group(ig, carry2):
            y_parts = [None] * SUBLANES
            for ii in range(SUBLANES):
                i = ig * SUBLANES + ii
                s_i = s_ref[i]
                sa = _sublane_allsum(s_i * a_ref[t, keys])
                v_i = jnp.broadcast_to(v_ref[t, pl.ds(i, 1), :], (SUBLANES, LANES))
                s_new = s_i + tile8(sa) * b_ref[t, keys] + tile8(v_i) * k_ref[t, keys]
                s_ref[i] = s_new
                y_parts[slots[ii]] = _sublane_groupsum(s_new * r_ref[t, keys])
            y_acc = _fold_rows(y_parts, lambda x, s: pltpu.roll(x, s, 0), jnp.where, sub)
            y_ref[t, pl.ds(pl.multiple_of(ig * SUBLANES, SUBLANES), SUBLANES), :] = y_acc
            return carry2

        n_groups = ni // SUBLANES
        if n_groups <= 2:
            for ig in range(n_groups):
                group(ig, 0)
        else:
            lax.fori_loop(0, n_groups, group, 0)
        if pad_rows:
            y_ref[t, pl.ds(ni, pad_rows), :] = jnp.zeros((pad_rows, LANES), F32)
        return carry

    lax.fori_loop(0, tc, step, 0, unroll=2)

    def rescale(i, carry):
        s_ref[i] = s_ref[i] * pc_ref[0, keys]
        return carry

    lax.fori_loop(0, ni, rescale, 0, unroll=8)


def _rwkv_scan(a, b, k, r, v, pc, s0, tc):
    g, t, pk, _ = a.shape
    pv = v.shape[2]
    _, ni, nkey, _ = s0.shape
    x_spec = pl.BlockSpec((None, tc, pk, LANES), lambda gi, ti: (gi, ti, 0, 0))
    v_spec = pl.BlockSpec((None, tc, pv, LANES), lambda gi, ti: (gi, ti, 0, 0))
    s_spec = pl.BlockSpec((None, ni, nkey, LANES), lambda gi, ti: (gi, 0, 0, 0))
    pc_spec = pl.BlockSpec((None, 1, pc.shape[2], LANES), lambda gi, ti: (gi, ti, 0, 0))
    return pl.pallas_call(
        _rwkv_scan_kernel,
        grid=(g, t // tc),
        in_specs=[x_spec] * 4 + [v_spec, pc_spec, s_spec],
        out_specs=[v_spec, s_spec],
        out_shape=[jax.ShapeDtypeStruct((g, t, pv, LANES), F32),
                   jax.ShapeDtypeStruct((g, ni, nkey, LANES), F32)],
        compiler_params=_params(("parallel", "arbitrary")),
        name="rwkv_scan",
    )(a, b, k, r, v, pc, s0)


KEY_PITCH = HD_A + SUBLANES
TIME_TILE = LANES


def _val_pitch(n_rows):
    p = n_rows + SUBLANES
    return p if (p // SUBLANES) % 2 else p + SUBLANES


def _scan_round1(tile, nb, c_scr):
    heads_per_tile = LANES // HD_A
    for b in range(nb):
        for hp in range(W_A // LANES):
            tile_t = tile(b, hp).T
            for h2 in range(heads_per_tile):
                c = b * H_A + hp * heads_per_tile + h2
                c_scr[c * KEY_PITCH:c * KEY_PITCH + HD_A, :] = tile_t[h2 * HD_A:(h2 + 1) * HD_A]


def _scan_round2(c_scr, nb, g_ref, value_indexed, pitch):
    tt = c_scr.shape[1]
    chains = nb * H_A
    groups = LANES // chains
    chain_rows = lambda j: c_scr[pl.ds(j, chains, stride=KEY_PITCH), :]
    n_out = HD_A // groups if value_indexed else HD_A
    for n in range(n_out):
        if value_indexed:
            rows = [chain_rows(n * groups + r) for r in range(groups)]
        else:
            rows = [chain_rows(n)] * groups
        g_ref[pl.ds(n, tt, stride=pitch), :] = jnp.concatenate(rows, axis=0).T
    for n in range(n_out, pitch):
        g_ref[pl.ds(n, tt, stride=pitch), :] = jnp.zeros((tt, LANES), F32)


def _pre_scan_kernel(pr_ref, *refs, chunk, part, n_tiles):
    nb, tt, _ = pr_ref.shape
    params = tuple(ref[...] for ref in refs[:8])
    n_ops = 2 if part == 0 else 3
    outs = refs[8:-(2 * n_ops + 1)]
    sets = (refs[-(2 * n_ops + 1):-(n_ops + 1)], refs[-(n_ops + 1):-1])
    p_scr = refs[-1]
    i = pl.program_id(0)
    vp = outs[2].shape[0] // tt if part == 1 else None
    lanes = lambda hp: slice(hp * LANES, (hp + 1) * LANES)

    @pl.when(i == 0)
    def _():
        for c_scr in sets[1]:
            c_scr[...] = jnp.zeros_like(c_scr)

    def body(done, fresh):
        for idx in range(2):
            _scan_round2(done[idx], nb, outs[idx], False, KEY_PITCH)
        if part == 1:
            _scan_round2(done[2], nb, outs[2], True, vp)
        vals = [_pre_math(pr_ref[b], params, p_scr, chunk) for b in range(nb)]
        first = 0 if part == 0 else 2
        for idx in range(2):
            _scan_round1(lambda b, hp: vals[b][first + idx][:, lanes(hp)], nb, fresh[idx])
        if part == 1:
            v_cols = lambda hp: slice(2 * W_A + hp * LANES, 2 * W_A + (hp + 1) * LANES)
            _scan_round1(lambda b, hp: pr_ref[b, :, v_cols(hp)], nb, fresh[2])
            pc_ref, g_ref, bv_ref = outs[3:6]
            for b in range(nb):
                pc_ref[b] = vals[b][4]
                g_ref[b] = vals[b][5]
                bv_ref[b] = vals[b][6]

    @pl.when(i % 2 == 0)
    def _():
        body(sets[1], sets[0])

    @pl.when(i % 2 == 1)
    def _():
        body(sets[0], sets[1])


def _pre_scan(pr3, params, chunk, part):
    nb, t, _ = pr3.shape
    groups = LANES // (nb * H_A)
    vp = _val_pitch(HD_A // groups)
    n_tiles = t // TIME_TILE
    cpt = TIME_TILE // chunk
    this = lambda i: jnp.minimum(i, n_tiles - 1)
    prev = lambda i: jnp.maximum(i - 1, 0)
    g_spec = lambda pitch: pl.BlockSpec((TIME_TILE * pitch, LANES), lambda i: (prev(i), 0))
    g_shape = lambda pitch: jax.ShapeDtypeStruct((t * pitch, LANES), F32)
    nat_spec = pl.BlockSpec((nb, TIME_TILE, W_A), lambda i: (0, this(i), 0))
    nat_shape = jax.ShapeDtypeStruct((nb, t, W_A), F32)
    if part == 0:
        out_specs = [g_spec(KEY_PITCH)] * 2
        out_shape = [g_shape(KEY_PITCH)] * 2
    else:
        out_specs = [g_spec(KEY_PITCH)] * 2 + [
            g_spec(vp), pl.BlockSpec((None, nb, cpt, W_A), lambda i: (this(i), 0, 0, 0)),
            nat_spec, nat_spec]
        out_shape = [g_shape(KEY_PITCH)] * 2 + [
            g_shape(vp), jax.ShapeDtypeStruct((n_tiles, nb, cpt, W_A), F32),
            nat_shape, nat_shape]
    n_ops = 2 if part == 0 else 3
    return pl.pallas_call(
        functools.partial(_pre_scan_kernel, chunk=chunk, part=part, n_tiles=n_tiles),
        grid=(n_tiles + 1,),
        in_specs=[pl.BlockSpec((nb, TIME_TILE, RWKV_COLS), lambda i: (0, this(i), 0))]
                 + [_const_spec(p.shape) for p in params],
        out_specs=out_specs,
        out_shape=out_shape,
        scratch_shapes=[pltpu.VMEM((nb * H_A * KEY_PITCH, TIME_TILE), F32)] * (2 * n_ops)
                       + [pltpu.VMEM((W_A // LANES, TIME_TILE, LANES), F32)],
        compiler_params=_params(("arbitrary",)),
        name="pre_scan",
    )(pr3, *params)


def _from_scan_kernel(y_ref, lnw_ref, lnb_ref, o_ref, c_scr, *, pitch):
    nb, tt, _ = o_ref.shape
    chains = nb * H_A
    groups = LANES // chains
    heads_per_tile = LANES // HD_A
    for n in range(HD_A // groups):
        lanes_t = y_ref[pl.ds(n, tt, stride=pitch), :].T
        for r in range(groups):
            c_scr[pl.ds(n * groups + r, chains, stride=KEY_PITCH), :] = (
                lanes_t[r * chains:(r + 1) * chains])
    for c in range(chains):
        rows = pl.ds(c * KEY_PITCH, HD_A)
        head = c % H_A
        y = c_scr[rows, :]
        mean = _sublane_allsum(y) * (1.0 / HD_A)
        d = y - jnp.concatenate([mean] * (HD_A // SUBLANES), axis=0)
        var = _sublane_allsum(d * d) * (1.0 / HD_A)
        inv = jnp.concatenate([lax.rsqrt(var + GN_EPS)] * (HD_A // SUBLANES), axis=0)
        c_scr[rows, :] = (d * inv * lnw_ref[head * HD_A:(head + 1) * HD_A, :]
                          + lnb_ref[head * HD_A:(head + 1) * HD_A, :])
    for b in range(nb):
        for hp in range(W_A // LANES):
            c0 = b * H_A + hp * heads_per_tile
            tile_t = jnp.concatenate(
                [c_scr[(c0 + h2) * KEY_PITCH:(c0 + h2) * KEY_PITCH + HD_A, :]
                 for h2 in range(heads_per_tile)], axis=0)
            o_ref[b, :, hp * LANES:(hp + 1) * LANES] = tile_t.T


def _from_scan(y, nb, ln_w, ln_b):
    t, pitch, _ = y.shape
    cols = lambda p: jnp.broadcast_to(p.reshape(W_A, 1), (W_A, TIME_TILE))
    return pl.pallas_call(
        functools.partial(_from_scan_kernel, pitch=pitch),
        grid=(t // TIME_TILE,),
        in_specs=[pl.BlockSpec((TIME_TILE * pitch, LANES), lambda i: (i, 0)),
                  _const_spec((W_A, TIME_TILE)), _const_spec((W_A, TIME_TILE))],
        out_specs=pl.BlockSpec((nb, TIME_TILE, W_A), lambda i: (0, i, 0)),
        out_shape=jax.ShapeDtypeStruct((nb, t, W_A), F32),
        scratch_shapes=[pltpu.VMEM((nb * H_A * KEY_PITCH, TIME_TILE), F32)],
        compiler_params=_params(("parallel",)),
        name="from_scan",
    )(y.reshape(t * pitch, LANES), cols(ln_w), cols(ln_b))


def _group_norm_kernel(y_ref, lnw_ref, lnb_ref, o_ref):
    y = y_ref[...]
    mean = _head_sum(y, HD_A) * (1.0 / HD_A)
    d = y - mean
    var = _head_sum(d * d, HD_A) * (1.0 / HD_A)
    o_ref[...] = d * lax.rsqrt(var + GN_EPS) * lnw_ref[...] + lnb_ref[...]


def _group_norm(y, ln_w, ln_b):
    m = y.shape[0]
    tm = min(256, m)
    row_spec = pl.BlockSpec((tm, W_A), lambda i: (i, 0))
    vec = _const_spec((1, W_A))
    return pl.pallas_call(
        _group_norm_kernel,
        grid=(m // tm,),
        in_specs=[row_spec, vec, vec],
        out_specs=row_spec,
        out_shape=jax.ShapeDtypeStruct((m, W_A), F32),
        compiler_params=_params(("parallel",)),
        name="group_norm",
    )(y, ln_w, ln_b)


def _lam_value(lam_ref):
    lv = lam_ref[...]
    s1 = jnp.sum(lv[0:1] * lv[1:2], axis=-1, keepdims=True)
    s2 = jnp.sum(lv[2:3] * lv[3:4], axis=-1, keepdims=True)
    return jnp.exp(s1) - jnp.exp(s2) + LAM_INIT


POS_SPLIT = 16
ONES_ROWS = 16


def _attn_prompt_kernel(q_ref, k_ref, v_ref, lam_ref, sg_ref, o_ref,
                        kaug_ref, vt_ref, m_ref, acc_ref, sa_ref, sb_ref):
    qi = pl.program_id(2)
    tq = q_ref.shape[0]
    nblk = k_ref.shape[0] // tq
    width = 2 * HD_B
    n_heads = q_ref.shape[1] // width
    heads = range(n_heads)
    head_cols = lambda hh: slice(hh * width, (hh + 1) * width)
    nt = (((1,), (1,)), ((), ()))
    head0 = pl.program_id(1) * n_heads
    slope = [jnp.exp2(jnp.full((1, width), -8.0 / H_B, F32) * (head0 + hh + 1).astype(F32))
             for hh in heads]

    lane = lax.broadcasted_iota(jnp.int32, (tq, width), 1)
    pos = lax.broadcasted_iota(jnp.int32, (tq, width), 0)
    pos_hi = (pos // POS_SPLIT).astype(F32)
    pos_lo = (pos % POS_SPLIT).astype(F32)
    one = jnp.ones((tq, width), F32)

    def features(base, vals):
        out = jnp.zeros((tq, width), F32)
        for i, val in enumerate(vals):
            out = jnp.where(lane == base + i, val, out)
        return out

    own = [(lane >= c * HD_B) & (lane < (c + 1) * HD_B) for c in range(2)]
    other = [(1 - c) * HD_B for c in range(2)]

    @pl.when(qi == 0)
    def _():
        for c in range(2):
            k_feat = features(other[c], [POS_SPLIT * pos_hi, pos_lo, -POS_SPLIT * one, -one])
            for hh in heads:
                for j in range(nblk):
                    kaug_ref[hh, c, j] = jnp.where(
                        own[c], k_ref[j * tq:(j + 1) * tq, head_cols(hh)],
                        k_feat * slope[hh]).astype(BF16)
        for hh in heads:
            for j in range(nblk):
                vt_ref[hh, j] = jnp.concatenate(
                    [v_ref[j * tq:(j + 1) * tq, head_cols(hh)].T,
                     jnp.ones((ONES_ROWS, tq), F32)], axis=0).astype(BF16)

    q_feat = [features(other[c], [one, one, pos_hi, pos_lo]) for c in range(2)]
    q_aug = [[jnp.where(own[c], q_ref[:, head_cols(hh)] * (HD_B ** -0.5), q_feat[c])
              .astype(BF16) for c in range(2)] for hh in heads]

    m_ref[...] = jnp.full_like(m_ref, NEG)
    acc_ref[...] = jnp.zeros_like(acc_ref)

    def score(kj, s_ref):
        for hh in heads:
            for c in range(2):
                s_ref[hh, c] = lax.dot_general(kaug_ref[hh, c, kj], q_aug[hh][c], nt,
                                               preferred_element_type=F32)

    def col_max(s):
        chunk = s.shape[0] // SUBLANES
        part = s[0:chunk]
        for i in range(1, SUBLANES):
            part = jnp.maximum(part, s[i * chunk:(i + 1) * chunk])
        return jnp.max(part, axis=0, keepdims=True)

    def consume(kj, s_ref, masked):
        for hh in heads:
            block_bias = -slope[hh][:, 0:1] * (tq * (qi - kj)).astype(F32)
            v_t = vt_ref[hh, kj]
            for c in range(2):
                s = s_ref[hh, c]
                if masked:
                    key_i = lax.broadcasted_iota(jnp.int32, (tq, tq), 0)
                    qry_i = lax.broadcasted_iota(jnp.int32, (tq, tq), 1)
                    s = jnp.where(key_i <= qry_i, s, NEG)
                m_old = m_ref[hh, c]
                m_new = jnp.maximum(m_old, col_max(s) + block_bias)
                p = jnp.exp(s - (m_new - block_bias)).astype(BF16)
                acc_ref[hh, c] = (jnp.exp(m_old - m_new) * acc_ref[hh, c]
                                  + jnp.dot(v_t, p, preferred_element_type=F32))
                m_ref[hh, c] = m_new

    score(0, sa_ref)

    def pair(kk, carry):
        score(2 * kk + 1, sb_ref)
        consume(2 * kk, sa_ref, masked=False)
        score(2 * kk + 2, sa_ref)
        consume(2 * kk + 1, sb_ref, masked=False)
        return carry

    lax.fori_loop(0, qi // 2, pair, 0)

    @pl.when(qi % 2 == 1)
    def _():
        score(qi, sb_ref)
        consume(qi - 1, sa_ref, masked=False)
        consume(qi, sb_ref, masked=True)

    @pl.when(qi % 2 == 0)
    def _():
        consume(qi, sa_ref, masked=True)

    lam = _lam_value(lam_ref)
    for hh in heads:
        a0 = acc_ref[hh, 0]
        a1 = acc_ref[hh, 1]
        att_t = a0[0:width] / a0[width:width + 1] - lam * (a1[0:width] / a1[width:width + 1])
        o_ref[:, head_cols(hh)] = _rms(att_t.T, sg_ref[...], SUBLN_EPS) * (1.0 - LAM_INIT)


def _attn_prompt(q, k, v, seq, lam_rows, subln_g):
    m = q.shape[0]
    nb = m // seq
    tq = min(512, seq)
    assert tq % POS_SPLIT == 0 and tq // POS_SPLIT <= 256 and seq % tq == 0
    nq = seq // tq
    width = 2 * HD_B
    nh = ATTN_HEADS_PER_STEP
    q_spec = pl.BlockSpec((tq, nh * width), lambda b, h, qi: (b * nq + qi, h))
    kv_spec = pl.BlockSpec((seq, nh * width), lambda b, h, qi: (b, h))
    return pl.pallas_call(
        _attn_prompt_kernel,
        grid=(nb, H_B // nh, nq),
        in_specs=[q_spec, kv_spec, kv_spec,
                  pl.BlockSpec((4, HD_B), lambda *_: (0, 0)),
                  pl.BlockSpec((1, width), lambda *_: (0, 0))],
        out_specs=q_spec,
        out_shape=jax.ShapeDtypeStruct((m, W_B), F32),
        scratch_shapes=[pltpu.VMEM((nh, 2, nq, tq, width), BF16),
                        pltpu.VMEM((nh, nq, width + ONES_ROWS, tq), BF16),
                        pltpu.VMEM((nh, 2, 1, tq), F32),
                        pltpu.VMEM((nh, 2, width + ONES_ROWS, tq), F32),
                        pltpu.VMEM((nh, 2, tq, tq), F32),
                        pltpu.VMEM((nh, 2, tq, tq), F32)],
        compiler_params=_params(("parallel", "parallel", "arbitrary")),
        name="attn_prompt",
    )(q, k, v, lam_rows, subln_g)


def _attn_sample_kernel(pt_ref, q_ref, kn_ref, vn_ref, lam_ref, sg_ref, *rest,
                        n_pages, past, n_seq):
    del pt_ref
    o_ref, s_ref = rest[2 * n_pages * n_seq], rest[2 * n_pages * n_seq + 1]
    for i in range(n_seq):
        k_pages = rest[i * n_pages:(i + 1) * n_pages]
        v_pages = rest[(n_seq + i) * n_pages:(n_seq + i + 1) * n_pages]
        _attn_sample_one(q_ref.at[i], kn_ref.at[i], vn_ref.at[i], lam_ref, sg_ref,
                         k_pages, v_pages, o_ref.at[i], s_ref.at[i], past)


def _attn_sample_one(q_ref, kn_ref, vn_ref, lam_ref, sg_ref, k_pages, v_pages, o_ref, s_ref,
                     past):
    n_pages = len(k_pages)
    tq = q_ref.shape[0]
    rows = 2 * H_B * tq
    nt = (((1,), (1,)), ((), ()))

    q = q_ref[...] * (HD_B ** -0.5)
    q_rep = jnp.concatenate([q] * (2 * H_B), axis=0)
    r_i = lax.broadcasted_iota(jnp.int32, (rows, W_B), 0)
    c_i = lax.broadcasted_iota(jnp.int32, (rows, W_B), 1)
    q_blk = jnp.where(r_i // tq == c_i // HD_B, q_rep, 0.0).astype(BF16)

    row = lax.broadcasted_iota(jnp.int32, (rows, PAGE_SIZE), 0)
    lane = lax.broadcasted_iota(jnp.int32, (rows, PAGE_SIZE), 1)
    head = row // (2 * tq)
    slope = jnp.exp2(-8.0 * (head + 1).astype(F32) / H_B)
    q_pos = past + row % tq

    def page(ref):
        return jnp.concatenate([ref[pl.ds(h, PAGE_SIZE, stride=H_B), :] for h in range(H_B)],
                               axis=-1).astype(BF16)

    for p in range(n_pages):
        s = lax.dot_general(q_blk, page(k_pages[p]), nt, preferred_element_type=F32)
        dist = q_pos - (p * PAGE_SIZE + lane)
        s_ref[:, p * PAGE_SIZE:(p + 1) * PAGE_SIZE] = s - slope * dist.astype(F32)

    kn = jnp.concatenate(
        [kn_ref[...], jnp.zeros((PAGE_SIZE - tq, W_B), F32)], axis=0).astype(BF16)
    s = lax.dot_general(q_blk, kn, nt, preferred_element_type=F32)
    dist = q_pos - (past + lane)
    s = jnp.where((dist >= 0) & (lane < tq), s - slope * dist.astype(F32), NEG)
    s_ref[:, past:past + PAGE_SIZE] = s

    s = s_ref[...]
    mx = jnp.max(s, axis=-1, keepdims=True)
    p_un = jnp.exp(s - mx)
    pn = p_un / jnp.sum(p_un, axis=-1, keepdims=True)
    lam = _lam_value(lam_ref)
    attn = jnp.concatenate(
        [pn[(2 * h) * tq:(2 * h + 1) * tq] - lam * pn[(2 * h + 1) * tq:(2 * h + 2) * tq]
         for h in range(H_B)], axis=0).astype(BF16)

    acc = jnp.zeros((H_B * tq, W_B), F32)
    for p in range(n_pages):
        acc = acc + jnp.dot(attn[:, p * PAGE_SIZE:(p + 1) * PAGE_SIZE],
                            page(v_pages[p]), preferred_element_type=F32)
    vn = jnp.concatenate(
        [vn_ref[...], jnp.zeros((PAGE_SIZE - tq, W_B), F32)], axis=0).astype(BF16)
    acc = acc + jnp.dot(attn[:, past:past + PAGE_SIZE], vn, preferred_element_type=F32)

    width = 2 * HD_B
    outs = []
    for h in range(H_B):
        att = acc[h * tq:(h + 1) * tq, h * width:(h + 1) * width]
        outs.append(_rms(att, sg_ref[...], SUBLN_EPS) * (1.0 - LAM_INIT))
    o_ref[...] = jnp.concatenate(outs, axis=-1)


def _attn_sample(q, k_new, v_new, cache_k, cache_v, page_table, lam_rows, subln_g):
    bd, tq, _ = q.shape
    n_pages = page_table.shape[1]
    past = n_pages * PAGE_SIZE
    n_seq = SAMPLE_SEQS_PER_STEP if bd % SAMPLE_SEQS_PER_STEP == 0 else 1
    tok_spec = pl.BlockSpec((n_seq, tq, W_B), lambda b, pt: (b, 0, 0))
    page_specs = [pl.BlockSpec((None, PAGE_SIZE * H_B, 2 * HD_B),
                               lambda b, pt, i=i, p=p: (pt[(b * n_seq + i) * n_pages + p], 0, 0))
                  for i in range(n_seq) for p in range(n_pages)]
    grid_spec = pltpu.PrefetchScalarGridSpec(
        num_scalar_prefetch=1,
        grid=(bd // n_seq,),
        in_specs=[tok_spec, tok_spec, tok_spec,
                  pl.BlockSpec((4, HD_B), lambda b, pt: (0, 0)),
                  pl.BlockSpec((1, 2 * HD_B), lambda b, pt: (0, 0))]
                 + page_specs + page_specs,
        out_specs=tok_spec,
        scratch_shapes=[pltpu.VMEM((n_seq, 2 * H_B * tq, past + PAGE_SIZE), F32)],
    )
    return pl.pallas_call(
        functools.partial(_attn_sample_kernel, n_pages=n_pages, past=past, n_seq=n_seq),
        grid_spec=grid_spec,
        out_shape=jax.ShapeDtypeStruct((bd, tq, W_B), F32),
        compiler_params=_params(("parallel",)),
        name="attn_sample",
    )(page_table.reshape(-1), q, k_new, v_new, lam_rows, subln_g,
      *([cache_k] * (n_pages * n_seq)), *([cache_v] * (n_pages * n_seq)))


def _merge_ffn_kernel(x_ref, yn_ref, bv_ref, g_ref, ob_ref, gate_ref, wa_ref, wb_ref, wo_ref,
                      gf_ref, wu_ref, wd_ref, gl_ref, y_ref):
    ga = gate_ref[:, 0:D_MODEL]
    gb = gate_ref[:, D_MODEL:GATE_COLS]
    o_a = (yn_ref[...] + bv_ref[...]) * g_ref[...]
    m = ga * _mm(o_a, wa_ref[...]) + gb * _mm(ob_ref[...], wb_ref[...])
    x1 = x_ref[...] + _mm(m, wo_ref[...])
    hn = _rms(x1, gf_ref[...], NORM_EPS).astype(BF16)
    acc = x1
    chunk = D_MODEL
    for c in range(D_FF // chunk):
        up = jnp.dot(hn, wu_ref[:, c * chunk:(c + 1) * chunk], preferred_element_type=F32)
        act = jnp.square(jnp.maximum(up, 0.0))
        acc = acc + _mm(act, wd_ref[c * chunk:(c + 1) * chunk, :])
    y_ref[...] = _rms(acc, gl_ref[...], NORM_EPS)


def _merge_ffn(x2d, yn, bv, g, o_b, gates, w_br_a, w_br_b, w_out, g_ffn, w_up, w_down,
               g_final):
    m = x2d.shape[0]
    tm = min(512, m)
    row_spec = lambda w: pl.BlockSpec((tm, w), lambda i: (i, 0))
    full = lambda a: _const_spec(a.shape)
    return pl.pallas_call(
        _merge_ffn_kernel,
        grid=(m // tm,),
        in_specs=[row_spec(D_MODEL), row_spec(W_A), row_spec(W_A), row_spec(W_A),
                  row_spec(W_B), row_spec(GATE_COLS),
                  full(w_br_a), full(w_br_b), full(w_out), full(g_ffn), full(w_up),
                  full(w_down), full(g_final)],
        out_specs=row_spec(D_MODEL),
        out_shape=jax.ShapeDtypeStruct((m, D_MODEL), F32),
        compiler_params=_params(("parallel",)),
        name="merge_ffn",
    )(x2d, yn, bv, g, o_b, gates, w_br_a, w_br_b, w_out, g_ffn, w_up, w_down, g_final)


def _rwkv_branch_prompt(pr, params, nb, seq, tc, ln_w, ln_b):
    groups = LANES // (nb * H_A)
    ni = HD_A // groups
    pr3 = pr.reshape(nb, seq, RWKV_COLS)
    scan_view = lambda x: x.reshape(1, seq, x.shape[0] // seq, LANES)
    ga, gb = _pre_scan(pr3, params, tc, 0)
    gk, gr, gv, pc, g, bv = _pre_scan(pr3, params, tc, 1)
    pc = pc.transpose(0, 2, 1, 3).reshape(seq // tc, nb, H_A, HD_A).transpose(0, 3, 1, 2)
    pc = jnp.tile(pc.reshape(1, seq // tc, HD_A, nb * H_A), (1, 1, 1, groups))
    s0 = jnp.zeros((1, ni, HD_A, LANES), F32)
    y, s = _rwkv_scan(scan_view(ga), scan_view(gb), scan_view(gk), scan_view(gr),
                      scan_view(gv), pc, s0, tc=tc)
    y = _from_scan(y[0], nb, ln_w, ln_b).reshape(nb * seq, W_A)
    s = s.reshape(ni, HD_A, groups, nb, H_A).transpose(3, 4, 0, 2, 1)
    flat = lambda x: x.reshape(nb * seq, W_A)
    return y, flat(g), flat(bv), s.reshape(nb, H_A, HD_A, HD_A)


def _rwkv_branch_sample(pr, a_s, b_s, k_s, r_s, pc, state, nb, seq, ln_w, ln_b):
    chains = nb * H_A
    ng = chains // LANES

    def lay(x):
        rows = x.shape[0] // nb
        x = x.reshape(nb, rows, H_A, HD_A).transpose(1, 3, 0, 2).reshape(rows, HD_A, ng, LANES)
        return x.transpose(2, 0, 1, 3)

    s0 = state.transpose(2, 3, 0, 1).reshape(HD_A, HD_A, ng, LANES).transpose(2, 0, 1, 3)
    y, s = _rwkv_scan(lay(a_s), lay(b_s), lay(k_s), lay(r_s), lay(pr[:, 2 * W_A:3 * W_A]),
                      lay(pc), s0, tc=seq)
    y = y.transpose(1, 2, 0, 3).reshape(seq, HD_A, nb, H_A).transpose(2, 0, 3, 1)
    s = s.transpose(1, 2, 0, 3).reshape(HD_A, HD_A, nb, H_A).transpose(2, 3, 0, 1)
    return _group_norm(y.reshape(nb * seq, W_A), ln_w, ln_b), s


def kernel(x_prompt, x_sample, cache_k, cache_v, state_rwkv, state_shift, page_table,
           w_in, mu_shift, w0, w2, a0, a2, g2, k_k, k_a, r_k, ln_x_w, ln_x_b,
           lam_q1, lam_k1, lam_q2, lam_k2, subln_g, w_br_a, w_br_b, w_out,
           g_mix, g_ffn, w_up, w_down, g_final):
    nb, seq, _ = x_prompt.shape
    bd, dseq, _ = x_sample.shape
    w_in_b = w_in[0].astype(BF16)
    wa_b, wb_b, wo_b = (w[0].astype(BF16) for w in (w_br_a, w_br_b, w_out))
    wu_b, wd_b = w_up[0].astype(BF16), w_down[0].astype(BF16)
    r_k2 = r_k.reshape(1, W_A)
    lam_rows = jnp.concatenate([lam_q1, lam_k1, lam_q2, lam_k2], axis=0)
    g_fin = g_final.reshape(1, D_MODEL)

    rwkv_params = (w0, w2[0], a0, a2[0], g2[0], k_k, k_a, r_k2)

    def finish(x2d, yn, bv, g, o_b, gates):
        return _merge_ffn(x2d, yn, bv, g, o_b, gates, wa_b, wb_b, wo_b, g_ffn, wu_b, wd_b,
                          g_fin)

    xp = x_prompt.reshape(nb * seq, D_MODEL)
    pr, q, k, v, kt, vt, gates, x_last = _inproj_prompt(xp, seq, g_mix, w_in_b, mu_shift)
    yn, g, bv, s_prompt = _rwkv_branch_prompt(pr, rwkv_params, nb, seq, SCAN_CHUNK,
                                              ln_x_w, ln_x_b)
    o_b = _attn_prompt(q, k, v, seq, lam_rows, subln_g)
    y_prompt = finish(xp, yn, bv, g, o_b, gates).reshape(nb, seq, D_MODEL)
    k_prompt = kt.reshape(1, nb, seq, H_B, 2 * HD_B)
    v_prompt = vt.reshape(1, nb, seq, H_B, 2 * HD_B)
    shift_prompt = x_last.reshape(1, nb, D_MODEL)

    xs = x_sample.reshape(bd * dseq, D_MODEL)
    shift_pad = jnp.zeros((bd, dseq, D_MODEL), F32).at[:, 0].set(state_shift[0])
    pr, q, k, v, kt, vt, gates, xn = _inproj_sample(
        xs, dseq, shift_pad.reshape(bd * dseq, D_MODEL), g_mix, w_in_b, mu_shift)
    a_s, b_s, k_s, r_s, pc, g, bv = _rwkv_pre(pr, *rwkv_params, dseq)
    yn, s_sample = _rwkv_branch_sample(pr, a_s, b_s, k_s, r_s, pc, state_rwkv[0], bd, dseq,
                                       ln_x_w, ln_x_b)
    tok = lambda t: t.reshape(bd, dseq, W_B)
    pages = lambda c: c[0].reshape(c.shape[1], PAGE_SIZE * H_B, 2 * HD_B)
    o_b = _attn_sample(tok(q), tok(k), tok(v), pages(cache_k), pages(cache_v),
                       page_table, lam_rows, subln_g)
    y_sample = finish(xs, yn, bv, g, o_b.reshape(bd * dseq, W_B), gates)
    y_sample = y_sample.reshape(bd, dseq, D_MODEL)
    k_sample = kt.reshape(1, bd, dseq, H_B, 2 * HD_B)
    v_sample = vt.reshape(1, bd, dseq, H_B, 2 * HD_B)
    shift_sample = xn.reshape(bd, dseq, D_MODEL)[:, -1][None]

    return (y_prompt, y_sample, k_prompt, v_prompt, k_sample, v_sample,
            s_prompt[None], s_sample[None], shift_prompt, shift_sample)
```

```python
import functools

import jax
import jax.numpy as jnp
import numpy as np
from jax import lax
from jax.experimental import pallas as pl
from jax.experimental.pallas import tpu as pltpu

F32 = jnp.float32
BF16 = jnp.bfloat16

D_MODEL = 1024
H_A = 8
HD_A = 64
W_A = H_A * HD_A
DECAY_LORA = 64
AAA_LORA = 64
GATE_LORA = 128
RWKV_COLS = 3 * W_A + DECAY_LORA + AAA_LORA + GATE_LORA
H_B = 4
HD_B = 64
W_B = H_B * 2 * HD_B
GATE_COLS = 2 * D_MODEL
N_COLS = RWKV_COLS + 3 * W_B + GATE_COLS
D_FF = 4 * D_MODEL
PAGE_SIZE = 128
NORM_EPS = 1e-6
GN_EPS = 64e-5
SUBLN_EPS = 1e-5
NEG = -1e30
LAM_INIT = 0.8 - 0.6 * 1.0

LANES = 128
SUBLANES = 8
MXU_TILE = 256
VMEM_LIMIT = 56 * 1024 * 1024
SCAN_CHUNK = 64
SAMPLE_SEQS_PER_STEP = 2
ATTN_HEADS_PER_STEP = 2
INPROJ_STREAMS = 2

OFF_Q = RWKV_COLS
OFF_K = OFF_Q + W_B
OFF_V = OFF_K + W_B
OFF_GA = OFF_V + W_B


def _params(sem):
    return pltpu.CompilerParams(dimension_semantics=sem, vmem_limit_bytes=VMEM_LIMIT)


def _const_spec(shape):
    nd = len(shape)
    return pl.BlockSpec(shape, lambda *_: (0,) * nd, pipeline_mode=pl.Buffered(1))


def _rms(x, g, eps):
    return x * lax.rsqrt(jnp.mean(x * x, axis=-1, keepdims=True) + eps) * g


def _mm(a, w):
    return jnp.dot(a.astype(BF16), w, preferred_element_type=F32)


def _split2(x):
    hi = x.astype(BF16)
    lo = (x - hi.astype(F32)).astype(BF16)
    return hi, lo


def _head_sum(x, head):
    width = x.shape[1]
    tile = min(MXU_TILE, width)
    r = lax.broadcasted_iota(jnp.int32, (tile, tile), 0) // head
    c = lax.broadcasted_iota(jnp.int32, (tile, tile), 1) // head
    ones = (r == c).astype(BF16)
    hi, lo = _split2(x)
    parts = []
    for s in range(width // tile):
        cols = slice(s * tile, (s + 1) * tile)
        parts.append(jnp.dot(hi[:, cols], ones, preferred_element_type=F32)
                     + jnp.dot(lo[:, cols], ones, preferred_element_type=F32))
    return jnp.concatenate(parts, axis=-1)


def _proj_tail(xn, w_ref, q_ref, k_ref, v_ref, kt_ref, vt_ref, gate_ref, r0=0):
    xb = xn.astype(BF16)
    n = xn.shape[0]
    rows = slice(r0, r0 + n)
    q_ref[rows, :] = (jnp.dot(xb, w_ref[:, OFF_Q:OFF_K], preferred_element_type=F32)
                      * (HD_B ** -0.5)).astype(BF16)
    k = jnp.dot(xb, w_ref[:, OFF_K:OFF_V], preferred_element_type=F32)
    v = jnp.dot(xb, w_ref[:, OFF_V:OFF_GA], preferred_element_type=F32)
    k_ref[rows, :] = k
    v_ref[rows, :] = v
    for h in range(H_B):
        cols = slice(h * 2 * HD_B, (h + 1) * 2 * HD_B)
        kt_ref[pl.ds(r0 * H_B + h, n, stride=H_B), :] = k[:, cols]
        vt_ref[pl.ds(r0 * H_B + h, n, stride=H_B), :] = v[:, cols]
    gate_ref[rows, :] = jax.nn.sigmoid(
        jnp.dot(xb, w_ref[:, OFF_GA:N_COLS], preferred_element_type=F32)).astype(BF16)
    return xb


def _inproj_prompt_kernel(x_ref, g_ref, w_ref, mu_ref,
                          pr_ref, q_ref, k_ref, v_ref, kt_ref, vt_ref, gate_ref, xl_ref,
                          carry_ref):
    tm = x_ref.shape[0]
    n = tm // INPROJ_STREAMS

    @pl.when(pl.program_id(1) == 0)
    def _():
        carry_ref[...] = jnp.zeros_like(carry_ref)

    last = carry_ref[SUBLANES - 1:SUBLANES, :]
    for s in range(INPROJ_STREAMS):
        rows = slice(s * n, (s + 1) * n)
        xn = _rms(x_ref[rows, :], g_ref[...], NORM_EPS)
        xb = _proj_tail(xn, w_ref, q_ref, k_ref, v_ref, kt_ref, vt_ref, gate_ref, r0=s * n)
        pr = jnp.dot(xb, w_ref[:, 0:RWKV_COLS], preferred_element_type=F32)
        prev = pltpu.roll(pr, 1, 0)
        row = lax.broadcasted_iota(jnp.int32, pr.shape, 0)
        prev = jnp.where(row == 0, last, prev)
        pr_ref[rows, :] = pr + (prev - pr) * mu_ref[...]
        last = pr[n - 1:n, :]
    carry_ref[...] = pr[n - SUBLANES:n, :]
    xl_ref[0] = xn[n - 1:n, :]


def _inproj_prompt(x2d, seq, g_mix, w_in, mu):
    m = x2d.shape[0]
    nb = m // seq
    tm = min(512, seq)
    nt = seq // tm
    row_spec = lambda w: pl.BlockSpec((tm, w), lambda b, t: (b * nt + t, 0))
    head_spec = pl.BlockSpec((tm * H_B, 2 * HD_B), lambda b, t: (b * nt + t, 0))
    return pl.pallas_call(
        _inproj_prompt_kernel,
        grid=(nb, nt),
        in_specs=[row_spec(D_MODEL), _const_spec((1, D_MODEL)),
                  _const_spec((D_MODEL, N_COLS)), _const_spec((1, RWKV_COLS))],
        out_specs=[row_spec(RWKV_COLS), row_spec(W_B), row_spec(W_B), row_spec(W_B),
                   head_spec, head_spec, row_spec(GATE_COLS),
                   pl.BlockSpec((1, 1, D_MODEL), lambda b, t: (b, 0, 0))],
        out_shape=[jax.ShapeDtypeStruct((m, RWKV_COLS), F32),
                   jax.ShapeDtypeStruct((m, W_B), BF16),
                   jax.ShapeDtypeStruct((m, W_B), F32),
                   jax.ShapeDtypeStruct((m, W_B), F32),
                   jax.ShapeDtypeStruct((m * H_B, 2 * HD_B), F32),
                   jax.ShapeDtypeStruct((m * H_B, 2 * HD_B), F32),
                   jax.ShapeDtypeStruct((m, GATE_COLS), BF16),
                   jax.ShapeDtypeStruct((nb, 1, D_MODEL), F32)],
        scratch_shapes=[pltpu.VMEM((SUBLANES, RWKV_COLS), F32)],
        compiler_params=_params(("parallel", "arbitrary")),
        name="inproj_prompt",
    )(x2d, g_mix, w_in, mu)


def _inproj_sample_kernel(x_ref, sp_ref, g_ref, w_ref, mu_ref,
                          pr_ref, q_ref, k_ref, v_ref, kt_ref, vt_ref, gate_ref, xn_ref, *,
                          seq):
    xn = _rms(x_ref[...], g_ref[...], NORM_EPS)
    xn_ref[...] = xn
    xb = _proj_tail(xn, w_ref, q_ref, k_ref, v_ref, kt_ref, vt_ref, gate_ref)
    row = lax.broadcasted_iota(jnp.int32, xn.shape, 0)
    xprev = jnp.where(row % seq == 0, sp_ref[...], pltpu.roll(xn, 1, 0))
    w_r = w_ref[:, 0:RWKV_COLS]
    pr = jnp.dot(xb, w_r, preferred_element_type=F32)
    prev = jnp.dot(xprev.astype(BF16), w_r, preferred_element_type=F32)
    pr_ref[...] = pr + (prev - pr) * mu_ref[...]


def _inproj_sample(x2d, seq, shift_pad, g_mix, w_in, mu):
    m = x2d.shape[0]
    tm = min(256, m)
    assert tm % seq == 0 and m % tm == 0
    row_spec = lambda w: pl.BlockSpec((tm, w), lambda i: (i, 0))
    head_spec = pl.BlockSpec((tm * H_B, 2 * HD_B), lambda i: (i, 0))
    return pl.pallas_call(
        functools.partial(_inproj_sample_kernel, seq=seq),
        grid=(m // tm,),
        in_specs=[row_spec(D_MODEL), row_spec(D_MODEL), _const_spec((1, D_MODEL)),
                  _const_spec((D_MODEL, N_COLS)), _const_spec((1, RWKV_COLS))],
        out_specs=[row_spec(RWKV_COLS), row_spec(W_B), row_spec(W_B), row_spec(W_B),
                   head_spec, head_spec, row_spec(GATE_COLS), row_spec(D_MODEL)],
        out_shape=[jax.ShapeDtypeStruct((m, RWKV_COLS), F32),
                   jax.ShapeDtypeStruct((m, W_B), BF16),
                   jax.ShapeDtypeStruct((m, W_B), F32),
                   jax.ShapeDtypeStruct((m, W_B), F32),
                   jax.ShapeDtypeStruct((m * H_B, 2 * HD_B), F32),
                   jax.ShapeDtypeStruct((m * H_B, 2 * HD_B), F32),
                   jax.ShapeDtypeStruct((m, GATE_COLS), BF16),
                   jax.ShapeDtypeStruct((m, D_MODEL), F32)],
        compiler_params=_params(("parallel",)),
        name="inproj_sample",
    )(x2d, shift_pad, g_mix, w_in, mu)


def _dot_f32(a, b):
    a_hi, a_lo = _split2(a)
    b_hi, b_lo = _split2(b)
    d = lambda x, y: jnp.dot(x, y, preferred_element_type=F32)
    return d(a_hi, b_hi) + d(a_lo, b_hi) + d(a_hi, b_lo)


def _pre_math(pr, params, p_scr, chunk):
    w0, w2, a0, a2, g2, k_k, k_a, r_k = params
    tm = pr.shape[0]
    o_wd = 3 * W_A
    o_ad = o_wd + DECAY_LORA
    o_gd = o_ad + AAA_LORA
    k = pr[:, W_A:2 * W_A]
    wd = pr[:, o_wd:o_ad]
    ad = pr[:, o_ad:o_gd]
    gd = pr[:, o_gd:RWKV_COLS]
    z = -(w0 + _dot_f32(jnp.tanh(wd), w2))
    softplus = jnp.maximum(z, 0.0) + jnp.log1p(jnp.exp(-jnp.abs(z)))
    w = -softplus - 0.5
    log_decay = -jnp.exp(w)
    r_i = lax.broadcasted_iota(jnp.int32, (tm, tm), 0)
    c_i = lax.broadcasted_iota(jnp.int32, (tm, tm), 1)
    tri = ((r_i // chunk == c_i // chunk) & (c_i <= r_i)).astype(BF16)
    hi = log_decay.astype(BF16)
    rest = log_decay - hi.astype(F32)
    mid = rest.astype(BF16)
    lo = (rest - mid.astype(F32)).astype(BF16)
    run = lambda t: jnp.dot(tri, t, preferred_element_type=F32)
    cum = run(hi) + run(mid) + run(lo)
    p_incl = jnp.exp(cum)
    p_inv = jnp.exp(-cum)
    p_prev = jnp.exp(cum - log_decay)
    pc = []
    for s in range(W_A // LANES):
        p_scr[s] = p_incl[:, s * LANES:(s + 1) * LANES]
        pc.append(p_scr[s, pl.ds(chunk - 1, tm // chunk, stride=chunk), :])
    pc = jnp.concatenate(pc, axis=-1)
    a = jax.nn.sigmoid(a0 + _dot_f32(ad, a2))
    g = _dot_f32(jax.nn.sigmoid(gd), g2)
    kk = k * k_k
    norm = jnp.sqrt(_head_sum(kk * kk, HD_A))
    kk = kk / jnp.maximum(norm, 1e-12)
    k_mod = k * (1.0 + (a - 1.0) * k_a)
    r = pr[:, 0:W_A]
    v = pr[:, 2 * W_A:3 * W_A]
    bv = _head_sum(r * k_mod * r_k, HD_A) * v
    return -kk * p_prev, kk * a * p_inv, k_mod * p_inv, r * p_incl, pc, g, bv


def _rwkv_pre_kernel(pr_ref, *refs, chunk):
    params = tuple(ref[...] for ref in refs[:8])
    outs, p_scr = refs[8:15], refs[15]
    for out_ref, val in zip(outs, _pre_math(pr_ref[...], params, p_scr, chunk)):
        out_ref[...] = val


def _rwkv_pre(pr, w0, w2, a0, a2, g2, k_k, k_a, r_k, chunk):
    m = pr.shape[0]
    tm = min(256, m)
    assert tm % chunk == 0 and (tm // chunk) % SUBLANES == 0
    row_spec = lambda w: pl.BlockSpec((tm, w), lambda i: (i, 0))
    full = lambda a: _const_spec(a.shape)
    out = jax.ShapeDtypeStruct((m, W_A), F32)
    return pl.pallas_call(
        functools.partial(_rwkv_pre_kernel, chunk=chunk),
        grid=(m // tm,),
        in_specs=[row_spec(RWKV_COLS), full(w0), full(w2), full(a0), full(a2), full(g2),
                  full(k_k), full(k_a), full(r_k)],
        out_specs=[row_spec(W_A)] * 4 + [pl.BlockSpec((tm // chunk, W_A), lambda i: (i, 0))]
                  + [row_spec(W_A)] * 2,
        out_shape=[out] * 4 + [jax.ShapeDtypeStruct((m // chunk, W_A), F32)] + [out] * 2,
        scratch_shapes=[pltpu.VMEM((W_A // LANES, tm, LANES), F32)],
        compiler_params=_params(("parallel",)),
        name="rwkv_pre",
    )(pr, w0, w2, a0, a2, g2, k_k, k_a, r_k)


def _sublane_allsum(x):
    n = x.shape[0] // SUBLANES
    acc = x[0:SUBLANES]
    for g in range(1, n):
        acc = acc + x[g * SUBLANES:(g + 1) * SUBLANES]
    shift = SUBLANES // 2
    while shift >= 1:
        acc = acc + pltpu.roll(acc, shift, 0)
        shift //= 2
    return acc


def _sublane_groupsum(x):
    acc = x[0:SUBLANES]
    for g in range(1, x.shape[0] // SUBLANES):
        acc = acc + x[g * SUBLANES:(g + 1) * SUBLANES]
    return acc


def _fold_rows(parts, roll, where, sub):
    level = list(parts)
    shift = SUBLANES // 2
    first = True
    while len(level) > 1:
        keep = (sub & shift) != 0
        nxt = []
        for a, b in zip(level[0::2], level[1::2]):
            a2 = a + roll(a, shift)
            b2 = b + roll(b, shift)
            nxt.append(where(keep, a2, b2 if first else roll(b2, shift)))
        level, shift, first = nxt, shift // 2, False
    return level[0]


def _fold_slots():
    tags = [np.tile(np.eye(SUBLANES)[k] / SUBLANES, (SUBLANES, 1)) for k in range(SUBLANES)]
    sub = np.arange(SUBLANES)[:, None]
    out = _fold_rows(tags, lambda x, s: np.roll(x, s, axis=0), np.where, sub)
    assert np.array_equal(np.sort(out.argmax(1)), np.arange(SUBLANES))
    assert np.allclose(out.max(1), 1.0)
    return tuple(int(k) for k in out.argmax(1))


def _rwkv_scan_kernel(a_ref, b_ref, k_ref, r_ref, v_ref, pc_ref, s0_ref, y_ref, s_ref):
    tc = a_ref.shape[0]
    ni, nkey = s0_ref.shape[0], s0_ref.shape[1]
    ng = nkey // SUBLANES
    pad_rows = y_ref.shape[1] - ni

    @pl.when(pl.program_id(1) == 0)
    def _():
        s_ref[...] = s0_ref[...]

    sub = lax.broadcasted_iota(jnp.int32, (SUBLANES, LANES), 0)
    keys = pl.ds(0, nkey)
    slots = _fold_slots()

    def tile8(x):
        return jnp.concatenate([x] * ng, axis=0)

    def step(t, carry):
        def group(ig, carry2):
            y_parts = [None] * SUBLANES
            for ii in range(SUBLANES):
                i = ig * SUBLANES + ii
                s_i = s_ref[i]
                sa = _sublane_allsum(s_i * a_ref[t, keys])
                v_i = jnp.broadcast_to(v_ref[t, pl.ds(i, 1), :], (SUBLANES, LANES))
                s_new = s_i + tile8(sa) * b_ref[t, keys] + tile8(v_i) * k_ref[t, keys]
                s_ref[i] = s_new
                y_parts[slots[ii]] = _sublane_groupsum(s_new * r_ref[t, keys])
            y_acc = _fold_rows(y_parts, lambda x, s: pltpu.roll(x, s, 0), jnp.where, sub)
            y_ref[t, pl.ds(pl.multiple_of(ig * SUBLANES, SUBLANES), SUBLANES), :] = y_acc
            return carry2

        n_groups = ni // SUBLANES
        if n_groups <= 2:
            for ig in range(n_groups):
                group(ig, 0)
        else:
            lax.fori_loop(0, n_groups, group, 0, unroll=2)
        if pad_rows:
            y_ref[t, pl.ds(ni, pad_rows), :] = jnp.zeros((pad_rows, LANES), F32)
        return carry

    lax.fori_loop(0, tc, step, 0, unroll=2)

    def rescale(i, carry):
        s_ref[i] = s_ref[i] * pc_ref[0, keys]
        return carry

    lax.fori_loop(0, ni, rescale, 0, unroll=8)


def _rwkv_scan(a, b, k, r, v, pc, s0, tc):
    g, t, pk, _ = a.shape
    pv = v.shape[2]
    _, ni, nkey, _ = s0.shape
    x_spec = pl.BlockSpec((None, tc, pk, LANES), lambda gi, ti: (gi, ti, 0, 0))
    v_spec = pl.BlockSpec((None, tc, pv, LANES), lambda gi, ti: (gi, ti, 0, 0))
    s_spec = pl.BlockSpec((None, ni, nkey, LANES), lambda gi, ti: (gi, 0, 0, 0))
    pc_spec = pl.BlockSpec((None, 1, pc.shape[2], LANES), lambda gi, ti: (gi, ti, 0, 0))
    return pl.pallas_call(
        _rwkv_scan_kernel,
        grid=(g, t // tc),
        in_specs=[x_spec] * 4 + [v_spec, pc_spec, s_spec],
        out_specs=[v_spec, s_spec],
        out_shape=[jax.ShapeDtypeStruct((g, t, pv, LANES), F32),
                   jax.ShapeDtypeStruct((g, ni, nkey, LANES), F32)],
        compiler_params=_params(("parallel", "arbitrary")),
        name="rwkv_scan",
    )(a, b, k, r, v, pc, s0)


KEY_PITCH = HD_A + SUBLANES
TIME_TILE = LANES


def _val_pitch(n_rows):
    p = n_rows + SUBLANES
    return p if (p // SUBLANES) % 2 else p + SUBLANES


def _scan_round1(tile, nb, c_scr):
    heads_per_tile = LANES // HD_A
    for b in range(nb):
        for hp in range(W_A // LANES):
            tile_t = tile(b, hp).T
            for h2 in range(heads_per_tile):
                c = b * H_A + hp * heads_per_tile + h2
                c_scr[c * KEY_PITCH:c * KEY_PITCH + HD_A, :] = tile_t[h2 * HD_A:(h2 + 1) * HD_A]


def _scan_round2(c_scr, nb, g_ref, value_indexed, pitch):
    tt = c_scr.shape[1]
    chains = nb * H_A
    groups = LANES // chains
    chain_rows = lambda j: c_scr[pl.ds(j, chains, stride=KEY_PITCH), :]
    n_out = HD_A // groups if value_indexed else HD_A
    for n in range(n_out):
        if value_indexed:
            rows = [chain_rows(n * groups + r) for r in range(groups)]
        else:
            rows = [chain_rows(n)] * groups
        g_ref[pl.ds(n, tt, stride=pitch), :] = jnp.concatenate(rows, axis=0).T
    for n in range(n_out, pitch):
        g_ref[pl.ds(n, tt, stride=pitch), :] = jnp.zeros((tt, LANES), F32)


def _pre_scan_kernel(pr_ref, *refs, chunk, part):
    nb, tt, _ = pr_ref.shape
    params = tuple(ref[...] for ref in refs[:8])
    n_ops = 2 if part == 0 else 3
    outs = refs[8:-(2 * n_ops + 1)]
    sets = (refs[-(2 * n_ops + 1):-(n_ops + 1)], refs[-(n_ops + 1):-1])
    p_scr = refs[-1]
    i = pl.program_id(0)
    vp = outs[2].shape[0] // tt if part == 1 else None
    lanes = lambda hp: slice(hp * LANES, (hp + 1) * LANES)

    @pl.when(i == 0)
    def _():
        for c_scr in sets[1]:
            c_scr[...] = jnp.zeros_like(c_scr)

    def body(done, fresh):
        for idx in range(2):
            _scan_round2(done[idx], nb, outs[idx], False, KEY_PITCH)
        if part == 1:
            _scan_round2(done[2], nb, outs[2], True, vp)
        vals = [_pre_math(pr_ref[b], params, p_scr, chunk) for b in range(nb)]
        first = 0 if part == 0 else 2
        for idx in range(2):
            _scan_round1(lambda b, hp: vals[b][first + idx][:, lanes(hp)], nb, fresh[idx])
        if part == 1:
            v_cols = lambda hp: slice(2 * W_A + hp * LANES, 2 * W_A + (hp + 1) * LANES)
            _scan_round1(lambda b, hp: pr_ref[b, :, v_cols(hp)], nb, fresh[2])
            pc_ref, g_ref, bv_ref = outs[3:6]
            for b in range(nb):
                pc_ref[b] = vals[b][4]
                g_ref[b] = vals[b][5]
                bv_ref[b] = vals[b][6]

    @pl.when(i % 2 == 0)
    def _():
        body(sets[1], sets[0])

    @pl.when(i % 2 == 1)
    def _():
        body(sets[0], sets[1])


def _pre_scan(pr3, params, chunk, part):
    nb, t, _ = pr3.shape
    groups = LANES // (nb * H_A)
    vp = _val_pitch(HD_A // groups)
    n_tiles = t // TIME_TILE
    cpt = TIME_TILE // chunk
    this = lambda i: jnp.minimum(i, n_tiles - 1)
    prev = lambda i: jnp.maximum(i - 1, 0)
    g_spec = lambda pitch: pl.BlockSpec((TIME_TILE * pitch, LANES), lambda i: (prev(i), 0))
    g_shape = lambda pitch: jax.ShapeDtypeStruct((t * pitch, LANES), F32)
    nat_spec = pl.BlockSpec((nb, TIME_TILE, W_A), lambda i: (0, this(i), 0))
    nat_shape = jax.ShapeDtypeStruct((nb, t, W_A), F32)
    if part == 0:
        out_specs = [g_spec(KEY_PITCH)] * 2
        out_shape = [g_shape(KEY_PITCH)] * 2
    else:
        out_specs = [g_spec(KEY_PITCH)] * 2 + [
            g_spec(vp), pl.BlockSpec((None, nb, cpt, W_A), lambda i: (this(i), 0, 0, 0)),
            nat_spec, nat_spec]
        out_shape = [g_shape(KEY_PITCH)] * 2 + [
            g_shape(vp), jax.ShapeDtypeStruct((n_tiles, nb, cpt, W_A), F32),
            nat_shape, nat_shape]
    n_ops = 2 if part == 0 else 3
    return pl.pallas_call(
        functools.partial(_pre_scan_kernel, chunk=chunk, part=part),
        grid=(n_tiles + 1,),
        in_specs=[pl.BlockSpec((nb, TIME_TILE, RWKV_COLS), lambda i: (0, this(i), 0))]
                 + [_const_spec(p.shape) for p in params],
        out_specs=out_specs,
        out_shape=out_shape,
        scratch_shapes=[pltpu.VMEM((nb * H_A * KEY_PITCH, TIME_TILE), F32)] * (2 * n_ops)
                       + [pltpu.VMEM((W_A // LANES, TIME_TILE, LANES), F32)],
        compiler_params=_params(("arbitrary",)),
        name="pre_scan",
    )(pr3, *params)


def _from_scan_kernel(y_ref, lnw_ref, lnb_ref, o_ref, c_scr, *, pitch):
    nb, tt, _ = o_ref.shape
    chains = nb * H_A
    groups = LANES // chains
    heads_per_tile = LANES // HD_A
    for n in range(HD_A // groups):
        lanes_t = y_ref[pl.ds(n, tt, stride=pitch), :].T
        for r in range(groups):
            c_scr[pl.ds(n * groups + r, chains, stride=KEY_PITCH), :] = (
                lanes_t[r * chains:(r + 1) * chains])
    for c in range(chains):
        rows = pl.ds(c * KEY_PITCH, HD_A)
        head = c % H_A
        y = c_scr[rows, :]
        mean = _sublane_allsum(y) * (1.0 / HD_A)
        d = y - jnp.concatenate([mean] * (HD_A // SUBLANES), axis=0)
        var = _sublane_allsum(d * d) * (1.0 / HD_A)
        inv = jnp.concatenate([lax.rsqrt(var + GN_EPS)] * (HD_A // SUBLANES), axis=0)
        c_scr[rows, :] = (d * inv * lnw_ref[head * HD_A:(head + 1) * HD_A, :]
                          + lnb_ref[head * HD_A:(head + 1) * HD_A, :])
    for b in range(nb):
        for hp in range(W_A // LANES):
            c0 = b * H_A + hp * heads_per_tile
            tile_t = jnp.concatenate(
                [c_scr[(c0 + h2) * KEY_PITCH:(c0 + h2) * KEY_PITCH + HD_A, :]
                 for h2 in range(heads_per_tile)], axis=0)
            o_ref[b, :, hp * LANES:(hp + 1) * LANES] = tile_t.T


def _from_scan(y, nb, ln_w, ln_b):
    t, pitch, _ = y.shape
    cols = lambda p: jnp.broadcast_to(p.reshape(W_A, 1), (W_A, TIME_TILE))
    return pl.pallas_call(
        functools.partial(_from_scan_kernel, pitch=pitch),
        grid=(t // TIME_TILE,),
        in_specs=[pl.BlockSpec((TIME_TILE * pitch, LANES), lambda i: (i, 0)),
                  _const_spec((W_A, TIME_TILE)), _const_spec((W_A, TIME_TILE))],
        out_specs=pl.BlockSpec((nb, TIME_TILE, W_A), lambda i: (0, i, 0)),
        out_shape=jax.ShapeDtypeStruct((nb, t, W_A), F32),
        scratch_shapes=[pltpu.VMEM((nb * H_A * KEY_PITCH, TIME_TILE), F32)],
        compiler_params=_params(("parallel",)),
        name="from_scan",
    )(y.reshape(t * pitch, LANES), cols(ln_w), cols(ln_b))


def _group_norm_kernel(y_ref, lnw_ref, lnb_ref, o_ref):
    y = y_ref[...]
    mean = _head_sum(y, HD_A) * (1.0 / HD_A)
    d = y - mean
    var = _head_sum(d * d, HD_A) * (1.0 / HD_A)
    o_ref[...] = d * lax.rsqrt(var + GN_EPS) * lnw_ref[...] + lnb_ref[...]


def _group_norm(y, ln_w, ln_b):
    m = y.shape[0]
    tm = min(256, m)
    row_spec = pl.BlockSpec((tm, W_A), lambda i: (i, 0))
    vec = _const_spec((1, W_A))
    return pl.pallas_call(
        _group_norm_kernel,
        grid=(m // tm,),
        in_specs=[row_spec, vec, vec],
        out_specs=row_spec,
        out_shape=jax.ShapeDtypeStruct((m, W_A), F32),
        compiler_params=_params(("parallel",)),
        name="group_norm",
    )(y, ln_w, ln_b)


def _lam_value(lam_ref):
    lv = lam_ref[...]
    s1 = jnp.sum(lv[0:1] * lv[1:2], axis=-1, keepdims=True)
    s2 = jnp.sum(lv[2:3] * lv[3:4], axis=-1, keepdims=True)
    return jnp.exp(s1) - jnp.exp(s2) + LAM_INIT


POS_SPLIT = 16
ONES_ROWS = 16


def _attn_prompt_kernel(q_ref, k_ref, v_ref, lam_ref, sg_ref, o_ref,
                        kaug_ref, vt_ref, m_ref, acc_ref, sa_ref, sb_ref):
    qi = pl.program_id(2)
    tq = q_ref.shape[0]
    nblk = k_ref.shape[0] // tq
    width = 2 * HD_B
    n_heads = q_ref.shape[1] // width
    heads = range(n_heads)
    head_cols = lambda hh: slice(hh * width, (hh + 1) * width)
    nt = (((1,), (1,)), ((), ()))
    head0 = pl.program_id(1) * n_heads
    slope = [jnp.exp2(jnp.full((1, width), -8.0 / H_B, F32) * (head0 + hh + 1).astype(F32))
             for hh in heads]

    lane = lax.broadcasted_iota(jnp.int32, (tq, width), 1)
    pos = lax.broadcasted_iota(jnp.int32, (tq, width), 0)
    pos_hi = (pos // POS_SPLIT).astype(F32)
    pos_lo = (pos % POS_SPLIT).astype(F32)
    one = jnp.ones((tq, width), F32)

    def features(base, vals):
        out = jnp.zeros((tq, width), F32)
        for i, val in enumerate(vals):
            out = jnp.where(lane == base + i, val, out)
        return out

    own = [(lane >= c * HD_B) & (lane < (c + 1) * HD_B) for c in range(2)]
    other = [(1 - c) * HD_B for c in range(2)]

    @pl.when(qi == 0)
    def _():
        for c in range(2):
            k_feat = features(other[c], [POS_SPLIT * pos_hi, pos_lo, -POS_SPLIT * one, -one])
            for hh in heads:
                for j in range(nblk):
                    kaug_ref[hh, c, j] = jnp.where(
                        own[c], k_ref[j * tq:(j + 1) * tq, head_cols(hh)],
                        k_feat * slope[hh]).astype(BF16)
        for hh in heads:
            for j in range(nblk):
                vt_ref[hh, j] = jnp.concatenate(
                    [v_ref[j * tq:(j + 1) * tq, head_cols(hh)].T,
                     jnp.ones((ONES_ROWS, tq), F32)], axis=0).astype(BF16)

    q_feat = [features(other[c], [one, one, pos_hi, pos_lo]) for c in range(2)]
    q_aug = [[jnp.where(own[c], q_ref[:, head_cols(hh)].astype(F32), q_feat[c])
              .astype(BF16) for c in range(2)] for hh in heads]

    m_ref[...] = jnp.full_like(m_ref, NEG)
    acc_ref[...] = jnp.zeros_like(acc_ref)

    def score(kj, s_ref):
        for hh in heads:
            for c in range(2):
                s_ref[hh, c] = lax.dot_general(kaug_ref[hh, c, kj], q_aug[hh][c], nt,
                                               preferred_element_type=F32)

    def col_max(s):
        chunk = s.shape[0] // SUBLANES
        part = s[0:chunk]
        for i in range(1, SUBLANES):
            part = jnp.maximum(part, s[i * chunk:(i + 1) * chunk])
        return jnp.max(part, axis=0, keepdims=True)

    def consume(kj, s_ref, masked):
        for hh in heads:
            block_bias = -slope[hh][:, 0:1] * (tq * (qi - kj)).astype(F32)
            v_t = vt_ref[hh, kj]
            for c in range(2):
                s = s_ref[hh, c]
                if masked:
                    key_i = lax.broadcasted_iota(jnp.int32, (tq, tq), 0)
                    qry_i = lax.broadcasted_iota(jnp.int32, (tq, tq), 1)
                    s = jnp.where(key_i <= qry_i, s, NEG)
                m_old = m_ref[hh, c]
                m_new = jnp.maximum(m_old, col_max(s) + block_bias)
                p = jnp.exp(s - (m_new - block_bias)).astype(BF16)
                acc_ref[hh, c] = (jnp.exp(m_old - m_new) * acc_ref[hh, c]
                                  + jnp.dot(v_t, p, preferred_element_type=F32))
                m_ref[hh, c] = m_new

    score(0, sa_ref)

    def pair(kk, carry):
        score(2 * kk + 1, sb_ref)
        consume(2 * kk, sa_ref, masked=False)
        score(2 * kk + 2, sa_ref)
        consume(2 * kk + 1, sb_ref, masked=False)
        return carry

    lax.fori_loop(0, qi // 2, pair, 0)

    @pl.when(qi % 2 == 1)
    def _():
        score(qi, sb_ref)
        consume(qi - 1, sa_ref, masked=False)
        consume(qi, sb_ref, masked=True)

    @pl.when(qi % 2 == 0)
    def _():
        consume(qi, sa_ref, masked=True)

    lam = _lam_value(lam_ref)
    for hh in heads:
        a0 = acc_ref[hh, 0]
        a1 = acc_ref[hh, 1]
        att_t = a0[0:width] / a0[width:width + 1] - lam * (a1[0:width] / a1[width:width + 1])
        o_ref[:, head_cols(hh)] = (_rms(att_t.T, sg_ref[...], SUBLN_EPS)
                                   * (1.0 - LAM_INIT)).astype(BF16)


def _attn_prompt(q, k, v, seq, lam_rows, subln_g):
    m = q.shape[0]
    nb = m // seq
    tq = min(512, seq)
    assert tq % POS_SPLIT == 0 and tq // POS_SPLIT <= 256 and seq % tq == 0
    nq = seq // tq
    width = 2 * HD_B
    nh = ATTN_HEADS_PER_STEP
    q_spec = pl.BlockSpec((tq, nh * width), lambda b, h, qi: (b * nq + qi, h))
    kv_spec = pl.BlockSpec((seq, nh * width), lambda b, h, qi: (b, h))
    return pl.pallas_call(
        _attn_prompt_kernel,
        grid=(nb, H_B // nh, nq),
        in_specs=[q_spec, kv_spec, kv_spec,
                  pl.BlockSpec((4, HD_B), lambda *_: (0, 0)),
                  pl.BlockSpec((1, width), lambda *_: (0, 0))],
        out_specs=q_spec,
        out_shape=jax.ShapeDtypeStruct((m, W_B), BF16),
        scratch_shapes=[pltpu.VMEM((nh, 2, nq, tq, width), BF16),
                        pltpu.VMEM((nh, nq, width + ONES_ROWS, tq), BF16),
                        pltpu.VMEM((nh, 2, 1, tq), F32),
                        pltpu.VMEM((nh, 2, width + ONES_ROWS, tq), F32),
                        pltpu.VMEM((nh, 2, tq, tq), F32),
                        pltpu.VMEM((nh, 2, tq, tq), F32)],
        compiler_params=_params(("parallel", "parallel", "arbitrary")),
        name="attn_prompt",
    )(q, k, v, lam_rows, subln_g)


def _attn_sample_kernel(pt_ref, q_ref, kn_ref, vn_ref, lam_ref, sg_ref, *rest,
                        n_pages, past, n_seq):
    del pt_ref
    o_ref, s_ref = rest[2 * n_pages * n_seq], rest[2 * n_pages * n_seq + 1]
    for i in range(n_seq):
        k_pages = rest[i * n_pages:(i + 1) * n_pages]
        v_pages = rest[(n_seq + i) * n_pages:(n_seq + i + 1) * n_pages]
        _attn_sample_one(q_ref.at[i], kn_ref.at[i], vn_ref.at[i], lam_ref, sg_ref,
                         k_pages, v_pages, o_ref.at[i], s_ref.at[i], past)


def _attn_sample_one(q_ref, kn_ref, vn_ref, lam_ref, sg_ref, k_pages, v_pages, o_ref, s_ref,
                     past):
    n_pages = len(k_pages)
    tq = q_ref.shape[0]
    rows = 2 * H_B * tq
    nt = (((1,), (1,)), ((), ()))

    q = q_ref[...].astype(F32)
    q_rep = jnp.concatenate([q] * (2 * H_B), axis=0)
    r_i = lax.broadcasted_iota(jnp.int32, (rows, W_B), 0)
    c_i = lax.broadcasted_iota(jnp.int32, (rows, W_B), 1)
    q_blk = jnp.where(r_i // tq == c_i // HD_B, q_rep, 0.0).astype(BF16)

    row = lax.broadcasted_iota(jnp.int32, (rows, PAGE_SIZE), 0)
    lane = lax.broadcasted_iota(jnp.int32, (rows, PAGE_SIZE), 1)
    head = row // (2 * tq)
    slope = jnp.exp2(-8.0 * (head + 1).astype(F32) / H_B)
    q_pos = past + row % tq

    def page(ref):
        return jnp.concatenate([ref[pl.ds(h, PAGE_SIZE, stride=H_B), :] for h in range(H_B)],
                               axis=-1).astype(BF16)

    for p in range(n_pages):
        s = lax.dot_general(q_blk, page(k_pages[p]), nt, preferred_element_type=F32)
        dist = q_pos - (p * PAGE_SIZE + lane)
        s_ref[:, p * PAGE_SIZE:(p + 1) * PAGE_SIZE] = s - slope * dist.astype(F32)

    kn = jnp.concatenate(
        [kn_ref[...], jnp.zeros((PAGE_SIZE - tq, W_B), F32)], axis=0).astype(BF16)
    s = lax.dot_general(q_blk, kn, nt, preferred_element_type=F32)
    dist = q_pos - (past + lane)
    s = jnp.where((dist >= 0) & (lane < tq), s - slope * dist.astype(F32), NEG)
    s_ref[:, past:past + PAGE_SIZE] = s

    s = s_ref[...]
    mx = jnp.max(s, axis=-1, keepdims=True)
    p_un = jnp.exp(s - mx)
    pn = p_un / jnp.sum(p_un, axis=-1, keepdims=True)
    lam = _lam_value(lam_ref)
    attn = jnp.concatenate(
        [pn[(2 * h) * tq:(2 * h + 1) * tq] - lam * pn[(2 * h + 1) * tq:(2 * h + 2) * tq]
         for h in range(H_B)], axis=0).astype(BF16)

    acc = jnp.zeros((H_B * tq, W_B), F32)
    for p in range(n_pages):
        acc = acc + jnp.dot(attn[:, p * PAGE_SIZE:(p + 1) * PAGE_SIZE],
                            page(v_pages[p]), preferred_element_type=F32)
    vn = jnp.concatenate(
        [vn_ref[...], jnp.zeros((PAGE_SIZE - tq, W_B), F32)], axis=0).astype(BF16)
    acc = acc + jnp.dot(attn[:, past:past + PAGE_SIZE], vn, preferred_element_type=F32)

    width = 2 * HD_B
    outs = []
    for h in range(H_B):
        att = acc[h * tq:(h + 1) * tq, h * width:(h + 1) * width]
        outs.append(_rms(att, sg_ref[...], SUBLN_EPS) * (1.0 - LAM_INIT))
    o_ref[...] = jnp.concatenate(outs, axis=-1).astype(BF16)


def _attn_sample(q, k_new, v_new, cache_k, cache_v, page_table, lam_rows, subln_g):
    bd, tq, _ = q.shape
    n_pages = page_table.shape[1]
    past = n_pages * PAGE_SIZE
    n_seq = SAMPLE_SEQS_PER_STEP if bd % SAMPLE_SEQS_PER_STEP == 0 else 1
    tok_spec = pl.BlockSpec((n_seq, tq, W_B), lambda b, pt: (b, 0, 0))
    page_specs = [pl.BlockSpec((None, PAGE_SIZE * H_B, 2 * HD_B),
                               lambda b, pt, i=i, p=p: (pt[(b * n_seq + i) * n_pages + p], 0, 0))
                  for i in range(n_seq) for p in range(n_pages)]
    grid_spec = pltpu.PrefetchScalarGridSpec(
        num_scalar_prefetch=1,
        grid=(bd // n_seq,),
        in_specs=[tok_spec, tok_spec, tok_spec,
                  pl.BlockSpec((4, HD_B), lambda b, pt: (0, 0)),
                  pl.BlockSpec((1, 2 * HD_B), lambda b, pt: (0, 0))]
                 + page_specs + page_specs,
        out_specs=tok_spec,
        scratch_shapes=[pltpu.VMEM((n_seq, 2 * H_B * tq, past + PAGE_SIZE), F32)],
    )
    return pl.pallas_call(
        functools.partial(_attn_sample_kernel, n_pages=n_pages, past=past, n_seq=n_seq),
        grid_spec=grid_spec,
        out_shape=jax.ShapeDtypeStruct((bd, tq, W_B), BF16),
        compiler_params=_params(("parallel",)),
        name="attn_sample",
    )(page_table.reshape(-1), q, k_new, v_new, lam_rows, subln_g,
      *([cache_k] * (n_pages * n_seq)), *([cache_v] * (n_pages * n_seq)))


def _merge_ffn_kernel(x_ref, yn_ref, bv_ref, g_ref, ob_ref, gate_ref, wa_ref, wb_ref, wo_ref,
                      gf_ref, wu_ref, wd_ref, gl_ref, y_ref):
    ga = gate_ref[:, 0:D_MODEL]
    gb = gate_ref[:, D_MODEL:GATE_COLS]
    o_a = (yn_ref[...] + bv_ref[...]) * g_ref[...]
    m = ga * _mm(o_a, wa_ref[...]) + gb * _mm(ob_ref[...], wb_ref[...])
    x1 = x_ref[...] + _mm(m, wo_ref[...])
    hn = _rms(x1, gf_ref[...], NORM_EPS).astype(BF16)
    acc = x1
    chunk = D_MODEL
    for c in range(D_FF // chunk):
        up = jnp.dot(hn, wu_ref[:, c * chunk:(c + 1) * chunk], preferred_element_type=F32)
        act = jnp.square(jnp.maximum(up, 0.0))
        acc = acc + _mm(act, wd_ref[c * chunk:(c + 1) * chunk, :])
    y_ref[...] = _rms(acc, gl_ref[...], NORM_EPS)


def _merge_ffn(x2d, yn, bv, g, o_b, gates, w_br_a, w_br_b, w_out, g_ffn, w_up, w_down,
               g_final):
    m = x2d.shape[0]
    tm = min(512, m)
    row_spec = lambda w: pl.BlockSpec((tm, w), lambda i: (i, 0))
    full = lambda a: _const_spec(a.shape)
    return pl.pallas_call(
        _merge_ffn_kernel,
        grid=(m // tm,),
        in_specs=[row_spec(D_MODEL), row_spec(W_A), row_spec(W_A), row_spec(W_A),
                  row_spec(W_B), row_spec(GATE_COLS),
                  full(w_br_a), full(w_br_b), full(w_out), full(g_ffn), full(w_up),
                  full(w_down), full(g_final)],
        out_specs=row_spec(D_MODEL),
        out_shape=jax.ShapeDtypeStruct((m, D_MODEL), F32),
        compiler_params=_params(("parallel",)),
        name="merge_ffn",
    )(x2d, yn, bv, g, o_b, gates, w_br_a, w_br_b, w_out, g_ffn, w_up, w_down, g_final)


def _rwkv_branch_prompt(pr, params, nb, seq, tc, ln_w, ln_b):
    groups = LANES // (nb * H_A)
    ni = HD_A // groups
    pr3 = pr.reshape(nb, seq, RWKV_COLS)
    scan_view = lambda x: x.reshape(1, seq, x.shape[0] // seq, LANES)
    ga, gb = _pre_scan(pr3, params, tc, 0)
    gk, gr, gv, pc, g, bv = _pre_scan(pr3, params, tc, 1)
    pc = pc.transpose(0, 2, 1, 3).reshape(seq // tc, nb, H_A, HD_A).transpose(0, 3, 1, 2)
    pc = jnp.tile(pc.reshape(1, seq // tc, HD_A, nb * H_A), (1, 1, 1, groups))
    s0 = jnp.zeros((1, ni, HD_A, LANES), F32)
    y, s = _rwkv_scan(scan_view(ga), scan_view(gb), scan_view(gk), scan_view(gr),
                      scan_view(gv), pc, s0, tc=tc)
    y = _from_scan(y[0], nb, ln_w, ln_b).reshape(nb * seq, W_A)
    s = s.reshape(ni, HD_A, groups, nb, H_A).transpose(3, 4, 0, 2, 1)
    flat = lambda x: x.reshape(nb * seq, W_A)
    return y, flat(g), flat(bv), s.reshape(nb, H_A, HD_A, HD_A)


def _rwkv_branch_sample(pr, a_s, b_s, k_s, r_s, pc, state, nb, seq, ln_w, ln_b):
    chains = nb * H_A
    ng = chains // LANES

    def lay(x):
        rows = x.shape[0] // nb
        x = x.reshape(nb, rows, H_A, HD_A).transpose(1, 3, 0, 2).reshape(rows, HD_A, ng, LANES)
        return x.transpose(2, 0, 1, 3)

    s0 = state.transpose(2, 3, 0, 1).reshape(HD_A, HD_A, ng, LANES).transpose(2, 0, 1, 3)
    y, s = _rwkv_scan(lay(a_s), lay(b_s), lay(k_s), lay(r_s), lay(pr[:, 2 * W_A:3 * W_A]),
                      lay(pc), s0, tc=seq)
    y = y.transpose(1, 2, 0, 3).reshape(seq, HD_A, nb, H_A).transpose(2, 0, 3, 1)
    s = s.transpose(1, 2, 0, 3).reshape(HD_A, HD_A, nb, H_A).transpose(2, 3, 0, 1)
    return _group_norm(y.reshape(nb * seq, W_A), ln_w, ln_b), s


def kernel(x_prompt, x_sample, cache_k, cache_v, state_rwkv, state_shift, page_table,
           w_in, mu_shift, w0, w2, a0, a2, g2, k_k, k_a, r_k, ln_x_w, ln_x_b,
           lam_q1, lam_k1, lam_q2, lam_k2, subln_g, w_br_a, w_br_b, w_out,
           g_mix, g_ffn, w_up, w_down, g_final):
    nb, seq, _ = x_prompt.shape
    bd, dseq, _ = x_sample.shape
    w_in_b = w_in[0].astype(BF16)
    wa_b, wb_b, wo_b = (w[0].astype(BF16) for w in (w_br_a, w_br_b, w_out))
    wu_b, wd_b = w_up[0].astype(BF16), w_down[0].astype(BF16)
    r_k2 = r_k.reshape(1, W_A)
    lam_rows = jnp.concatenate([lam_q1, lam_k1, lam_q2, lam_k2], axis=0)
    g_fin = g_final.reshape(1, D_MODEL)

    rwkv_params = (w0, w2[0], a0, a2[0], g2[0], k_k, k_a, r_k2)

    def finish(x2d, yn, bv, g, o_b, gates):
        return _merge_ffn(x2d, yn, bv, g, o_b, gates, wa_b, wb_b, wo_b, g_ffn, wu_b, wd_b,
                          g_fin)

    xp = x_prompt.reshape(nb * seq, D_MODEL)
    pr, q, k, v, kt, vt, gates, x_last = _inproj_prompt(xp, seq, g_mix, w_in_b, mu_shift)
    yn, g, bv, s_prompt = _rwkv_branch_prompt(pr, rwkv_params, nb, seq, SCAN_CHUNK,
                                              ln_x_w, ln_x_b)
    o_b = _attn_prompt(q, k, v, seq, lam_rows, subln_g)
    y_prompt = finish(xp, yn, bv, g, o_b, gates).reshape(nb, seq, D_MODEL)
    k_prompt = kt.reshape(1, nb, seq, H_B, 2 * HD_B)
    v_prompt = vt.reshape(1, nb, seq, H_B, 2 * HD_B)
    shift_prompt = x_last.reshape(1, nb, D_MODEL)

    xs = x_sample.reshape(bd * dseq, D_MODEL)
    shift_pad = jnp.zeros((bd, dseq, D_MODEL), F32).at[:, 0].set(state_shift[0])
    pr, q, k, v, kt, vt, gates, xn = _inproj_sample(
        xs, dseq, shift_pad.reshape(bd * dseq, D_MODEL), g_mix, w_in_b, mu_shift)
    a_s, b_s, k_s, r_s, pc, g, bv = _rwkv_pre(pr, *rwkv_params, dseq)
    yn, s_sample = _rwkv_branch_sample(pr, a_s, b_s, k_s, r_s, pc, state_rwkv[0], bd, dseq,
                                       ln_x_w, ln_x_b)
    tok = lambda t: t.reshape(bd, dseq, W_B)
    pages = lambda c: c[0].reshape(c.shape[1], PAGE_SIZE * H_B, 2 * HD_B)
    o_b = _attn_sample(tok(q), tok(k), tok(v), pages(cache_k), pages(cache_v),
                       page_table, lam_rows, subln_g)
    y_sample = finish(xs, yn, bv, g, o_b.reshape(bd * dseq, W_B), gates)
    y_sample = y_sample.reshape(bd, dseq, D_MODEL)
    k_sample = kt.reshape(1, bd, dseq, H_B, 2 * HD_B)
    v_sample = vt.reshape(1, bd, dseq, H_B, 2 * HD_B)
    shift_sample = xn.reshape(bd, dseq, D_MODEL)[:, -1][None]

    return (y_prompt, y_sample, k_prompt, v_prompt, k_sample, v_sample,
            s_prompt[None], s_sample[None], shift_prompt, shift_sample)
```
